```python
import jax, jax.numpy as jnp
from jax import lax
import numpy as np

D_MODEL = 1024
BATCH = 32
SEQ = 2048
DEPTH = 1

GLA_HEADS = 4
GLA_V_W = D_MODEL // 2
GLA_DV = GLA_V_W // GLA_HEADS
GLA_DK = GLA_DV // 2
GLA_QK_W = GLA_HEADS * GLA_DK
GLA_GATE_RANK = 16
GLA_LOGIT_NORM = 16.0
GLA_CHUNK = 64
RWKV_HEAD = 64
RWKV_W = D_MODEL // 2
RWKV_HEADS = RWKV_W // RWKV_HEAD
RWKV_DECAY_LORA = 64
RWKV_AAA_LORA = 64
RWKV_GATE_LORA = 128
RWKV_GN_EPS = RWKV_HEAD * 1e-5
GLA_SPLITS = (GLA_QK_W, GLA_QK_W, GLA_V_W, GLA_V_W, GLA_GATE_RANK, GLA_GATE_RANK)
RWKV_SPLITS = (RWKV_W, RWKV_W, RWKV_W, RWKV_DECAY_LORA, RWKV_AAA_LORA, RWKV_GATE_LORA)
GLA_PROJ_W = sum(GLA_SPLITS)
RWKV_PROJ_W = sum(RWKV_SPLITS)
GATE_PROJ_W = 2 * D_MODEL
N_PROJ = GLA_PROJ_W + RWKV_PROJ_W + GATE_PROJ_W
D_FF = ((8 * D_MODEL // 3) + 63) // 64 * 64
CONV_W = 3
NORM_EPS = 1e-6
HEAD_NORM_EPS = 1e-5

kernel_name = "bidir_gla_rwkv7_gated_hybrid"


def _rmsnorm(x, g):
    xf = x.astype(jnp.float32)
    y = xf * lax.rsqrt(jnp.mean(xf * xf, axis=-1, keepdims=True) + NORM_EPS)
    return (y * g.astype(jnp.float32)).astype(x.dtype)


def _shift_prev(u):
    return jnp.pad(u[:, :-1], ((0, 0), (1, 0), (0, 0)))


def _shift_next(u):
    return jnp.pad(u[:, 1:], ((0, 0), (0, 1), (0, 0)))


def _split(t, sizes):
    return jnp.split(t, np.cumsum(sizes)[:-1].tolist(), axis=-1)


def _gla_chunked(q, k, v, log_a):
    f32 = jnp.float32
    B_, T, H, K = q.shape
    V = v.shape[-1]
    C = GLA_CHUNK
    n = T // C
    q = q.astype(f32).reshape(B_, n, C, H, K)
    k = k.astype(f32).reshape(B_, n, C, H, K)
    v = v.astype(f32).reshape(B_, n, C, H, V)
    b = jnp.cumsum(log_a.astype(f32).reshape(B_, n, C, H, K), axis=2)
    b_ref = b[:, :, C // 2:C // 2 + 1]
    qi = q * jnp.exp(b - b_ref)
    ki = k * jnp.exp(b_ref - b)
    A = jnp.einsum('bnchk,bnshk->bnhcs', qi, ki)
    A = jnp.where(jnp.tril(jnp.ones((C, C), dtype=bool)), A, 0.0)
    o_intra = jnp.einsum('bnhcs,bnshv->bnchv', A, v)
    b_last = b[:, :, -1:]
    kv = jnp.einsum('bnchk,bnchv->bnhkv', k * jnp.exp(b_last - b), v)
    decay = jnp.exp(b_last[:, :, 0])

    def step(S, inp):
        d, u = inp
        return S * d[..., None] + u, S

    _, S_prev = lax.scan(step, jnp.zeros((B_, H, K, V), f32),
                         (jnp.moveaxis(decay, 1, 0), jnp.moveaxis(kv, 1, 0)))
    S_prev = jnp.moveaxis(S_prev, 0, 1)
    o_inter = jnp.einsum('bnchk,bnhkv->bnchv', q * jnp.exp(b), S_prev)
    return (o_intra + o_inter).reshape(B_, T, H, V)


def _gla_branch(p, wa2_f, ba_f, wa2_b, ba_b, norm_g, proj):
    f32 = jnp.float32
    B_, T, _ = p.shape
    q, k, v, og, af, ab = _split(p, GLA_SPLITS)
    q = q.reshape(B_, T, GLA_HEADS, GLA_DK) * (GLA_DK ** -0.5)
    k = k.reshape(B_, T, GLA_HEADS, GLA_DK)
    v = v.reshape(B_, T, GLA_HEADS, GLA_DV)
    la_f = (jax.nn.log_sigmoid((af @ wa2_f + ba_f).astype(f32)) / GLA_LOGIT_NORM).reshape(B_, T, GLA_HEADS, GLA_DK)
    la_b = (jax.nn.log_sigmoid((ab @ wa2_b + ba_b).astype(f32)) / GLA_LOGIT_NORM).reshape(B_, T, GLA_HEADS, GLA_DK)
    flip = lambda t: jnp.flip(t, axis=1)
    o = _gla_chunked(q, k, v, la_f) + flip(_gla_chunked(flip(q), flip(k), flip(v), flip(la_b)))
    o = o * lax.rsqrt(jnp.mean(o * o, axis=-1, keepdims=True) + HEAD_NORM_EPS)
    o = (o.reshape(B_, T, GLA_V_W) * norm_g.astype(f32)).astype(p.dtype)
    o = o * jax.nn.silu(og)
    return o @ proj


def _rwkv7_step(S, inp):
    r, w, k, v, a, b = inp
    sa = jnp.einsum('bhvk,bhk->bhv', S, a)
    S = S * w[:, :, None, :] + sa[..., None] * b[:, :, None, :] + v[..., None] * k[:, :, None, :]
    y = jnp.einsum('bhvk,bhk->bhv', S, r)
    return S, y


def _rwkv_branch(p, mu_prev, mu_next, w0_f, w2_f, w0_b, w2_b, a0, a2, g2, k_k, k_a, r_k, ln_w, ln_b, proj):
    f32 = jnp.float32
    B_, T, _ = p.shape
    s = p + mu_prev * (_shift_prev(p) - p) + mu_next * (_shift_next(p) - p)
    r, k, v, wl, al, gl = _split(s, RWKV_SPLITS)
    tw = jnp.tanh(wl)

    def decay(w0, w2):
        w = -jax.nn.softplus(-(w0 + tw @ w2).astype(f32)) - 0.5
        return jnp.exp(-jnp.exp(w))

    a = jax.nn.sigmoid(a0 + al @ a2)
    g = jax.nn.sigmoid(gl) @ g2
    heads = lambda t: t.reshape(B_, T, RWKV_HEADS, RWKV_HEAD).astype(f32)
    kk = heads(k * k_k)
    kk = kk / jnp.maximum(jnp.sqrt(jnp.sum(kk * kk, axis=-1, keepdims=True)), 1e-12)
    k = k * (1.0 + (a - 1.0) * k_a)
    rh, kh, vh, ah = heads(r), heads(k), heads(v), heads(a)
    tm = lambda t: jnp.moveaxis(t, 1, 0)
    r_s, k_s, v_s, a_s, b_s = tm(rh), tm(kh), tm(vh), tm(-kk), tm(kk * ah)
    S0 = jnp.zeros((B_, RWKV_HEADS, RWKV_HEAD, RWKV_HEAD), f32)
    _, y_f = lax.scan(_rwkv7_step, S0, (r_s, tm(heads(decay(w0_f, w2_f))), k_s, v_s, a_s, b_s))
    _, y_b = lax.scan(_rwkv7_step, S0, (r_s, tm(heads(decay(w0_b, w2_b))), k_s, v_s, a_s, b_s), reverse=True)
    y = jnp.moveaxis(y_f + y_b, 0, 1)
    mu = jnp.mean(y, axis=-1, keepdims=True)
    var = jnp.mean(jnp.square(y - mu), axis=-1, keepdims=True)
    y = ((y - mu) * lax.rsqrt(var + RWKV_GN_EPS)).reshape(B_, T, RWKV_W) * ln_w.astype(f32) + ln_b.astype(f32)
    bonus = (jnp.sum(rh * kh * r_k.astype(f32), axis=-1, keepdims=True) * vh).reshape(B_, T, RWKV_W)
    o = (y + bonus).astype(p.dtype) * g
    return o @ proj


def _fwd_setup_inputs(seed: int = 0) -> dict:
    key = jax.random.key(seed)
    ks = jax.random.split(key, 32)
    L, D = DEPTH, D_MODEL
    f32 = jnp.float32
    nrm = lambda k, shape, scale: jax.random.normal(k, shape, f32) * scale
    uni = lambda k, shape: jax.random.uniform(k, shape, f32, 0.0, 0.5)
    centre = jnp.array([0.0, 1.0, 0.0], f32)[None, :, None]
    return {
        "x": nrm(ks[0], (BATCH, SEQ, D), 1.0),
        "norm1_g": 1.0 + nrm(ks[1], (L, D), 0.02),
        "w_in": nrm(ks[2], (L, D, N_PROJ), D ** -0.5),
        "gla_wa2_f": nrm(ks[3], (L, GLA_GATE_RANK, GLA_QK_W), GLA_GATE_RANK ** -0.5),
        "gla_ba_f": 1.0 + nrm(ks[4], (L, GLA_QK_W), 0.5),
        "gla_wa2_b": nrm(ks[5], (L, GLA_GATE_RANK, GLA_QK_W), GLA_GATE_RANK ** -0.5),
        "gla_ba_b": 1.0 + nrm(ks[6], (L, GLA_QK_W), 0.5),
        "gla_norm_g": 1.0 + nrm(ks[7], (L, GLA_V_W), 0.02),
        "gla_proj": nrm(ks[8], (L, GLA_V_W, D), GLA_V_W ** -0.5),
        "rwkv_mu_prev": uni(ks[9], (L, RWKV_PROJ_W)),
        "rwkv_mu_next": uni(ks[10], (L, RWKV_PROJ_W)),
        "rwkv_w0_f": -1.0 + nrm(ks[11], (L, RWKV_W), 0.5),
        "rwkv_w2_f": nrm(ks[12], (L, RWKV_DECAY_LORA, RWKV_W), RWKV_DECAY_LORA ** -0.5),
        "rwkv_w0_b": -1.0 + nrm(ks[13], (L, RWKV_W), 0.5),
        "rwkv_w2_b": nrm(ks[14], (L, RWKV_DECAY_LORA, RWKV_W), RWKV_DECAY_LORA ** -0.5),
        "rwkv_a0": nrm(ks[15], (L, RWKV_W), 0.1),
        "rwkv_a2": nrm(ks[16], (L, RWKV_AAA_LORA, RWKV_W), RWKV_AAA_LORA ** -0.5),
        "rwkv_g2": nrm(ks[17], (L, RWKV_GATE_LORA, RWKV_W), RWKV_GATE_LORA ** -0.5),
        "rwkv_k_k": 0.85 + nrm(ks[18], (L, RWKV_W), 0.05),
        "rwkv_k_a": 1.0 + nrm(ks[19], (L, RWKV_W), 0.05),
        "rwkv_r_k": nrm(ks[20], (L, RWKV_HEADS, RWKV_HEAD), 0.1),
        "rwkv_ln_w": 1.0 + nrm(ks[21], (L, RWKV_W), 0.02),
        "rwkv_ln_b": nrm(ks[22], (L, RWKV_W), 0.02),
        "rwkv_proj": nrm(ks[23], (L, RWKV_W, D), RWKV_W ** -0.5),
        "w_out": nrm(ks[24], (L, D, D), D ** -0.5),
        "norm2_g": 1.0 + nrm(ks[25], (L, D), 0.02),
        "ffn_up": nrm(ks[26], (L, D, 2 * D_FF), D ** -0.5),
        "ffn_conv_w": centre + nrm(ks[27], (L, CONV_W, 2 * D_FF), 0.2),
        "ffn_conv_b": nrm(ks[28], (L, 2 * D_FF), 0.02),
        "ffn_down": nrm(ks[29], (L, D_FF, D), D_FF ** -0.5),
        "norm_f_g": 1.0 + nrm(ks[30], (D,), 0.02),
    }


def _fwd_reference(x, norm1_g, w_in, gla_wa2_f, gla_ba_f, gla_wa2_b, gla_ba_b, gla_norm_g, gla_proj,
              rwkv_mu_prev, rwkv_mu_next, rwkv_w0_f, rwkv_w2_f, rwkv_w0_b, rwkv_w2_b, rwkv_a0, rwkv_a2,
              rwkv_g2, rwkv_k_k, rwkv_k_a, rwkv_r_k, rwkv_ln_w, rwkv_ln_b, rwkv_proj, w_out,
              norm2_g, ffn_up, ffn_conv_w, ffn_conv_b, ffn_down, norm_f_g):
    for l in range(DEPTH):
        h = _rmsnorm(x, norm1_g[l])
        p = h @ w_in[l]
        p_gla, p_rwkv, p_gate = jnp.split(p, [GLA_PROJ_W, GLA_PROJ_W + RWKV_PROJ_W], axis=-1)
        y_a = _gla_branch(p_gla, gla_wa2_f[l], gla_ba_f[l], gla_wa2_b[l], gla_ba_b[l],
                          gla_norm_g[l], gla_proj[l])
        y_b = _rwkv_branch(p_rwkv, rwkv_mu_prev[l], rwkv_mu_next[l], rwkv_w0_f[l], rwkv_w2_f[l],
                           rwkv_w0_b[l], rwkv_w2_b[l], rwkv_a0[l], rwkv_a2[l], rwkv_g2[l],
                           rwkv_k_k[l], rwkv_k_a[l], rwkv_r_k[l], rwkv_ln_w[l], rwkv_ln_b[l], rwkv_proj[l])
        gate_a, gate_b = jnp.split(p_gate, 2, axis=-1)
        merged = jax.nn.sigmoid(gate_a) * y_a + jax.nn.sigmoid(gate_b) * y_b
        x = x + merged @ w_out[l]
        h2 = _rmsnorm(x, norm2_g[l])
        u = h2 @ ffn_up[l]
        cw = ffn_conv_w[l]
        u = cw[0] * _shift_prev(u) + cw[1] * u + cw[2] * _shift_next(u) + ffn_conv_b[l]
        u_gate, u_val = jnp.split(u, 2, axis=-1)
        x = x + (jax.nn.silu(u_gate) * u_val) @ ffn_down[l]
    return _rmsnorm(x, norm_f_g)


import jax as _jax
import jax.numpy as _jnp

TWIN_FORMAT = 'train_step'
FWD_PARAMS = ['x', 'norm1_g', 'w_in', 'gla_wa2_f', 'gla_ba_f', 'gla_wa2_b', 'gla_ba_b', 'gla_norm_g', 'gla_proj', 'rwkv_mu_prev', 'rwkv_mu_next', 'rwkv_w0_f', 'rwkv_w2_f', 'rwkv_w0_b', 'rwkv_w2_b', 'rwkv_a0', 'rwkv_a2', 'rwkv_g2', 'rwkv_k_k', 'rwkv_k_a', 'rwkv_r_k', 'rwkv_ln_w', 'rwkv_ln_b', 'rwkv_proj', 'w_out', 'norm2_g', 'ffn_up', 'ffn_conv_w', 'ffn_conv_b', 'ffn_down', 'norm_f_g']
TWIN_WEIGHTS = ['norm1_g', 'w_in', 'gla_wa2_f', 'gla_ba_f', 'gla_wa2_b', 'gla_ba_b', 'gla_norm_g', 'gla_proj', 'rwkv_mu_prev', 'rwkv_mu_next', 'rwkv_w0_f', 'rwkv_w2_f', 'rwkv_w0_b', 'rwkv_w2_b', 'rwkv_a0', 'rwkv_a2', 'rwkv_g2', 'rwkv_k_k', 'rwkv_k_a', 'rwkv_r_k', 'rwkv_ln_w', 'rwkv_ln_b', 'rwkv_proj', 'w_out', 'norm2_g', 'ffn_up', 'ffn_conv_w', 'ffn_conv_b', 'ffn_down', 'norm_f_g']
TWIN_DIFF_INPUT = 'x'
TWIN_INPUTS = ['x', 'norm1_g', 'w_in', 'gla_wa2_f', 'gla_ba_f', 'gla_wa2_b', 'gla_ba_b', 'gla_norm_g', 'gla_proj', 'rwkv_mu_prev', 'rwkv_mu_next', 'rwkv_w0_f', 'rwkv_w2_f', 'rwkv_w0_b', 'rwkv_w2_b', 'rwkv_a0', 'rwkv_a2', 'rwkv_g2', 'rwkv_k_k', 'rwkv_k_a', 'rwkv_r_k', 'rwkv_ln_w', 'rwkv_ln_b', 'rwkv_proj', 'w_out', 'norm2_g', 'ffn_up', 'ffn_conv_w', 'ffn_conv_b', 'ffn_down', 'norm_f_g', 'loss_target', 'm_norm1_g', 'm_w_in', 'm_gla_wa2_f', 'm_gla_ba_f', 'm_gla_wa2_b', 'm_gla_ba_b', 'm_gla_norm_g', 'm_gla_proj', 'm_rwkv_mu_prev', 'm_rwkv_mu_next', 'm_rwkv_w0_f', 'm_rwkv_w2_f', 'm_rwkv_w0_b', 'm_rwkv_w2_b', 'm_rwkv_a0', 'm_rwkv_a2', 'm_rwkv_g2', 'm_rwkv_k_k', 'm_rwkv_k_a', 'm_rwkv_r_k', 'm_rwkv_ln_w', 'm_rwkv_ln_b', 'm_rwkv_proj', 'm_w_out', 'm_norm2_g', 'm_ffn_up', 'm_ffn_conv_w', 'm_ffn_conv_b', 'm_ffn_down', 'm_norm_f_g', 'v_norm1_g', 'v_w_in', 'v_gla_wa2_f', 'v_gla_ba_f', 'v_gla_wa2_b', 'v_gla_ba_b', 'v_gla_norm_g', 'v_gla_proj', 'v_rwkv_mu_prev', 'v_rwkv_mu_next', 'v_rwkv_w0_f', 'v_rwkv_w2_f', 'v_rwkv_w0_b', 'v_rwkv_w2_b', 'v_rwkv_a0', 'v_rwkv_a2', 'v_rwkv_g2', 'v_rwkv_k_k', 'v_rwkv_k_a', 'v_rwkv_r_k', 'v_rwkv_ln_w', 'v_rwkv_ln_b', 'v_rwkv_proj', 'v_w_out', 'v_norm2_g', 'v_ffn_up', 'v_ffn_conv_w', 'v_ffn_conv_b', 'v_ffn_down', 'v_norm_f_g']
TWIN_OUTPUTS = ['loss', 'grad_x', 'grad_norm1_g', 'grad_w_in', 'grad_gla_wa2_f', 'grad_gla_ba_f', 'grad_gla_wa2_b', 'grad_gla_ba_b', 'grad_gla_norm_g', 'grad_gla_proj', 'grad_rwkv_mu_prev', 'grad_rwkv_mu_next', 'grad_rwkv_w0_f', 'grad_rwkv_w2_f', 'grad_rwkv_w0_b', 'grad_rwkv_w2_b', 'grad_rwkv_a0', 'grad_rwkv_a2', 'grad_rwkv_g2', 'grad_rwkv_k_k', 'grad_rwkv_k_a', 'grad_rwkv_r_k', 'grad_rwkv_ln_w', 'grad_rwkv_ln_b', 'grad_rwkv_proj', 'grad_w_out', 'grad_norm2_g', 'grad_ffn_up', 'grad_ffn_conv_w', 'grad_ffn_conv_b', 'grad_ffn_down', 'grad_norm_f_g', 'delta_norm1_g', 'delta_w_in', 'delta_gla_wa2_f', 'delta_gla_ba_f', 'delta_gla_wa2_b', 'delta_gla_ba_b', 'delta_gla_norm_g', 'delta_gla_proj', 'delta_rwkv_mu_prev', 'delta_rwkv_mu_next', 'delta_rwkv_w0_f', 'delta_rwkv_w2_f', 'delta_rwkv_w0_b', 'delta_rwkv_w2_b', 'delta_rwkv_a0', 'delta_rwkv_a2', 'delta_rwkv_g2', 'delta_rwkv_k_k', 'delta_rwkv_k_a', 'delta_rwkv_r_k', 'delta_rwkv_ln_w', 'delta_rwkv_ln_b', 'delta_rwkv_proj', 'delta_w_out', 'delta_norm2_g', 'delta_ffn_up', 'delta_ffn_conv_w', 'delta_ffn_conv_b', 'delta_ffn_down', 'delta_norm_f_g', 'new_m_norm1_g', 'new_m_w_in', 'new_m_gla_wa2_f', 'new_m_gla_ba_f', 'new_m_gla_wa2_b', 'new_m_gla_ba_b', 'new_m_gla_norm_g', 'new_m_gla_proj', 'new_m_rwkv_mu_prev', 'new_m_rwkv_mu_next', 'new_m_rwkv_w0_f', 'new_m_rwkv_w2_f', 'new_m_rwkv_w0_b', 'new_m_rwkv_w2_b', 'new_m_rwkv_a0', 'new_m_rwkv_a2', 'new_m_rwkv_g2', 'new_m_rwkv_k_k', 'new_m_rwkv_k_a', 'new_m_rwkv_r_k', 'new_m_rwkv_ln_w', 'new_m_rwkv_ln_b', 'new_m_rwkv_proj', 'new_m_w_out', 'new_m_norm2_g', 'new_m_ffn_up', 'new_m_ffn_conv_w', 'new_m_ffn_conv_b', 'new_m_ffn_down', 'new_m_norm_f_g', 'new_v_norm1_g', 'new_v_w_in', 'new_v_gla_wa2_f', 'new_v_gla_ba_f', 'new_v_gla_wa2_b', 'new_v_gla_ba_b', 'new_v_gla_norm_g', 'new_v_gla_proj', 'new_v_rwkv_mu_prev', 'new_v_rwkv_mu_next', 'new_v_rwkv_w0_f', 'new_v_rwkv_w2_f', 'new_v_rwkv_w0_b', 'new_v_rwkv_w2_b', 'new_v_rwkv_a0', 'new_v_rwkv_a2', 'new_v_rwkv_g2', 'new_v_rwkv_k_k', 'new_v_rwkv_k_a', 'new_v_rwkv_r_k', 'new_v_rwkv_ln_w', 'new_v_rwkv_ln_b', 'new_v_rwkv_proj', 'new_v_w_out', 'new_v_norm2_g', 'new_v_ffn_up', 'new_v_ffn_conv_w', 'new_v_ffn_conv_b', 'new_v_ffn_down', 'new_v_norm_f_g']
TWIN_LEAF_KINDS = {'loss': 'loss', 'grad_x': 'grad_x', 'grad_norm1_g': 'grad_w', 'grad_w_in': 'grad_w', 'grad_gla_wa2_f': 'grad_w', 'grad_gla_ba_f': 'grad_w', 'grad_gla_wa2_b': 'grad_w', 'grad_gla_ba_b': 'grad_w', 'grad_gla_norm_g': 'grad_w', 'grad_gla_proj': 'grad_w', 'grad_rwkv_mu_prev': 'grad_w', 'grad_rwkv_mu_next': 'grad_w', 'grad_rwkv_w0_f': 'grad_w', 'grad_rwkv_w2_f': 'grad_w', 'grad_rwkv_w0_b': 'grad_w', 'grad_rwkv_w2_b': 'grad_w', 'grad_rwkv_a0': 'grad_w', 'grad_rwkv_a2': 'grad_w', 'grad_rwkv_g2': 'grad_w', 'grad_rwkv_k_k': 'grad_w', 'grad_rwkv_k_a': 'grad_w', 'grad_rwkv_r_k': 'grad_w', 'grad_rwkv_ln_w': 'grad_w', 'grad_rwkv_ln_b': 'grad_w', 'grad_rwkv_proj': 'grad_w', 'grad_w_out': 'grad_w', 'grad_norm2_g': 'grad_w', 'grad_ffn_up': 'grad_w', 'grad_ffn_conv_w': 'grad_w', 'grad_ffn_conv_b': 'grad_w', 'grad_ffn_down': 'grad_w', 'grad_norm_f_g': 'grad_w', 'delta_norm1_g': 'delta_w', 'delta_w_in': 'delta_w', 'delta_gla_wa2_f': 'delta_w', 'delta_gla_ba_f': 'delta_w', 'delta_gla_wa2_b': 'delta_w', 'delta_gla_ba_b': 'delta_w', 'delta_gla_norm_g': 'delta_w', 'delta_gla_proj': 'delta_w', 'delta_rwkv_mu_prev': 'delta_w', 'delta_rwkv_mu_next': 'delta_w', 'delta_rwkv_w0_f': 'delta_w', 'delta_rwkv_w2_f': 'delta_w', 'delta_rwkv_w0_b': 'delta_w', 'delta_rwkv_w2_b': 'delta_w', 'delta_rwkv_a0': 'delta_w', 'delta_rwkv_a2': 'delta_w', 'delta_rwkv_g2': 'delta_w', 'delta_rwkv_k_k': 'delta_w', 'delta_rwkv_k_a': 'delta_w', 'delta_rwkv_r_k': 'delta_w', 'delta_rwkv_ln_w': 'delta_w', 'delta_rwkv_ln_b': 'delta_w', 'delta_rwkv_proj': 'delta_w', 'delta_w_out': 'delta_w', 'delta_norm2_g': 'delta_w', 'delta_ffn_up': 'delta_w', 'delta_ffn_conv_w': 'delta_w', 'delta_ffn_conv_b': 'delta_w', 'delta_ffn_down': 'delta_w', 'delta_norm_f_g': 'delta_w', 'new_m_norm1_g': 'new_m', 'new_m_w_in': 'new_m', 'new_m_gla_wa2_f': 'new_m', 'new_m_gla_ba_f': 'new_m', 'new_m_gla_wa2_b': 'new_m', 'new_m_gla_ba_b': 'new_m', 'new_m_gla_norm_g': 'new_m', 'new_m_gla_proj': 'new_m', 'new_m_rwkv_mu_prev': 'new_m', 'new_m_rwkv_mu_next': 'new_m', 'new_m_rwkv_w0_f': 'new_m', 'new_m_rwkv_w2_f': 'new_m', 'new_m_rwkv_w0_b': 'new_m', 'new_m_rwkv_w2_b': 'new_m', 'new_m_rwkv_a0': 'new_m', 'new_m_rwkv_a2': 'new_m', 'new_m_rwkv_g2': 'new_m', 'new_m_rwkv_k_k': 'new_m', 'new_m_rwkv_k_a': 'new_m', 'new_m_rwkv_r_k': 'new_m', 'new_m_rwkv_ln_w': 'new_m', 'new_m_rwkv_ln_b': 'new_m', 'new_m_rwkv_proj': 'new_m', 'new_m_w_out': 'new_m', 'new_m_norm2_g': 'new_m', 'new_m_ffn_up': 'new_m', 'new_m_ffn_conv_w': 'new_m', 'new_m_ffn_conv_b': 'new_m', 'new_m_ffn_down': 'new_m', 'new_m_norm_f_g': 'new_m', 'new_v_norm1_g': 'new_v', 'new_v_w_in': 'new_v', 'new_v_gla_wa2_f': 'new_v', 'new_v_gla_ba_f': 'new_v', 'new_v_gla_wa2_b': 'new_v', 'new_v_gla_ba_b': 'new_v', 'new_v_gla_norm_g': 'new_v', 'new_v_gla_proj': 'new_v', 'new_v_rwkv_mu_prev': 'new_v', 'new_v_rwkv_mu_next': 'new_v', 'new_v_rwkv_w0_f': 'new_v', 'new_v_rwkv_w2_f': 'new_v', 'new_v_rwkv_w0_b': 'new_v', 'new_v_rwkv_w2_b': 'new_v', 'new_v_rwkv_a0': 'new_v', 'new_v_rwkv_a2': 'new_v', 'new_v_rwkv_g2': 'new_v', 'new_v_rwkv_k_k': 'new_v', 'new_v_rwkv_k_a': 'new_v', 'new_v_rwkv_r_k': 'new_v', 'new_v_rwkv_ln_w': 'new_v', 'new_v_rwkv_ln_b': 'new_v', 'new_v_rwkv_proj': 'new_v', 'new_v_w_out': 'new_v', 'new_v_norm2_g': 'new_v', 'new_v_ffn_up': 'new_v', 'new_v_ffn_conv_w': 'new_v', 'new_v_ffn_conv_b': 'new_v', 'new_v_ffn_down': 'new_v', 'new_v_norm_f_g': 'new_v'}


def _forward(args):
    return _fwd_reference(*[args[k] for k in FWD_PARAMS])


def _output_shape():
    out = _jax.eval_shape(lambda: _forward(_fwd_setup_inputs(0)))
    return out.shape, out.dtype

N_MICROBATCH = 1
ADAM_LR = 0.001
ADAM_B1 = 0.9
ADAM_B2 = 0.999
ADAM_EPS = 1e-08
ADAM_WD = 0.01
ADAM_STEP = 10
PER_EXAMPLE_BATCH_AXIS = {'x': 0, 'loss_target': 0}
SHARED_INPUTS = []
_WEIGHT_DTYPES = {'norm1_g': _jnp.float32, 'w_in': _jnp.float32, 'gla_wa2_f': _jnp.float32, 'gla_ba_f': _jnp.float32, 'gla_wa2_b': _jnp.float32, 'gla_ba_b': _jnp.float32, 'gla_norm_g': _jnp.float32, 'gla_proj': _jnp.float32, 'rwkv_mu_prev': _jnp.float32, 'rwkv_mu_next': _jnp.float32, 'rwkv_w0_f': _jnp.float32, 'rwkv_w2_f': _jnp.float32, 'rwkv_w0_b': _jnp.float32, 'rwkv_w2_b': _jnp.float32, 'rwkv_a0': _jnp.float32, 'rwkv_a2': _jnp.float32, 'rwkv_g2': _jnp.float32, 'rwkv_k_k': _jnp.float32, 'rwkv_k_a': _jnp.float32, 'rwkv_r_k': _jnp.float32, 'rwkv_ln_w': _jnp.float32, 'rwkv_ln_b': _jnp.float32, 'rwkv_proj': _jnp.float32, 'w_out': _jnp.float32, 'norm2_g': _jnp.float32, 'ffn_up': _jnp.float32, 'ffn_conv_w': _jnp.float32, 'ffn_conv_b': _jnp.float32, 'ffn_down': _jnp.float32, 'norm_f_g': _jnp.float32}
MOMENT_SCALE = {'norm1_g': 2.355629e-01, 'w_in': 1.003681e-01, 'gla_wa2_f': 1.803840e-02, 'gla_ba_f': 1.089280e-01, 'gla_wa2_b': 1.624368e-02, 'gla_ba_b': 8.066864e-02, 'gla_norm_g': 1.262318e-01, 'gla_proj': 8.455359e-02, 'rwkv_mu_prev': 2.173313e-01, 'rwkv_mu_next': 2.272838e-01, 'rwkv_w0_f': 3.700582e-02, 'rwkv_w2_f': 9.417542e-03, 'rwkv_w0_b': 3.975601e-02, 'rwkv_w2_b': 9.446213e-03, 'rwkv_a0': 5.121699e-02, 'rwkv_a2': 3.305286e-02, 'rwkv_g2': 1.104285e-01, 'rwkv_k_k': 5.380110e-02, 'rwkv_k_a': 1.410349e-01, 'rwkv_r_k': 1.462693e-01, 'rwkv_ln_w': 1.097657e-01, 'rwkv_ln_b': 1.182990e-01, 'rwkv_proj': 7.527800e-02, 'w_out': 1.132188e-01, 'norm2_g': 1.985462e-01, 'ffn_up': 8.246232e-02, 'ffn_conv_w': 8.240647e-02, 'ffn_conv_b': 7.722638e-02, 'ffn_down': 1.331144e-01, 'norm_f_g': 6.385574e+01}


def _to_microbatches(a, axis):
    t = _jnp.moveaxis(a, axis, 0)
    t = t.reshape((N_MICROBATCH, t.shape[0] // N_MICROBATCH) + t.shape[1:])
    return _jnp.moveaxis(t, 1, axis + 1)


def setup_inputs(seed: int = 0) -> dict:
    inp = _fwd_setup_inputs(seed)
    key = _jax.random.fold_in(_jax.random.key(seed), 7919)
    shape, _ = _output_shape()
    out = dict(inp)
    out["loss_target"] = _jax.random.normal(_jax.random.fold_in(key, 0), shape, _jnp.float32)
    for i, name in enumerate(TWIN_WEIGHTS):
        w = inp[name].astype(_jnp.float32)
        if MOMENT_SCALE is None:
            s = _jnp.sqrt(_jnp.mean(_jnp.square(w)) + 1e-30)
        else:
            s = MOMENT_SCALE[name]
        km, kv = _jax.random.split(_jax.random.fold_in(key, i + 1))
        out[name] = w
        out["m_" + name] = s * _jax.random.normal(km, w.shape, _jnp.float32)
        out["v_" + name] = (s * s) * _jax.random.uniform(kv, w.shape, _jnp.float32, 0.5, 1.5)
    if N_MICROBATCH > 1:
        for name, axis in PER_EXAMPLE_BATCH_AXIS.items():
            out[name] = _to_microbatches(out[name], axis)
    return {'x': out['x'], 'norm1_g': out['norm1_g'], 'w_in': out['w_in'], 'gla_wa2_f': out['gla_wa2_f'], 'gla_ba_f': out['gla_ba_f'], 'gla_wa2_b': out['gla_wa2_b'], 'gla_ba_b': out['gla_ba_b'], 'gla_norm_g': out['gla_norm_g'], 'gla_proj': out['gla_proj'], 'rwkv_mu_prev': out['rwkv_mu_prev'], 'rwkv_mu_next': out['rwkv_mu_next'], 'rwkv_w0_f': out['rwkv_w0_f'], 'rwkv_w2_f': out['rwkv_w2_f'], 'rwkv_w0_b': out['rwkv_w0_b'], 'rwkv_w2_b': out['rwkv_w2_b'], 'rwkv_a0': out['rwkv_a0'], 'rwkv_a2': out['rwkv_a2'], 'rwkv_g2': out['rwkv_g2'], 'rwkv_k_k': out['rwkv_k_k'], 'rwkv_k_a': out['rwkv_k_a'], 'rwkv_r_k': out['rwkv_r_k'], 'rwkv_ln_w': out['rwkv_ln_w'], 'rwkv_ln_b': out['rwkv_ln_b'], 'rwkv_proj': out['rwkv_proj'], 'w_out': out['w_out'], 'norm2_g': out['norm2_g'], 'ffn_up': out['ffn_up'], 'ffn_conv_w': out['ffn_conv_w'], 'ffn_conv_b': out['ffn_conv_b'], 'ffn_down': out['ffn_down'], 'norm_f_g': out['norm_f_g'], 'loss_target': out['loss_target'], 'm_norm1_g': out['m_norm1_g'], 'm_w_in': out['m_w_in'], 'm_gla_wa2_f': out['m_gla_wa2_f'], 'm_gla_ba_f': out['m_gla_ba_f'], 'm_gla_wa2_b': out['m_gla_wa2_b'], 'm_gla_ba_b': out['m_gla_ba_b'], 'm_gla_norm_g': out['m_gla_norm_g'], 'm_gla_proj': out['m_gla_proj'], 'm_rwkv_mu_prev': out['m_rwkv_mu_prev'], 'm_rwkv_mu_next': out['m_rwkv_mu_next'], 'm_rwkv_w0_f': out['m_rwkv_w0_f'], 'm_rwkv_w2_f': out['m_rwkv_w2_f'], 'm_rwkv_w0_b': out['m_rwkv_w0_b'], 'm_rwkv_w2_b': out['m_rwkv_w2_b'], 'm_rwkv_a0': out['m_rwkv_a0'], 'm_rwkv_a2': out['m_rwkv_a2'], 'm_rwkv_g2': out['m_rwkv_g2'], 'm_rwkv_k_k': out['m_rwkv_k_k'], 'm_rwkv_k_a': out['m_rwkv_k_a'], 'm_rwkv_r_k': out['m_rwkv_r_k'], 'm_rwkv_ln_w': out['m_rwkv_ln_w'], 'm_rwkv_ln_b': out['m_rwkv_ln_b'], 'm_rwkv_proj': out['m_rwkv_proj'], 'm_w_out': out['m_w_out'], 'm_norm2_g': out['m_norm2_g'], 'm_ffn_up': out['m_ffn_up'], 'm_ffn_conv_w': out['m_ffn_conv_w'], 'm_ffn_conv_b': out['m_ffn_conv_b'], 'm_ffn_down': out['m_ffn_down'], 'm_norm_f_g': out['m_norm_f_g'], 'v_norm1_g': out['v_norm1_g'], 'v_w_in': out['v_w_in'], 'v_gla_wa2_f': out['v_gla_wa2_f'], 'v_gla_ba_f': out['v_gla_ba_f'], 'v_gla_wa2_b': out['v_gla_wa2_b'], 'v_gla_ba_b': out['v_gla_ba_b'], 'v_gla_norm_g': out['v_gla_norm_g'], 'v_gla_proj': out['v_gla_proj'], 'v_rwkv_mu_prev': out['v_rwkv_mu_prev'], 'v_rwkv_mu_next': out['v_rwkv_mu_next'], 'v_rwkv_w0_f': out['v_rwkv_w0_f'], 'v_rwkv_w2_f': out['v_rwkv_w2_f'], 'v_rwkv_w0_b': out['v_rwkv_w0_b'], 'v_rwkv_w2_b': out['v_rwkv_w2_b'], 'v_rwkv_a0': out['v_rwkv_a0'], 'v_rwkv_a2': out['v_rwkv_a2'], 'v_rwkv_g2': out['v_rwkv_g2'], 'v_rwkv_k_k': out['v_rwkv_k_k'], 'v_rwkv_k_a': out['v_rwkv_k_a'], 'v_rwkv_r_k': out['v_rwkv_r_k'], 'v_rwkv_ln_w': out['v_rwkv_ln_w'], 'v_rwkv_ln_b': out['v_rwkv_ln_b'], 'v_rwkv_proj': out['v_rwkv_proj'], 'v_w_out': out['v_w_out'], 'v_norm2_g': out['v_norm2_g'], 'v_ffn_up': out['v_ffn_up'], 'v_ffn_conv_w': out['v_ffn_conv_w'], 'v_ffn_conv_b': out['v_ffn_conv_b'], 'v_ffn_down': out['v_ffn_down'], 'v_norm_f_g': out['v_norm_f_g']}


def _loss(weights, diff, rest, loss_target):
    with _jax.named_scope("forward"):
        args = {**rest, TWIN_DIFF_INPUT: diff, **{k: w.astype(_WEIGHT_DTYPES[k]) for k, w in weights.items()}}
        y = _forward(args)
    with _jax.named_scope("loss_head"):
        err = _jnp.square(y.astype(_jnp.float32) - loss_target)
        return 0.5 * _jnp.sum(_jnp.mean(err, axis=-1)) if err.ndim else 0.5 * err


def _adamw(w, g, m, v):
    m = ADAM_B1 * m + (1.0 - ADAM_B1) * g
    v = ADAM_B2 * v + (1.0 - ADAM_B2) * _jnp.square(g)
    m_hat = m / (1.0 - ADAM_B1 ** ADAM_STEP)
    v_hat = v / (1.0 - ADAM_B2 ** ADAM_STEP)
    delta = -ADAM_LR * (m_hat / (_jnp.sqrt(v_hat) + ADAM_EPS) + ADAM_WD * w)
    return delta, m, v


def reference(x, norm1_g, w_in, gla_wa2_f, gla_ba_f, gla_wa2_b, gla_ba_b, gla_norm_g, gla_proj, rwkv_mu_prev, rwkv_mu_next, rwkv_w0_f, rwkv_w2_f, rwkv_w0_b, rwkv_w2_b, rwkv_a0, rwkv_a2, rwkv_g2, rwkv_k_k, rwkv_k_a, rwkv_r_k, rwkv_ln_w, rwkv_ln_b, rwkv_proj, w_out, norm2_g, ffn_up, ffn_conv_w, ffn_conv_b, ffn_down, norm_f_g, loss_target, m_norm1_g, m_w_in, m_gla_wa2_f, m_gla_ba_f, m_gla_wa2_b, m_gla_ba_b, m_gla_norm_g, m_gla_proj, m_rwkv_mu_prev, m_rwkv_mu_next, m_rwkv_w0_f, m_rwkv_w2_f, m_rwkv_w0_b, m_rwkv_w2_b, m_rwkv_a0, m_rwkv_a2, m_rwkv_g2, m_rwkv_k_k, m_rwkv_k_a, m_rwkv_r_k, m_rwkv_ln_w, m_rwkv_ln_b, m_rwkv_proj, m_w_out, m_norm2_g, m_ffn_up, m_ffn_conv_w, m_ffn_conv_b, m_ffn_down, m_norm_f_g, v_norm1_g, v_w_in, v_gla_wa2_f, v_gla_ba_f, v_gla_wa2_b, v_gla_ba_b, v_gla_norm_g, v_gla_proj, v_rwkv_mu_prev, v_rwkv_mu_next, v_rwkv_w0_f, v_rwkv_w2_f, v_rwkv_w0_b, v_rwkv_w2_b, v_rwkv_a0, v_rwkv_a2, v_rwkv_g2, v_rwkv_k_k, v_rwkv_k_a, v_rwkv_r_k, v_rwkv_ln_w, v_rwkv_ln_b, v_rwkv_proj, v_w_out, v_norm2_g, v_ffn_up, v_ffn_conv_w, v_ffn_conv_b, v_ffn_down, v_norm_f_g):
    given = dict(x=x, norm1_g=norm1_g, w_in=w_in, gla_wa2_f=gla_wa2_f, gla_ba_f=gla_ba_f, gla_wa2_b=gla_wa2_b, gla_ba_b=gla_ba_b, gla_norm_g=gla_norm_g, gla_proj=gla_proj, rwkv_mu_prev=rwkv_mu_prev, rwkv_mu_next=rwkv_mu_next, rwkv_w0_f=rwkv_w0_f, rwkv_w2_f=rwkv_w2_f, rwkv_w0_b=rwkv_w0_b, rwkv_w2_b=rwkv_w2_b, rwkv_a0=rwkv_a0, rwkv_a2=rwkv_a2, rwkv_g2=rwkv_g2, rwkv_k_k=rwkv_k_k, rwkv_k_a=rwkv_k_a, rwkv_r_k=rwkv_r_k, rwkv_ln_w=rwkv_ln_w, rwkv_ln_b=rwkv_ln_b, rwkv_proj=rwkv_proj, w_out=w_out, norm2_g=norm2_g, ffn_up=ffn_up, ffn_conv_w=ffn_conv_w, ffn_conv_b=ffn_conv_b, ffn_down=ffn_down, norm_f_g=norm_f_g, loss_target=loss_target, m_norm1_g=m_norm1_g, m_w_in=m_w_in, m_gla_wa2_f=m_gla_wa2_f, m_gla_ba_f=m_gla_ba_f, m_gla_wa2_b=m_gla_wa2_b, m_gla_ba_b=m_gla_ba_b, m_gla_norm_g=m_gla_norm_g, m_gla_proj=m_gla_proj, m_rwkv_mu_prev=m_rwkv_mu_prev, m_rwkv_mu_next=m_rwkv_mu_next, m_rwkv_w0_f=m_rwkv_w0_f, m_rwkv_w2_f=m_rwkv_w2_f, m_rwkv_w0_b=m_rwkv_w0_b, m_rwkv_w2_b=m_rwkv_w2_b, m_rwkv_a0=m_rwkv_a0, m_rwkv_a2=m_rwkv_a2, m_rwkv_g2=m_rwkv_g2, m_rwkv_k_k=m_rwkv_k_k, m_rwkv_k_a=m_rwkv_k_a, m_rwkv_r_k=m_rwkv_r_k, m_rwkv_ln_w=m_rwkv_ln_w, m_rwkv_ln_b=m_rwkv_ln_b, m_rwkv_proj=m_rwkv_proj, m_w_out=m_w_out, m_norm2_g=m_norm2_g, m_ffn_up=m_ffn_up, m_ffn_conv_w=m_ffn_conv_w, m_ffn_conv_b=m_ffn_conv_b, m_ffn_down=m_ffn_down, m_norm_f_g=m_norm_f_g, v_norm1_g=v_norm1_g, v_w_in=v_w_in, v_gla_wa2_f=v_gla_wa2_f, v_gla_ba_f=v_gla_ba_f, v_gla_wa2_b=v_gla_wa2_b, v_gla_ba_b=v_gla_ba_b, v_gla_norm_g=v_gla_norm_g, v_gla_proj=v_gla_proj, v_rwkv_mu_prev=v_rwkv_mu_prev, v_rwkv_mu_next=v_rwkv_mu_next, v_rwkv_w0_f=v_rwkv_w0_f, v_rwkv_w2_f=v_rwkv_w2_f, v_rwkv_w0_b=v_rwkv_w0_b, v_rwkv_w2_b=v_rwkv_w2_b, v_rwkv_a0=v_rwkv_a0, v_rwkv_a2=v_rwkv_a2, v_rwkv_g2=v_rwkv_g2, v_rwkv_k_k=v_rwkv_k_k, v_rwkv_k_a=v_rwkv_k_a, v_rwkv_r_k=v_rwkv_r_k, v_rwkv_ln_w=v_rwkv_ln_w, v_rwkv_ln_b=v_rwkv_ln_b, v_rwkv_proj=v_rwkv_proj, v_w_out=v_w_out, v_norm2_g=v_norm2_g, v_ffn_up=v_ffn_up, v_ffn_conv_w=v_ffn_conv_w, v_ffn_conv_b=v_ffn_conv_b, v_ffn_down=v_ffn_down, v_norm_f_g=v_norm_f_g)
    weights = {n: given[n] for n in TWIN_WEIGHTS}
    shared = {n: given[n] for n in SHARED_INPUTS}
    per_example = {n: given[n] for n in ['x']}
    grad_fn = _jax.value_and_grad(_loss, argnums=(0, 1))

    def one_microbatch(ex, loss_target):
        ex = dict(ex)
        diff = ex.pop(TWIN_DIFF_INPUT)
        return grad_fn(weights, diff, {**shared, **ex}, loss_target)

    if N_MICROBATCH == 1:
        loss, (grad_w, grad_x) = one_microbatch(per_example, given["loss_target"])
    else:
        def body(carry, xs):
            loss_sum, grad_sum = carry
            l_k, (gw_k, gx_k) = one_microbatch(xs[0], xs[1])
            with _jax.named_scope("update"):
                return (loss_sum + l_k, _jax.tree.map(_jnp.add, grad_sum, gw_k)), gx_k

        init = (_jnp.zeros((), _jnp.float32), _jax.tree.map(_jnp.zeros_like, weights))
        (loss, grad_w), grad_x = _jax.lax.scan(body, init, (per_example, given["loss_target"]))
    with _jax.named_scope("update"):
        delta_w, new_m, new_v = {}, {}, {}
        for n in TWIN_WEIGHTS:
            delta_w[n], new_m[n], new_v[n] = _adamw(weights[n], grad_w[n], given["m_" + n], given["v_" + n])
    return (loss, grad_x, *[grad_w[n] for n in TWIN_WEIGHTS], *[delta_w[n] for n in TWIN_WEIGHTS],
            *[new_m[n] for n in TWIN_WEIGHTS], *[new_v[n] for n in TWIN_WEIGHTS])
```

```python
import functools

import jax
import jax.numpy as jnp
import numpy as np
from jax import lax
from jax.experimental import pallas as pl
from jax.experimental.pallas import tpu as pltpu

F32 = jnp.float32
BF16 = jnp.bfloat16
HIGHEST = lax.Precision.HIGHEST
MESH_IDS = pl.DeviceIdType.MESH

D_MODEL = 1024
N_PROJ = 5408
PROJ_PAD = 5632
D_FF = 2752
FF_PAD = 2816
GLA_CHUNK = 64
SCAN_CHUNK = 16
NORM_EPS = 1e-6
HEAD_NORM_EPS = 1e-5
RWKV_GN_EPS = 64 * 1e-5
ADAM_LR, ADAM_B1, ADAM_B2, ADAM_EPS, ADAM_WD, ADAM_STEP = 0.001, 0.9, 0.999, 1e-08, 0.01, 10
VMEM_LIMIT = 56 * 1024 * 1024

FLAT_ROWS, FLAT_COLS = 128, 1024
BIG = (
    ("w_in", (1024, 1352), 1), ("gla_proj", (512, 256), 1), ("rwkv_proj", (512, 256), 1),
    ("w_out", (256, 1024), 0), ("ffn_up", (1024, 1376), 1), ("ffn_down", (688, 1024), 0),
)
SMALL = (
    ("gla_wa2_f", (16, 64), 1), ("gla_wa2_b", (16, 64), 1), ("rwkv_w2_f", (64, 128), 1),
    ("rwkv_w2_b", (64, 128), 1), ("rwkv_a2", (64, 128), 1), ("rwkv_g2", (128, 128), 1),
    ("ffn_conv_w", (3, 1376), 1),
)
SHARDED = BIG + SMALL
REPLICATED = (
    ("norm1_g", 1024), ("gla_ba_f", 256), ("gla_ba_b", 256), ("gla_norm_g", 512),
    ("rwkv_mu_prev", 1792), ("rwkv_mu_next", 1792), ("rwkv_w0_f", 512), ("rwkv_w0_b", 512),
    ("rwkv_a0", 512), ("rwkv_k_k", 512), ("rwkv_k_a", 512), ("rwkv_r_k", 512),
    ("rwkv_ln_w", 512), ("rwkv_ln_b", 512), ("norm2_g", 1024), ("ffn_conv_b", 5504),
    ("norm_f_g", 1024),
)
WEIGHT_ORDER = ("norm1_g", "w_in", "gla_wa2_f", "gla_ba_f", "gla_wa2_b", "gla_ba_b", "gla_norm_g", "gla_proj",
                "rwkv_mu_prev", "rwkv_mu_next", "rwkv_w0_f", "rwkv_w2_f", "rwkv_w0_b", "rwkv_w2_b", "rwkv_a0",
                "rwkv_a2", "rwkv_g2", "rwkv_k_k", "rwkv_k_a", "rwkv_r_k", "rwkv_ln_w", "rwkv_ln_b", "rwkv_proj",
                "w_out", "norm2_g", "ffn_up", "ffn_conv_w", "ffn_conv_b", "ffn_down", "norm_f_g")


def _cparams(**kw):
    return pltpu.CompilerParams(vmem_limit_bytes=VMEM_LIMIT, **kw)


def _pack(sharded_vals, repl_vals, loss=None):
    parts = [sharded_vals[n].reshape(-1) for n, _, _ in SMALL]
    parts += [repl_vals[n].reshape(-1) for n, _ in REPLICATED]
    parts.append(jnp.zeros((1,), F32) if loss is None else loss.reshape(1))
    used = sum(int(np.prod(s)) for _, s, _ in SMALL) + sum(w for _, w in REPLICATED) + 1
    parts.append(jnp.zeros((FLAT_ROWS * FLAT_COLS - used,), F32))
    return jnp.concatenate(parts).reshape(FLAT_ROWS, FLAT_COLS)


def _unpack(flat):
    v = flat.reshape(-1)
    out, off = {}, 0
    for n, s, _ in SMALL:
        k = int(np.prod(s))
        out[n] = v[off:off + k].reshape(s)
        off += k
    for n, w in REPLICATED:
        out[n] = v[off:off + w]
        off += w
    out["loss"] = v[off]
    return out


def _chip_peers():
    x, y, c = lax.axis_index("x"), lax.axis_index("y"), lax.axis_index("c")
    return x, y, c, ((1 - x, y), (x, 1 - y), (1 - x, 1 - y))


def _exchange_chips(arrs, name, gather):
    n = len(arrs)

    def body(*refs):
        srcs, outs = refs[:n], refs[n:2 * n]
        send_sems, recv_sems, local_sems = refs[2 * n:]
        x, y, c, peers = _chip_peers()
        me = 2 * x + y
        own = []
        for i in range(n):
            cp = pltpu.make_async_copy(srcs[i] if gather else srcs[i].at[me], outs[i].at[me], local_sems.at[i])
            cp.start()
            own.append(cp)
        sends = []
        for k, (px, py) in enumerate(peers):
            for i in range(n):
                cp = pltpu.make_async_remote_copy(
                    src_ref=srcs[i] if gather else srcs[i].at[2 * px + py], dst_ref=outs[i].at[me],
                    send_sem=send_sems.at[3 * i + k], recv_sem=recv_sems.at[3 * i + k],
                    device_id=(px, py, c), device_id_type=MESH_IDS)
                cp.start()
                sends.append(cp)
        for k, (px, py) in enumerate(peers):
            for i in range(n):
                pltpu.make_async_remote_copy(
                    src_ref=srcs[i] if gather else srcs[i].at[me], dst_ref=outs[i].at[2 * px + py],
                    send_sem=send_sems.at[3 * i + k], recv_sem=recv_sems.at[3 * i + k],
                    device_id=(px, py, c), device_id_type=MESH_IDS).wait_recv()
        for cp in sends:
            cp.wait_send()
        for cp in own:
            cp.wait()

    out_shape = [jax.ShapeDtypeStruct(((4,) + a.shape) if gather else a.shape, a.dtype) for a in arrs]
    return pl.pallas_call(
        body, name=name, out_shape=out_shape,
        in_specs=[pl.BlockSpec(memory_space=pl.ANY)] * n,
        out_specs=[pl.BlockSpec(memory_space=pl.ANY)] * n,
        scratch_shapes=[pltpu.SemaphoreType.DMA((3 * n,)), pltpu.SemaphoreType.DMA((3 * n,)),
                        pltpu.SemaphoreType.DMA((n,))],
    )(*arrs)


def _swap_with_sibling(arrs):
    n = len(arrs)

    def body(*refs):
        srcs, outs = refs[:n], refs[n:2 * n]
        send_sems, recv_sems = refs[2 * n:]
        x, y, c = lax.axis_index("x"), lax.axis_index("y"), lax.axis_index("c")
        cps = [pltpu.make_async_remote_copy(src_ref=srcs[i], dst_ref=outs[i], send_sem=send_sems.at[i],
                                            recv_sem=recv_sems.at[i], device_id=(x, y, 1 - c),
                                            device_id_type=MESH_IDS) for i in range(n)]
        for cp in cps:
            cp.start()
        for cp in cps:
            cp.wait()

    return pl.pallas_call(
        body, name="swap_sibling",
        out_shape=[jax.ShapeDtypeStruct(a.shape, a.dtype) for a in arrs],
        in_specs=[pl.BlockSpec(memory_space=pl.ANY)] * n,
        out_specs=[pl.BlockSpec(memory_space=pl.ANY)] * n,
        scratch_shapes=[pltpu.SemaphoreType.DMA((n,)), pltpu.SemaphoreType.DMA((n,))],
    )(*arrs)


def _row_tile(rows, cols):
    cap = max(8, (3 << 19) // (4 * (-(-cols // 128) * 128)))
    best = None
    for t in range(8, min(rows, cap) + 1, 8):
        if rows % t == 0:
            best = t
    return best or rows


def _sum_sources(r4, name):
    _, A, Bc = r4.shape
    ta = _row_tile(A, Bc)

    def body(r_ref, o_ref):
        f = lambda s: r_ref[s].astype(F32)
        o_ref[...] = ((f(0) + f(1)) + f(2)) + f(3)

    return pl.pallas_call(
        body, name=name, grid=(A // ta,),
        out_shape=jax.ShapeDtypeStruct((A, Bc), F32),
        in_specs=[pl.BlockSpec((4, ta, Bc), lambda i: (0, i, 0))],
        out_specs=pl.BlockSpec((ta, Bc), lambda i: (i, 0)),
        compiler_params=_cparams(),
    )(r4)


def _adamw(own, other, w, m, v, name):
    R, C = own.shape
    tr = _row_tile(R, C)

    def body(a_ref, b_ref, w_ref, m_ref, v_ref, g_out, d_out, m_out, v_out):
        g = a_ref[...] + b_ref[...]
        m_new = ADAM_B1 * m_ref[...] + (1.0 - ADAM_B1) * g
        v_new = ADAM_B2 * v_ref[...] + (1.0 - ADAM_B2) * (g * g)
        m_hat = m_new / (1.0 - ADAM_B1 ** ADAM_STEP)
        v_hat = v_new / (1.0 - ADAM_B2 ** ADAM_STEP)
        g_out[...] = g
        d_out[...] = -ADAM_LR * (m_hat / (jnp.sqrt(v_hat) + ADAM_EPS) + ADAM_WD * w_ref[...])
        m_out[...] = m_new
        v_out[...] = v_new

    spec = pl.BlockSpec((tr, C), lambda i: (i, 0))
    return pl.pallas_call(
        body, name=name, grid=(R // tr,),
        out_shape=[jax.ShapeDtypeStruct((R, C), F32)] * 4,
        in_specs=[spec] * 5, out_specs=[spec] * 4,
        compiler_params=_cparams(),
    )(own, other, w, m, v)


def _pick(n, options):
    for o in options:
        if n % o == 0:
            return o
    return n


def _div128(n, cap):
    best = None
    for t in range(128, min(n, cap) + 1, 128):
        if n % t == 0:
            best = t
    return best or n


MATMUL_VMEM = 40 * 1024 * 1024


def _matmul(a, b, name, out_dtype=F32, residual=None):
    M, K = a.shape
    _, N = b.shape
    tm, tn = _pick(M, (1024, 512)), _div128(N, 1408)
    while tm > 256 and 2 * (2 * tm * K + 2 * K * tn + (8 if residual is not None else 4) * tm * tn) > MATMUL_VMEM:
        tm //= 2

    def body(*refs):
        a_ref, b_ref = refs[0], refs[1]
        o_ref = refs[-1]
        acc = jnp.dot(a_ref[...], b_ref[...], preferred_element_type=F32)
        if residual is not None:
            acc = acc + refs[2][...]
        o_ref[...] = acc.astype(out_dtype)

    in_specs = [pl.BlockSpec((tm, K), lambda j, i: (i, 0)), pl.BlockSpec((K, tn), lambda j, i: (0, j))]
    args = [a, b]
    if residual is not None:
        in_specs.append(pl.BlockSpec((tm, tn), lambda j, i: (i, j)))
        args.append(residual)
    return pl.pallas_call(
        body, name=name, grid=(N // tn, M // tm),
        out_shape=jax.ShapeDtypeStruct((M, N), out_dtype),
        in_specs=in_specs, out_specs=pl.BlockSpec((tm, tn), lambda j, i: (i, j)),
        compiler_params=_cparams(),
    )(*args)


def _matmul_tn(a, b, name):
    R, M = a.shape
    _, N = b.shape
    tr, tm, tn = _pick(R, (2048, 1024, 512)), _div128(M, 1408), _div128(N, 1408)
    while tr > 512 and 2 * (2 * tr * tm + 2 * tr * tn + 4 * tm * tn) > MATMUL_VMEM:
        tr //= 2

    def body(a_ref, b_ref, o_ref):
        @pl.when(pl.program_id(2) == 0)
        def _():
            o_ref[...] = jnp.zeros_like(o_ref)

        o_ref[...] += lax.dot_general(a_ref[...], b_ref[...], (((0,), (0,)), ((), ())),
                                      preferred_element_type=F32)

    return pl.pallas_call(
        body, name=name, grid=(M // tm, N // tn, R // tr),
        out_shape=jax.ShapeDtypeStruct((M, N), F32),
        in_specs=[pl.BlockSpec((tr, tm), lambda i, j, r: (r, i)), pl.BlockSpec((tr, tn), lambda i, j, r: (r, j))],
        out_specs=pl.BlockSpec((tm, tn), lambda i, j, r: (i, j)),
        compiler_params=_cparams(),
    )(a, b)


def _row_spec(tm, width, col):
    return pl.BlockSpec((tm, width), lambda i: (i, col))


def _whole_spec(arr):
    nd = arr.ndim
    return pl.BlockSpec(arr.shape, lambda i: (0,) * nd)


def _rowwise(name, fn, rows, params, consts, outs, n_rows, tm):
    nr, npar, nc = len(rows), len(params), len(consts)

    def body(*refs):
        vals = [r[...].astype(F32) for r in refs[:nr]] + [r[...] for r in refs[nr:nr + npar + nc]]
        res = fn(*vals)
        for o_ref, r in zip(refs[nr + npar + nc:], res):
            o_ref[...] = r.astype(o_ref.dtype)

    return pl.pallas_call(
        body, name=name, grid=(n_rows // tm,),
        out_shape=[jax.ShapeDtypeStruct((n_rows, w), dt) for w, dt in outs],
        in_specs=[_row_spec(tm, w, c) for _, w, c in rows] + [_whole_spec(p) for p in params + consts],
        out_specs=[_row_spec(tm, w, 0) for w, _ in outs],
        compiler_params=_cparams(),
    )(*[a for a, _, _ in rows], *params, *consts)


def _rowwise_bwd(name, fn, rows, params, consts, cts, row_grads, n_rows, tm):
    nr, npar, nc = len(rows), len(params), len(consts)
    ct_flat = [p for pieces in cts for p in pieces]
    res_flat = [rg[1] for rg in row_grads if rg is not None and rg[1] is not None]
    n_ct, n_res = len(ct_flat), len(res_flat)
    n_in = nr + npar + nc + n_ct + n_res
    wanted = [k for k, rg in enumerate(row_grads) if rg is not None]

    def body(*refs):
        row_vals = [r[...].astype(F32) for r in refs[:nr]]
        par_vals = [r[...] for r in refs[nr:nr + npar]]
        const_vals = [r[...] for r in refs[nr + npar:nr + npar + nc]]
        ct_refs = refs[nr + npar + nc:nr + npar + nc + n_ct]
        res_refs = refs[nr + npar + nc + n_ct:n_in]
        out_refs = refs[n_in:]
        ct_vals, pos = [], 0
        for pieces in cts:
            acc = ct_refs[pos][...].astype(F32)
            for q in range(1, len(pieces)):
                acc = acc + ct_refs[pos + q][...].astype(F32)
            pos += len(pieces)
            ct_vals.append(acc)
        _, vjp = jax.vjp(lambda *a: tuple(fn(*a, *const_vals)), *row_vals, *par_vals)
        grads = vjp(tuple(ct_vals))
        ri = 0
        for slot, k in enumerate(wanted):
            g = grads[k]
            if row_grads[k][1] is not None:
                g = g + res_refs[ri][...].astype(F32)
                ri += 1
            out_refs[slot][...] = g.astype(out_refs[slot].dtype)

        @pl.when(pl.program_id(0) == 0)
        def _():
            for q in range(npar):
                out_refs[len(wanted) + q][...] = jnp.zeros_like(out_refs[len(wanted) + q])

        for q in range(npar):
            out_refs[len(wanted) + q][...] += grads[nr + q]

    out_shape = [jax.ShapeDtypeStruct((n_rows, rows[k][1]), row_grads[k][0]) for k in wanted]
    out_shape += [jax.ShapeDtypeStruct(p.shape, F32) for p in params]
    out_specs = [_row_spec(tm, rows[k][1], 0) for k in wanted] + [_whole_spec(p) for p in params]
    in_specs = [_row_spec(tm, w, c) for _, w, c in rows] + [_whole_spec(p) for p in params + consts]
    in_specs += [_row_spec(tm, w, c) for _, w, c in ct_flat + res_flat]
    res = pl.pallas_call(
        body, name=name, grid=(n_rows // tm,),
        out_shape=out_shape, in_specs=in_specs, out_specs=out_specs,
        compiler_params=_cparams(),
    )(*[a for a, _, _ in rows], *params, *consts, *[a for a, _, _ in ct_flat + res_flat])
    return res[:len(wanted)], res[len(wanted):]


def _sigmoid(x):
    return 0.5 * jnp.tanh(0.5 * x) + 0.5


def _softplus(x):
    return jnp.maximum(x, 0.0) + jnp.log(1.0 + jnp.exp(-jnp.abs(x)))


def _seg_dot_impl(x, seg2):
    hi = x.astype(BF16)
    lo = (x - hi.astype(F32)).astype(BF16)
    return jnp.dot(jnp.concatenate([hi, lo], axis=1), seg2, preferred_element_type=F32)


@jax.custom_vjp
def _seg_dot(x, seg2):
    return _seg_dot_impl(x, seg2)


_seg_dot.defvjp(lambda x, seg2: (_seg_dot_impl(x, seg2), seg2),
                lambda seg2, ct: (_seg_dot_impl(ct, seg2), jnp.zeros_like(seg2)))


def _fn_norm(x, g):
    r = lax.rsqrt(jnp.mean(x * x, axis=-1, keepdims=True) + NORM_EPS)
    return ((x * r) * g,)


def _fn_gla_post(o_f, o_b, og, norm_g):
    o = o_f + o_b
    heads = []
    for h in range(4):
        oh = o[:, h * 128:(h + 1) * 128]
        heads.append(oh * lax.rsqrt(jnp.mean(oh * oh, axis=-1, keepdims=True) + HEAD_NORM_EPS))
    on = jnp.concatenate(heads, axis=1) * norm_g
    return (on * (og * _sigmoid(og)),)


def _fn_rwkv_pre(s_k, s_wag, w0_f, w0_b, a0, k_k, k_a, w2_f, w2_b, a2, g2, seg64):
    wa = s_wag[:, 0:128]
    gl = s_wag[:, 128:256]
    tw = jnp.tanh(wa)
    z_f = w0_f + jnp.dot(tw, w2_f, preferred_element_type=F32)
    z_b = w0_b + jnp.dot(tw, w2_b, preferred_element_type=F32)
    w_f = jnp.exp(-jnp.exp(-_softplus(-z_f) - 0.5))
    w_b = jnp.exp(-jnp.exp(-_softplus(-z_b) - 0.5))
    a = _sigmoid(a0 + jnp.dot(wa, a2, preferred_element_type=F32))
    g = jnp.dot(_sigmoid(gl), g2, preferred_element_type=F32)
    kk = s_k * k_k
    kkn = kk / jnp.maximum(jnp.sqrt(_seg_dot(kk * kk, seg64)), 1e-12)
    k2 = s_k * (1.0 + (a - 1.0) * k_a)
    return w_f, w_b, k2, -kkn, kkn * a, g


def _fn_rwkv_post(y_f, y_b, s_r, k2, s_v, g, ln_w, ln_b, r_k, seg64):
    y = y_f + y_b
    mu = _seg_dot(y, seg64) * (1.0 / 64.0)
    yc = y - mu
    var = _seg_dot(yc * yc, seg64) * (1.0 / 64.0)
    yn = yc * lax.rsqrt(var + RWKV_GN_EPS) * ln_w + ln_b
    bonus = _seg_dot(s_r * k2 * r_k, seg64) * s_v
    return ((yn + bonus) * g,)


def _fn_merge(ga, gb, y_a, y_b):
    return (_sigmoid(ga) * y_a + _sigmoid(gb) * y_b,)


def _loss_head(x2, target, gf):
    N, Dm = x2.shape
    tm = _pick(N, (512,))

    def fn(x, g, t):
        r = lax.rsqrt(jnp.mean(x * x, axis=-1, keepdims=True) + NORM_EPS)
        err = (x * r) * g - t
        return 0.5 * jnp.sum(jnp.mean(err * err, axis=-1, keepdims=True), axis=0, keepdims=True)

    def body(x_ref, t_ref, g_ref, loss_ref, dx_ref, dg_ref):
        t = t_ref[...]
        loss, vjp = jax.vjp(lambda x, g: fn(x, g, t), x_ref[...], g_ref[...])
        dx, dg = vjp(jnp.ones((1, 1), F32))

        @pl.when(pl.program_id(0) == 0)
        def _():
            loss_ref[...] = jnp.zeros_like(loss_ref)
            dg_ref[...] = jnp.zeros_like(dg_ref)

        loss_ref[...] += jnp.broadcast_to(loss, loss_ref.shape)
        dg_ref[...] += dg
        dx_ref[...] = dx

    return pl.pallas_call(
        body, name="loss_head", grid=(N // tm,),
        out_shape=[jax.ShapeDtypeStruct((8, 128), F32), jax.ShapeDtypeStruct((N, Dm), F32),
                   jax.ShapeDtypeStruct((1, Dm), F32)],
        in_specs=[_row_spec(tm, Dm, 0), _row_spec(tm, Dm, 0), _whole_spec(gf)],
        out_specs=[pl.BlockSpec((8, 128), lambda i: (0, 0)), _row_spec(tm, Dm, 0),
                   pl.BlockSpec((1, Dm), lambda i: (0, 0))],
        compiler_params=_cparams(),
    )(x2, target, gf)


def _shift_prev(u):
    rolled = pltpu.roll(u, 1, axis=0)
    row = lax.broadcasted_iota(jnp.int32, u.shape, 0)
    return jnp.where(row == 0, 0.0, rolled)


def _shift_next(u):
    T = u.shape[0]
    rolled = pltpu.roll(u, T - 1, axis=0)
    row = lax.broadcasted_iota(jnp.int32, u.shape, 0)
    return jnp.where(row == T - 1, 0.0, rolled)


_SHIFT_BLOCKS = 7


def _shift_src_col(j):
    return jnp.where(j < 6, 6 + j, 20)


def _token_shift(p, mu_prev, mu_next, B, T):
    def body(p_ref, mp_ref, mn_ref, s_ref):
        u = p_ref[...]
        s_ref[...] = u + mp_ref[...] * (_shift_prev(u) - u) + mn_ref[...] * (_shift_next(u) - u)

    return pl.pallas_call(
        body, name="token_shift", grid=(B, _SHIFT_BLOCKS),
        out_shape=jax.ShapeDtypeStruct((B * T, 1792), F32),
        in_specs=[pl.BlockSpec((T, 256), lambda b, j: (b, _shift_src_col(j))),
                  pl.BlockSpec((1, 256), lambda b, j: (0, j)), pl.BlockSpec((1, 256), lambda b, j: (0, j))],
        out_specs=pl.BlockSpec((T, 256), lambda b, j: (b, j)),
        compiler_params=_cparams(),
    )(p, mu_prev, mu_next)


def _token_shift_bwd(p, ds, mu_prev, mu_next, B, T):
    def body(p_ref, ds_ref, mp_ref, mn_ref, dp_ref, dmp_ref, dmn_ref):
        u, d = p_ref[...], ds_ref[...]
        mp, mn = mp_ref[...], mn_ref[...]
        dp = d * (1.0 - mp - mn) + _shift_next(d * mp) + _shift_prev(d * mn)
        dp_ref[...] = dp.astype(dp_ref.dtype)

        @pl.when(pl.program_id(1) == 0)
        def _():
            dmp_ref[...] = jnp.zeros_like(dmp_ref)
            dmn_ref[...] = jnp.zeros_like(dmn_ref)

        dmp_ref[...] += jnp.sum(d * (_shift_prev(u) - u), axis=0, keepdims=True)
        dmn_ref[...] += jnp.sum(d * (_shift_next(u) - u), axis=0, keepdims=True)

    return pl.pallas_call(
        body, name="token_shift_bwd", grid=(_SHIFT_BLOCKS, B),
        out_shape=[jax.ShapeDtypeStruct((B * T, 1792), BF16), jax.ShapeDtypeStruct((1, 1792), F32),
                   jax.ShapeDtypeStruct((1, 1792), F32)],
        in_specs=[pl.BlockSpec((T, 256), lambda j, b: (b, _shift_src_col(j))),
                  pl.BlockSpec((T, 256), lambda j, b: (b, j)),
                  pl.BlockSpec((1, 256), lambda j, b: (0, j)), pl.BlockSpec((1, 256), lambda j, b: (0, j))],
        out_specs=[pl.BlockSpec((T, 256), lambda j, b: (b, j)), pl.BlockSpec((1, 256), lambda j, b: (0, j)),
                   pl.BlockSpec((1, 256), lambda j, b: (0, j))],
        compiler_params=_cparams(),
    )(p, ds, mu_prev, mu_next)


_FF_BLOCKS = FF_PAD // 256


def _conv3(u, cw, cb):
    return cw[0:1] * _shift_prev(u) + cw[1:2] * u + cw[2:3] * _shift_next(u) + cb


def _ffn_conv(u, cw, cb, B, T):
    def body(ug_ref, uv_ref, cwg_ref, cwv_ref, cbg_ref, cbv_ref, o_ref):
        cg = _conv3(ug_ref[...], cwg_ref[...], cbg_ref[...])
        cv = _conv3(uv_ref[...], cwv_ref[...], cbv_ref[...])
        o_ref[...] = (cg * _sigmoid(cg) * cv).astype(o_ref.dtype)

    nb = _FF_BLOCKS
    return pl.pallas_call(
        body, name="ffn_conv", grid=(B, nb),
        out_shape=jax.ShapeDtypeStruct((B * T, FF_PAD), BF16),
        in_specs=[pl.BlockSpec((T, 256), lambda b, j: (b, j)), pl.BlockSpec((T, 256), lambda b, j: (b, j + nb)),
                  pl.BlockSpec((3, 256), lambda b, j: (0, j)), pl.BlockSpec((3, 256), lambda b, j: (0, j + nb)),
                  pl.BlockSpec((1, 256), lambda b, j: (0, j)), pl.BlockSpec((1, 256), lambda b, j: (0, j + nb))],
        out_specs=pl.BlockSpec((T, 256), lambda b, j: (b, j)),
        compiler_params=_cparams(),
    )(u, u, cw, cw, cb, cb)


def _ffn_conv_bwd(u, dact, cw, cb, B, T):
    def half(u_, dc, cw_):
        du = _shift_next(cw_[0:1] * dc) + cw_[1:2] * dc + _shift_prev(cw_[2:3] * dc)
        dcw = jnp.concatenate([jnp.sum(dc * _shift_prev(u_), axis=0, keepdims=True),
                               jnp.sum(dc * u_, axis=0, keepdims=True),
                               jnp.sum(dc * _shift_next(u_), axis=0, keepdims=True)], axis=0)
        return du, dcw, jnp.sum(dc, axis=0, keepdims=True)

    def body(ug_ref, uv_ref, da_ref, cwg_ref, cwv_ref, cbg_ref, cbv_ref,
             dug_ref, duv_ref, dcwg_ref, dcwv_ref, dcbg_ref, dcbv_ref):
        ug, uv, da = ug_ref[...], uv_ref[...], da_ref[...]
        cwg, cwv = cwg_ref[...], cwv_ref[...]
        cg = _conv3(ug, cwg, cbg_ref[...])
        cv = _conv3(uv, cwv, cbv_ref[...])
        sg = _sigmoid(cg)
        dcv = da * (cg * sg)
        dcg = da * cv * (sg * (1.0 + cg * (1.0 - sg)))
        dug, dcwg, dcbg = half(ug, dcg, cwg)
        duv, dcwv, dcbv = half(uv, dcv, cwv)
        dug_ref[...] = dug.astype(dug_ref.dtype)
        duv_ref[...] = duv.astype(duv_ref.dtype)

        @pl.when(pl.program_id(1) == 0)
        def _():
            for r in (dcwg_ref, dcwv_ref, dcbg_ref, dcbv_ref):
                r[...] = jnp.zeros_like(r)

        dcwg_ref[...] += dcwg
        dcwv_ref[...] += dcwv
        dcbg_ref[...] += dcbg
        dcbv_ref[...] += dcbv

    nb = _FF_BLOCKS
    N = B * T
    res = pl.pallas_call(
        body, name="ffn_conv_bwd", grid=(nb, B),
        out_shape=[jax.ShapeDtypeStruct((N, FF_PAD), BF16), jax.ShapeDtypeStruct((N, FF_PAD), BF16),
                   jax.ShapeDtypeStruct((3, FF_PAD), F32), jax.ShapeDtypeStruct((3, FF_PAD), F32),
                   jax.ShapeDtypeStruct((1, FF_PAD), F32), jax.ShapeDtypeStruct((1, FF_PAD), F32)],
        in_specs=[pl.BlockSpec((T, 256), lambda j, b: (b, j)), pl.BlockSpec((T, 256), lambda j, b: (b, j + nb)),
                  pl.BlockSpec((T, 256), lambda j, b: (b, j)),
                  pl.BlockSpec((3, 256), lambda j, b: (0, j)), pl.BlockSpec((3, 256), lambda j, b: (0, j + nb)),
                  pl.BlockSpec((1, 256), lambda j, b: (0, j)), pl.BlockSpec((1, 256), lambda j, b: (0, j + nb))],
        out_specs=[pl.BlockSpec((T, 256), lambda j, b: (b, j)), pl.BlockSpec((T, 256), lambda j, b: (b, j)),
                   pl.BlockSpec((3, 256), lambda j, b: (0, j)), pl.BlockSpec((3, 256), lambda j, b: (0, j)),
                   pl.BlockSpec((1, 256), lambda j, b: (0, j)), pl.BlockSpec((1, 256), lambda j, b: (0, j))],
        compiler_params=_cparams(),
    )(u, u, dact, cw, cw, cb, cb)
    dug, duv, dcwg, dcwv, dcbg, dcbv = res
    return dug, duv, jnp.concatenate([dcwg, dcwv], axis=1), jnp.concatenate([dcbg, dcbv], axis=1)


def _gla_chunk(q, k, v, afab, wa2, ba, state, rev):
    C = GLA_CHUNK
    z = jnp.dot(afab, wa2, preferred_element_type=F32) + ba
    la = (jnp.minimum(z, 0.0) - jnp.log(1.0 + jnp.exp(-jnp.abs(z)))) * (1.0 / 16.0)
    row = lax.broadcasted_iota(jnp.int32, (C, C), 0)
    col = lax.broadcasted_iota(jnp.int32, (C, C), 1)
    tri = (col - row) * (1 - 2 * rev) <= 0
    b = jnp.dot(tri.astype(F32), la, precision=HIGHEST, preferred_element_type=F32)
    rows = lax.broadcasted_iota(jnp.int32, (C, 256), 0)
    ref_row = jnp.where(rev == 0, C // 2, C - 1 - C // 2)
    last_row = jnp.where(rev == 0, C - 1, 0)
    b_ref = jnp.sum(jnp.where(rows == ref_row, b, 0.0), axis=0, keepdims=True)
    b_last = jnp.sum(jnp.where(rows == last_row, b, 0.0), axis=0, keepdims=True)
    qs = q * 0.125
    qi = qs * jnp.exp(b - b_ref)
    ki = k * jnp.exp(b_ref - b)
    kd = k * jnp.exp(b_last - b)
    qe = qs * jnp.exp(b)
    lane = lax.broadcasted_iota(jnp.int32, (1, 256), 1)
    outs = []
    upd = jnp.zeros_like(state)
    for h in range(4):
        mh = ((lane >= 64 * h) & (lane < 64 * (h + 1))).astype(F32)
        vh = v[:, 128 * h:128 * (h + 1)]
        a = lax.dot_general(qi * mh, ki, (((1,), (1,)), ((), ())), preferred_element_type=F32)
        a = jnp.where(tri, a, 0.0)
        o_h = jnp.dot(a, vh, preferred_element_type=F32)
        o_h = o_h + lax.dot_general(qe, state * mh, (((1,), (1,)), ((), ())), preferred_element_type=F32)
        outs.append(o_h)
        upd = upd + lax.dot_general(vh, kd * mh, (((0,), (0,)), ((), ())), preferred_element_type=F32)
    return jnp.concatenate(outs, axis=1), state * jnp.exp(b_last) + upd


def _gla_chunk_at(nC):
    return lambda d, j: j + d * (nC - 1 - 2 * j)


def _gla_fwd(p, wa2, ba, B, T):
    nC = T // GLA_CHUNK
    N = B * T
    at = _gla_chunk_at(nC)
    p3 = p.reshape(B, T, p.shape[-1])

    def body(q_ref, k_ref, v_ref, af_ref, wa_ref, ba_ref, o_ref, st_ref, state):
        @pl.when(pl.program_id(1) == 0)
        def _():
            state[...] = jnp.zeros_like(state)

        for b in range(B):
            st_ref[0, b, 0] = state[b]
            o, new = _gla_chunk(q_ref[b], k_ref[b], v_ref[b], af_ref[b], wa_ref[0], ba_ref[0], state[b],
                                pl.program_id(0))
            o_ref[0, b] = o
            state[b] = new

    blk = lambda w, col: pl.BlockSpec((B, 64, w), lambda d, j: (0, at(d, j), col))
    o, st = pl.pallas_call(
        body, name="gla_fwd", grid=(2, nC),
        out_shape=[jax.ShapeDtypeStruct((2, B, T, 512), F32), jax.ShapeDtypeStruct((2, B, nC, 128, 256), F32)],
        in_specs=[blk(256, 0), blk(256, 1), blk(512, 1), blk(128, 42),
                  pl.BlockSpec((1, 128, 256), lambda d, j: (d, 0, 0)),
                  pl.BlockSpec((1, 1, 256), lambda d, j: (d, 0, 0))],
        out_specs=[pl.BlockSpec((1, B, 64, 512), lambda d, j: (d, 0, at(d, j), 0)),
                   pl.BlockSpec((1, B, 1, 128, 256), lambda d, j: (d, 0, at(d, j), 0, 0))],
        scratch_shapes=[pltpu.VMEM((B, 128, 256), F32)],
        compiler_params=_cparams(),
    )(p3, p3, p3, p3, wa2, ba)
    return o.reshape(2, N, 512), st


def _gla_bwd(p, do, states, wa2, ba, B, T):
    nC = T // GLA_CHUNK
    N = B * T
    at_f = _gla_chunk_at(nC)
    at = lambda d, j: at_f(d, nC - 1 - j)
    p3 = p.reshape(B, T, p.shape[-1])

    def body(q_ref, k_ref, v_ref, af_ref, wa_ref, ba_ref, do_ref, st_ref,
             dqkv_ref, daf_ref, dwa_ref, dba_ref, dstate):
        rev = pl.program_id(0)

        @pl.when(pl.program_id(1) == 0)
        def _():
            dstate[...] = jnp.zeros_like(dstate)
            dwa_ref[...] = jnp.zeros_like(dwa_ref)
            dba_ref[...] = jnp.zeros_like(dba_ref)

        f = lambda q, k, v, af, wa, bb, st: _gla_chunk(q, k, v, af, wa, bb, st, rev)
        for b in range(B):
            _, vjp = jax.vjp(f, q_ref[b], k_ref[b], v_ref[b], af_ref[b], wa_ref[0], ba_ref[0], st_ref[0, b, 0])
            dq, dk, dv, daf, dwa, dba, dst = vjp((do_ref[b], dstate[b]))
            dqkv_ref[0, b] = jnp.concatenate([dq, dk, dv], axis=1)
            daf_ref[0, b] = daf
            dwa_ref[0] += dwa
            dba_ref[0] += dba
            dstate[b] = dst

    blk = lambda w, col: pl.BlockSpec((B, 64, w), lambda d, j: (0, at(d, j), col))
    out4 = lambda w: pl.BlockSpec((1, B, 64, w), lambda d, j: (d, 0, at(d, j), 0))
    dqkv, daf, dwa, dba = pl.pallas_call(
        body, name="gla_bwd", grid=(2, nC),
        out_shape=[jax.ShapeDtypeStruct((2, B, T, 1024), F32), jax.ShapeDtypeStruct((2, B, T, 128), F32),
                   jax.ShapeDtypeStruct((2, 128, 256), F32), jax.ShapeDtypeStruct((2, 1, 256), F32)],
        in_specs=[blk(256, 0), blk(256, 1), blk(512, 1), blk(128, 42),
                  pl.BlockSpec((1, 128, 256), lambda d, j: (d, 0, 0)),
                  pl.BlockSpec((1, 1, 256), lambda d, j: (d, 0, 0)),
                  blk(512, 0),
                  pl.BlockSpec((1, B, 1, 128, 256), lambda d, j: (d, 0, at(d, j), 0, 0))],
        out_specs=[out4(1024), out4(128),
                   pl.BlockSpec((1, 128, 256), lambda d, j: (d, 0, 0)),
                   pl.BlockSpec((1, 1, 256), lambda d, j: (d, 0, 0))],
        scratch_shapes=[pltpu.VMEM((B, 128, 256), F32)],
        compiler_params=_cparams(),
    )(p3, p3, p3, p3, wa2, ba, do.reshape(B, T, 512), states)
    return dqkv.reshape(2, N, 1024), daf.reshape(2, N, 128), dwa, dba


def _seg_ones():
    m = lax.broadcasted_iota(jnp.int32, (256, 128), 0)
    n = lax.broadcasted_iota(jnp.int32, (256, 128), 1)
    return (((m >> 6) & 1) == (n >> 6)).astype(BF16)


def _seg_mm(x, ones2):
    hi = x.astype(BF16)
    lo = (x - hi.astype(F32)).astype(BF16)
    return jnp.dot(jnp.concatenate([hi, lo], axis=1), ones2, preferred_element_type=F32)


def _diag_matrix():
    r = np.arange(2048)[:, None] % 64
    c = np.arange(128)[None, :] % 64
    return jnp.asarray((r == c).astype(np.float32))


def _rows4(tile, q):
    return jnp.concatenate([jnp.broadcast_to(tile[q:q + 1, 128 * p:128 * (p + 1)], (64, 128)) for p in range(4)],
                           axis=0)


def _head_rows():
    r = lax.broadcasted_iota(jnp.int32, (16, 256), 0)
    n = lax.broadcasted_iota(jnp.int32, (16, 256), 1)
    return (r == ((n >> 6) & 1)).astype(BF16)


def _head_sums_row(x, heads2):
    hi = x.astype(BF16)
    lo = (x - hi.astype(F32)).astype(BF16)
    out = lax.dot_general(heads2, jnp.concatenate([hi, lo], axis=1), (((1,), (1,)), ((), ())),
                          preferred_element_type=F32)
    return jnp.concatenate([out[0:1], out[1:2]], axis=1)


def _pair_major(y):
    n = y.shape[0]
    return y.reshape(n, 2, 4, 64).transpose(0, 2, 1, 3).reshape(n, 512)


def _colsum4(m):
    return jnp.concatenate([jnp.sum(m[64 * p:64 * (p + 1)], axis=0, keepdims=True) for p in range(4)], axis=1)


def _time_base(gi, n_groups, rev):
    return pl.multiple_of(((n_groups - 1 - gi) if rev else gi) * 8, 8)


def _scan_fwd_mxu(s, wf, wb, k2, na, bb, B, T):
    Tc = SCAN_CHUNK
    nT = T // Tc
    nG = Tc // 8
    N = B * T
    nb = _pick(B, (4, 2))
    fwd_j = lambda j: j
    bwd_j = lambda j: nT - 1 - j

    def body(rF, vF, kF, aF, bF, wF, rB, vB, kB, aB, bB, wB, dg, yF, yB, ckF, ckB,
             SF, SB, vcF, vcB, ypF, ypB, ytF, ytB):
        @pl.when(pl.program_id(1) == 0)
        def _():
            SF[...] = jnp.zeros_like(SF)
            SB[...] = jnp.zeros_like(SB)

        ckF[...] = SF[...]
        ckB[...] = SB[...]
        ones2, heads2 = _seg_ones(), _head_rows()
        chains = []
        for n in range(nb):
            chains.append((n, SF, (rF, vF, kF, aF, bF, wF), yF, vcF, ypF, ytF, False))
            chains.append((n, SB, (rB, vB, kB, aB, bB, wB), yB, vcB, ypB, ytB, True))

        def group(gi, carry):
            tiles, states = [], []
            for n, S_ref, refs, _, vc, _, _, rev in chains:
                base = _time_base(gi, nG, rev)
                t = [ref[n, pl.ds(base, 8), :] for ref in refs]
                tiles.append(t)
                states.append(S_ref[n])
                vc[n] = _seg_mm(jnp.concatenate([_rows4(t[1], q) for q in range(8)], axis=0) * dg[...], ones2)
            for i8 in range(8):
                for c, (n, _, _, _, vc, yp, _, rev) in enumerate(chains):
                    q = 7 - i8 if rev else i8
                    r, v, k, a, b, w = tiles[c]
                    S = states[c]
                    sa = _seg_mm(S * _rows4(a, q), ones2)
                    S = S * _rows4(w, q) + sa * _rows4(b, q) + vc[n, 256 * q:256 * (q + 1), :] * _rows4(k, q)
                    yp[n, 256 * q:256 * (q + 1), :] = S * _rows4(r, q)
                    states[c] = S
            for c, (n, S_ref, _, y_ref, _, yp, yt, rev) in enumerate(chains):
                S_ref[n] = states[c]
                for q in range(8):
                    yt[n, q:q + 1, :] = _head_sums_row(yp[n, 256 * q:256 * (q + 1), :], heads2)
                y_ref[n, pl.ds(_time_base(gi, nG, rev), 8), :] = yt[n]
            return carry

        lax.fori_loop(0, nG, group, 0)

    row_in = lambda at, col: pl.BlockSpec((nb, Tc, 512), lambda g, j: (g, at(j), col))
    state_io = lambda at: pl.BlockSpec((nb, 256, 128), lambda g, j: (g, at(j), 0))
    in_specs = []
    for at in (fwd_j, bwd_j):
        in_specs += [row_in(at, 0), row_in(at, 2)] + [row_in(at, 0)] * 4
    big = pltpu.VMEM((nb, 8 * 256, 128), F32)
    s3 = s.reshape(B, T, s.shape[-1])
    seq = lambda a: a.reshape(B, T, 512)
    y_f, y_b, ck_f, ck_b = pl.pallas_call(
        body, name="rwkv_scan", grid=(B // nb, nT),
        out_shape=[jax.ShapeDtypeStruct((B, T, 512), F32), jax.ShapeDtypeStruct((B, T, 512), F32),
                   jax.ShapeDtypeStruct((B, nT * 256, 128), F32), jax.ShapeDtypeStruct((B, nT * 256, 128), F32)],
        in_specs=in_specs + [pl.BlockSpec((2048, 128), lambda g, j: (0, 0))],
        out_specs=[row_in(fwd_j, 0), row_in(bwd_j, 0), state_io(fwd_j), state_io(bwd_j)],
        scratch_shapes=[pltpu.VMEM((nb, 256, 128), F32), pltpu.VMEM((nb, 256, 128), F32), big, big, big, big,
                        pltpu.VMEM((nb, 8, 512), F32), pltpu.VMEM((nb, 8, 512), F32)],
        compiler_params=_cparams(),
    )(s3, s3, seq(k2), seq(na), seq(bb), seq(wf), s3, s3, seq(k2), seq(na), seq(bb), seq(wb), _diag_matrix())
    ck_shape = (B * nT * 256, 128)
    return (_pair_major(y_f.reshape(N, 512)), _pair_major(y_b.reshape(N, 512)),
            ck_f.reshape(ck_shape), ck_b.reshape(ck_shape))


def _scan_bwd_mxu(s, wf, wb, k2, na, bb, dy, ckF, ckB, B, T):
    Tc = SCAN_CHUNK
    nT = T // Tc
    nG = Tc // 8
    N = B * T
    nb = _pick(B, (2,))
    f_at = lambda j: nT - 1 - j
    b_at = lambda j: j
    n_in, n_out, n_scr = 17, 12, 12

    def body(*refs):
        (rF, vF, kF, aF, bF, wF, dyF, ckF_ref, rB, vB, kB, aB, bB, wB, dyB, ckB_ref, dg) = refs[:n_in]
        outsF, outsB = refs[n_in:n_in + 6], refs[n_in + 6:n_in + n_out]
        chains = []
        for n in range(nb):
            stF, stB, saF, saB, vcF, vcB, dSF, dSB, bigF, bigB, tileF, tileB = \
                refs[n_in + n_out + n_scr * n:n_in + n_out + n_scr * (n + 1)]
            chains.append((n, stF, dSF, ckF_ref, (rF, vF, kF, aF, bF, wF, dyF), outsF, bigF, tileF, False, saF, vcF))
            chains.append((n, stB, dSB, ckB_ref, (rB, vB, kB, aB, bB, wB, dyB), outsB, bigB, tileB, True, saB, vcB))

        @pl.when(pl.program_id(1) == 0)
        def _():
            for chain in chains:
                chain[2][...] = jnp.zeros_like(chain[2])

        ones2, heads2 = _seg_ones(), _head_rows()
        for chain in chains:
            chain[1][0] = chain[3][chain[0]]

        def cols8(tile):
            return _seg_mm(jnp.concatenate([_rows4(tile, q) for q in range(8)], axis=0) * dg[...], ones2)

        def recompute(gi, carry):
            tiles, states = [], []
            for n, st, _, _, ins, _, big, _, rev, _, vc_keep in chains:
                base = _time_base(gi, nG, rev)
                t = [ref[n, pl.ds(base, 8), :] for ref in ins[1:6]]
                tiles.append(t)
                states.append(st[gi * 8])
                v_cols = cols8(t[0])
                for i8 in range(8):
                    q = 7 - i8 if rev else i8
                    vc_keep[gi * 8 + i8] = v_cols[256 * q:256 * (q + 1)]
            for i8 in range(8):
                for c, (_, st, _, _, _, _, _, _, rev, sa_keep, vc_keep) in enumerate(chains):
                    q = 7 - i8 if rev else i8
                    v, k, a, b, w = tiles[c]
                    S = states[c]
                    sa = _seg_mm(S * _rows4(a, q), ones2)
                    sa_keep[gi * 8 + i8] = sa
                    S = S * _rows4(w, q) + sa * _rows4(b, q) + vc_keep[gi * 8 + i8] * _rows4(k, q)
                    st[gi * 8 + i8 + 1] = S
                    states[c] = S
            return carry

        lax.fori_loop(0, nG, recompute, 0)

        def back(gg, carry):
            gi = nG - 1 - gg
            tiles, grads = [], []
            for n, st, dS_ref, _, ins, _, big, _, rev, _, _ in chains:
                base = _time_base(gi, nG, rev)
                t = [ref[n, pl.ds(base, 8), :] for ref in ins]
                tiles.append(t)
                grads.append(dS_ref[...])
                big[0] = cols8(t[6])
            for i8 in range(7, -1, -1):
                for c, (_, st, _, _, _, _, big, tile, rev, sa_keep, vc_keep) in enumerate(chains):
                    q = 7 - i8 if rev else i8
                    r, v, k, a, b, w, _ = tiles[c]
                    i = gi * 8 + i8
                    S_prev, S_t = st[i], st[i + 1]
                    rows = slice(256 * q, 256 * (q + 1))
                    dy_col, v_col, sa = big[0, rows, :], vc_keep[i], sa_keep[i]
                    dS = grads[c] + dy_col * _rows4(r, q)
                    sb = _seg_mm(dS * _rows4(b, q), ones2)
                    big[1, rows, :] = dS * _rows4(k, q)
                    tile[0, q:q + 1, :] = _colsum4(S_t * dy_col)
                    tile[2, q:q + 1, :] = _colsum4(dS * v_col)
                    tile[3, q:q + 1, :] = _colsum4(S_prev * sb)
                    tile[4, q:q + 1, :] = _colsum4(dS * sa)
                    tile[5, q:q + 1, :] = _colsum4(S_prev * dS)
                    grads[c] = dS * _rows4(w, q) + sb * _rows4(a, q)
            for c, (n, _, dS_ref, _, _, outs, big, tile, rev, _, _) in enumerate(chains):
                dS_ref[...] = grads[c]
                for q in range(8):
                    tile[1, q:q + 1, :] = _head_sums_row(big[1, 256 * q:256 * (q + 1), :], heads2)
                base = _time_base(gi, nG, rev)
                for o, o_ref in enumerate(outs):
                    o_ref[n, pl.ds(base, 8), :] = tile[o]
            return carry

        lax.fori_loop(0, nG, back, 0)

    row_io = lambda at, col: pl.BlockSpec((nb, Tc, 512), lambda g, j: (g, at(j), col))
    in_specs = []
    for at in (f_at, b_at):
        in_specs += [row_io(at, 0), row_io(at, 2)] + [row_io(at, 0)] * 5
        in_specs.append(pl.BlockSpec((nb, 256, 128), lambda g, j, at=at: (g, at(j), 0)))
    in_specs.append(pl.BlockSpec((2048, 128), lambda g, j: (0, 0)))
    out_specs = [row_io(f_at, 0)] * 6 + [row_io(b_at, 0)] * 6
    big = pltpu.VMEM((2, 8 * 256, 128), F32)
    states = pltpu.VMEM((Tc + 1, 256, 128), F32)
    per_step = pltpu.VMEM((Tc, 256, 128), F32)
    one_slot = [states, states, per_step, per_step, per_step, per_step,
                pltpu.VMEM((256, 128), F32), pltpu.VMEM((256, 128), F32), big, big,
                pltpu.VMEM((6, 8, 512), F32), pltpu.VMEM((6, 8, 512), F32)]
    s3 = s.reshape(B, T, s.shape[-1])
    seq = lambda a: a.reshape(B, T, 512)
    ck3 = lambda a: a.reshape(B, nT * 256, 128)
    outs = pl.pallas_call(
        body, name="rwkv_scan_bwd", grid=(B // nb, nT),
        out_shape=[jax.ShapeDtypeStruct((B, T, 512), F32)] * 12,
        in_specs=in_specs, out_specs=out_specs,
        scratch_shapes=one_slot * nb,
        compiler_params=_cparams(),
    )(s3, s3, seq(k2), seq(na), seq(bb), seq(wf), seq(dy), ck3(ckF),
      s3, s3, seq(k2), seq(na), seq(bb), seq(wb), seq(dy), ck3(ckB), _diag_matrix())
    outs = [o.reshape(N, 512) for o in outs]
    outs[1], outs[7] = _pair_major(outs[1]), _pair_major(outs[7])
    return outs


def _cat_shards(g4, name, axis):
    return jnp.concatenate([g4[s][name] for s in range(4)], axis=axis)


def _split_shards(full, axis):
    return jnp.split(full, 4, axis=axis)


def kernel(x, norm1_g, w_in, gla_wa2_f, gla_ba_f, gla_wa2_b, gla_ba_b, gla_norm_g, gla_proj, rwkv_mu_prev, rwkv_mu_next, rwkv_w0_f, rwkv_w2_f, rwkv_w0_b, rwkv_w2_b, rwkv_a0, rwkv_a2, rwkv_g2, rwkv_k_k, rwkv_k_a, rwkv_r_k, rwkv_ln_w, rwkv_ln_b, rwkv_proj, w_out, norm2_g, ffn_up, ffn_conv_w, ffn_conv_b, ffn_down, norm_f_g, loss_target, m_norm1_g, m_w_in, m_gla_wa2_f, m_gla_ba_f, m_gla_wa2_b, m_gla_ba_b, m_gla_norm_g, m_gla_proj, m_rwkv_mu_prev, m_rwkv_mu_next, m_rwkv_w0_f, m_rwkv_w2_f, m_rwkv_w0_b, m_rwkv_w2_b, m_rwkv_a0, m_rwkv_a2, m_rwkv_g2, m_rwkv_k_k, m_rwkv_k_a, m_rwkv_r_k, m_rwkv_ln_w, m_rwkv_ln_b, m_rwkv_proj, m_w_out, m_norm2_g, m_ffn_up, m_ffn_conv_w, m_ffn_conv_b, m_ffn_down, m_norm_f_g, v_norm1_g, v_w_in, v_gla_wa2_f, v_gla_ba_f, v_gla_wa2_b, v_gla_ba_b, v_gla_norm_g, v_gla_proj, v_rwkv_mu_prev, v_rwkv_mu_next, v_rwkv_w0_f, v_rwkv_w2_f, v_rwkv_w0_b, v_rwkv_w2_b, v_rwkv_a0, v_rwkv_a2, v_rwkv_g2, v_rwkv_k_k, v_rwkv_k_a, v_rwkv_r_k, v_rwkv_ln_w, v_rwkv_ln_b, v_rwkv_proj, v_w_out, v_norm2_g, v_ffn_up, v_ffn_conv_w, v_ffn_conv_b, v_ffn_down, v_norm_f_g):
    args = locals()
    weights = {n: args[n] for n in WEIGHT_ORDER}
    mom_m = {n: args["m_" + n] for n in WEIGHT_ORDER}
    mom_v = {n: args["v_" + n] for n in WEIGHT_ORDER}
    shapes = {n: weights[n].shape for n in WEIGHT_ORDER}
    B, T, _ = x.shape
    N = B * T
    tm = _pick(N, (512,))

    def local(d):
        sh = {n: d[n].reshape(s) for n, s, _ in SHARDED}
        rp = {n: d[n].reshape(-1) for n, _ in REPLICATED}
        return sh, rp

    w_loc, m_loc, v_loc = local(weights), local(mom_m), local(mom_v)

    gathered = _exchange_chips([w_loc[0][n].astype(BF16) for n, _, _ in BIG] + [_pack(*w_loc)],
                               "allgather_weights", gather=True)
    small_vals = [_unpack(gathered[-1][s]) for s in range(4)]
    W = {n: jnp.concatenate([gathered[i][s] for s in range(4)], axis=ax) for i, (n, _, ax) in enumerate(BIG)}
    W.update({n: _cat_shards(small_vals, n, ax) for n, _, ax in SMALL})
    R = {n: weights[n].reshape(1, -1) for n, _ in REPLICATED}

    zc = lambda r, c, dt=F32: jnp.zeros((r, c), dt)
    w_in_full = W["w_in"]
    w_in_p = jnp.concatenate([w_in_full[:, 0:1536], w_in_full[:, 1568:3104], w_in_full[:, 3360:5408],
                              w_in_full[:, 3104:3360], w_in_full[:, 1536:1568],
                              zc(1024, PROJ_PAD - N_PROJ, BF16)], axis=1)
    w_in_b = w_in_p
    pad_ff = lambda a: jnp.concatenate([a[:, :D_FF], zc(a.shape[0], FF_PAD - D_FF, a.dtype), a[:, D_FF:],
                                        zc(a.shape[0], FF_PAD - D_FF, a.dtype)], axis=1)
    ffn_up_p = pad_ff(W["ffn_up"])
    ffn_up_b = ffn_up_p
    conv_w_p = pad_ff(W["ffn_conv_w"])
    conv_b_p = pad_ff(R["ffn_conv_b"])
    ffn_down_p = jnp.concatenate([W["ffn_down"], zc(FF_PAD - D_FF, 1024, BF16)], axis=0)
    ffn_down_b = ffn_down_p
    w_out_b = W["w_out"]
    gla_proj_b = W["gla_proj"]
    rwkv_proj_b = W["rwkv_proj"]
    wa2 = jnp.stack([jnp.concatenate([W["gla_wa2_f"], zc(112, 256)], axis=0),
                     jnp.concatenate([zc(16, 256), W["gla_wa2_b"], zc(96, 256)], axis=0)])
    ba = jnp.stack([R["gla_ba_f"], R["gla_ba_b"]])
    w2_f = jnp.concatenate([W["rwkv_w2_f"], zc(64, 512)], axis=0)
    w2_b = jnp.concatenate([W["rwkv_w2_b"], zc(64, 512)], axis=0)
    a2 = jnp.concatenate([zc(64, 512), W["rwkv_a2"]], axis=0)
    g2 = W["rwkv_g2"]
    head_ones = np.kron(np.eye(8, dtype=np.float32), np.ones((64, 64), np.float32))
    seg64 = jnp.asarray(np.concatenate([head_ones, head_ones], axis=0), dtype=BF16)

    x2d = x.reshape(N, D_MODEL)
    tgt = loss_target.reshape(N, D_MODEL)

    (h1,) = _rowwise("norm1", _fn_norm, [(x2d, 1024, 0)], [R["norm1_g"]], [], [(1024, BF16)], N, tm)
    p = _matmul(h1, w_in_b, "proj_in")
    o_gla, gla_states = _gla_fwd(p, wa2, ba, B, T)
    gla_post_rows = [(o_gla[0], 512, 0), (o_gla[1], 512, 0), (p, 512, 2)]
    (gated,) = _rowwise("gla_post", _fn_gla_post, gla_post_rows, [R["gla_norm_g"]], [], [(512, BF16)], N, tm)
    y_a = _matmul(gated, gla_proj_b, "gla_out")
    s = _token_shift(p, R["rwkv_mu_prev"], R["rwkv_mu_next"], B, T)
    pre_rows = [(s, 512, 1), (s, 256, 6)]
    pre_params = [R["rwkv_w0_f"], R["rwkv_w0_b"], R["rwkv_a0"], R["rwkv_k_k"], R["rwkv_k_a"], w2_f, w2_b, a2, g2]
    wf, wb, k2, na, bb, g = _rowwise("rwkv_pre", _fn_rwkv_pre, pre_rows, pre_params, [seg64],
                                     [(512, F32)] * 6, N, tm)
    y_f, y_b, ck_f, ck_b = _scan_fwd_mxu(s, wf, wb, k2, na, bb, B, T)
    post_rows = [(y_f, 512, 0), (y_b, 512, 0), (s, 512, 0), (k2, 512, 0), (s, 512, 2), (g, 512, 0)]
    post_params = [R["rwkv_ln_w"], R["rwkv_ln_b"], R["rwkv_r_k"]]
    (o_rwkv,) = _rowwise("rwkv_post", _fn_rwkv_post, post_rows, post_params, [seg64], [(512, BF16)], N, tm)
    y_r = _matmul(o_rwkv, rwkv_proj_b, "rwkv_out")
    merge_rows = [(p, 1024, 3), (p, 1024, 4), (y_a, 1024, 0), (y_r, 1024, 0)]
    (merged,) = _rowwise("merge", _fn_merge, merge_rows, [], [], [(1024, BF16)], N, tm)
    x1 = _matmul(merged, w_out_b, "mix_out", residual=x2d)
    (h2,) = _rowwise("norm2", _fn_norm, [(x1, 1024, 0)], [R["norm2_g"]], [], [(1024, BF16)], N, tm)
    u = _matmul(h2, ffn_up_b, "ffn_up")
    act = _ffn_conv(u, conv_w_p, conv_b_p, B, T)
    x2 = _matmul(act, ffn_down_b, "ffn_down", residual=x1)
    loss_blk, dx2, d_norm_f = _loss_head(x2, tgt, weights["norm_f_g"].reshape(1, -1))

    dx2_b = dx2.astype(BF16)
    d_act = _matmul(dx2_b, ffn_down_p.T.astype(BF16), "d_act")
    d_ffn_down = _matmul_tn(act, dx2_b, "dw_ffn_down")[:D_FF]
    du_g, du_v, d_conv_w_p, d_conv_b_p = _ffn_conv_bwd(u, d_act, conv_w_p, conv_b_p, B, T)
    up_t = ffn_up_p.T.astype(BF16)
    d_h2 = _matmul(du_v, up_t[FF_PAD:], "d_h2_v", residual=_matmul(du_g, up_t[:FF_PAD], "d_h2_g"))
    d_ffn_up = jnp.concatenate([_matmul_tn(h2, du_g, "dw_ffn_up_g")[:, :D_FF],
                                _matmul_tn(h2, du_v, "dw_ffn_up_v")[:, :D_FF]], axis=1)
    unpad_ff = lambda a: jnp.concatenate([a[:, :D_FF], a[:, FF_PAD:FF_PAD + D_FF]], axis=1)
    (dx1,), (d_norm2,) = _rowwise_bwd("norm2_bwd", _fn_norm, [(x1, 1024, 0)], [R["norm2_g"]], [],
                                      [[(d_h2, 1024, 0)]], [(F32, (dx2, 1024, 0))], N, tm)
    dx1_b = dx1.astype(BF16)
    d_merged = _matmul(dx1_b, W["w_out"].T.astype(BF16), "d_merged")
    d_w_out = _matmul_tn(merged, dx1_b, "dw_out")
    (d_ga, d_gb, d_ya, d_yr), _ = _rowwise_bwd("merge_bwd", _fn_merge, merge_rows, [], [],
                                               [[(d_merged, 1024, 0)]], [(BF16, None)] * 4, N, tm)
    d_o_rwkv = _matmul(d_yr, W["rwkv_proj"].T.astype(BF16), "d_o_rwkv")
    d_rwkv_proj = _matmul_tn(o_rwkv, d_yr, "dw_rwkv_proj")
    (d_y, d_r_bonus, d_k2_bonus, d_v_bonus, d_g), (d_ln_w, d_ln_b, d_r_k) = _rowwise_bwd(
        "rwkv_post_bwd", _fn_rwkv_post, post_rows, post_params, [seg64], [[(d_o_rwkv, 512, 0)]],
        [(F32, None), None, (F32, None), (F32, None), (F32, None), (F32, None)], N, tm)
    (drF, dvF, dkF, daF, dbF, dwF, drB, dvB, dkB, daB, dbB, dwB) = _scan_bwd_mxu(s, wf, wb, k2, na, bb, d_y, ck_f, ck_b, B, T)
    pre_cts = [[(dwF, 512, 0)], [(dwB, 512, 0)], [(dkF, 512, 0), (dkB, 512, 0), (d_k2_bonus, 512, 0)],
               [(daF, 512, 0), (daB, 512, 0)], [(dbF, 512, 0), (dbB, 512, 0)], [(d_g, 512, 0)]]
    (ds_k, ds_wag), pre_grads = _rowwise_bwd("rwkv_pre_bwd", _fn_rwkv_pre, pre_rows, pre_params, [seg64], pre_cts,
                                             [(F32, None), (F32, None)], N, tm)
    d_w0_f, d_w0_b, d_a0, d_k_k, d_k_a, d_w2_f, d_w2_b, d_a2, d_g2 = pre_grads
    ds = jnp.concatenate([drF + drB + d_r_bonus, ds_k, dvF + dvB + d_v_bonus, ds_wag], axis=1)
    d_p_rwkv, d_mu_prev, d_mu_next = _token_shift_bwd(p, ds, R["rwkv_mu_prev"], R["rwkv_mu_next"], B, T)
    d_gated = _matmul(d_ya, W["gla_proj"].T.astype(BF16), "d_gated")
    d_gla_proj = _matmul_tn(gated, d_ya, "dw_gla_proj")
    (d_o, d_og), (d_gla_norm,) = _rowwise_bwd(
        "gla_post_bwd", _fn_gla_post, gla_post_rows, [R["gla_norm_g"]], [], [[(d_gated, 512, 0)]],
        [(F32, None), None, (BF16, None)], N, tm)
    dqkv2, dafab2, d_wa2, d_ba = _gla_bwd(p, d_o, gla_states, wa2, ba, B, T)
    add2 = lambda a, b: (a + b,)
    (d_qkv,) = _rowwise("sum_dqkv", add2, [(dqkv2[0], 1024, 0), (dqkv2[1], 1024, 0)], [], [], [(1024, BF16)], N, tm)
    (d_afab,) = _rowwise("sum_dafab", add2, [(dafab2[0], 128, 0), (dafab2[1], 128, 0)], [], [], [(128, BF16)], N, tm)
    w_in_t = w_in_p.T.astype(BF16)
    w_rwkv_t = jnp.concatenate([w_in_t[1536:3072], w_in_t[5120:5376]], axis=0)
    d_h1 = _matmul(d_qkv, w_in_t[0:1024], "d_h1_qkv")
    d_h1 = _matmul(d_og, w_in_t[1024:1536], "d_h1_og", residual=d_h1)
    d_h1 = _matmul(d_p_rwkv, w_rwkv_t, "d_h1_rwkv", residual=d_h1)
    d_h1 = _matmul(d_ga, w_in_t[3072:4096], "d_h1_ga", residual=d_h1)
    d_h1 = _matmul(d_gb, w_in_t[4096:5120], "d_h1_gb", residual=d_h1)
    d_h1 = _matmul(d_afab, w_in_t[5376:5504], "d_h1_afab", residual=d_h1)
    d_w_in = jnp.concatenate([
        _matmul_tn(h1, d_qkv, "dw_in_qkv"), _matmul_tn(h1, d_og, "dw_in_og"),
        _matmul_tn(h1, d_afab, "dw_in_afab")[:, :32], _matmul_tn(h1, d_p_rwkv, "dw_in_rwkv"),
        _matmul_tn(h1, d_ga, "dw_in_ga"), _matmul_tn(h1, d_gb, "dw_in_gb")], axis=1)
    (grad_x,), (d_norm1,) = _rowwise_bwd("norm1_bwd", _fn_norm, [(x2d, 1024, 0)], [R["norm1_g"]], [],
                                         [[(d_h1, 1024, 0)]], [(F32, (dx1, 1024, 0))], N, tm)

    full_grads = {
        "w_in": d_w_in, "gla_wa2_f": d_wa2[0, 0:16], "gla_wa2_b": d_wa2[1, 16:32], "gla_proj": d_gla_proj,
        "rwkv_w2_f": d_w2_f[0:64], "rwkv_w2_b": d_w2_b[0:64], "rwkv_a2": d_a2[64:128], "rwkv_g2": d_g2,
        "rwkv_proj": d_rwkv_proj, "w_out": d_w_out, "ffn_up": d_ffn_up, "ffn_conv_w": unpad_ff(d_conv_w_p),
        "ffn_down": d_ffn_down,
    }
    repl_grads = {
        "norm1_g": d_norm1, "gla_ba_f": d_ba[0], "gla_ba_b": d_ba[1], "gla_norm_g": d_gla_norm,
        "rwkv_mu_prev": d_mu_prev, "rwkv_mu_next": d_mu_next, "rwkv_w0_f": d_w0_f, "rwkv_w0_b": d_w0_b,
        "rwkv_a0": d_a0, "rwkv_k_k": d_k_k, "rwkv_k_a": d_k_a, "rwkv_r_k": d_r_k, "rwkv_ln_w": d_ln_w,
        "rwkv_ln_b": d_ln_b, "norm2_g": d_norm2, "ffn_conv_b": unpad_ff(d_conv_b_p), "norm_f_g": d_norm_f,
    }
    split = {n: _split_shards(full_grads[n], ax) for n, _, ax in SHARDED}
    to_owners = [jnp.stack([p.astype(BF16) for p in split[n]]) for n, _, _ in BIG]
    to_owners.append(jnp.stack([_pack({n: split[n][sidx] for n, _, _ in SMALL}, repl_grads, loss_blk[0, 0])
                                for sidx in range(4)]))
    received = _exchange_chips(to_owners, "scatter_grads", gather=False)
    names = [n for n, _, _ in BIG] + ["small"]
    mine = [_sum_sources(r, "sum_" + n) for r, n in zip(received, names)]
    other = _swap_with_sibling(mine)
    packs = [_pack(*w_loc), _pack(*m_loc), _pack(*v_loc)]
    results = {}
    for i, n in enumerate(names):
        wmv = packs if n == "small" else [d[0][n] for d in (w_loc, m_loc, v_loc)]
        results[n] = _adamw(mine[i], other[i], *wmv, "adamw_" + n)
    small = [_unpack(f) for f in results["small"]]
    outs = [small[0]["loss"], grad_x.reshape(B, T, D_MODEL)]
    for kind in range(4):
        for n in WEIGHT_ORDER:
            val = results[n][kind] if n in results else small[kind][n]
            outs.append(val.reshape(shapes[n]))
    return tuple(outs)
```

```python
import functools

import jax
import jax.numpy as jnp
import numpy as np
from jax import lax
from jax.experimental import pallas as pl
from jax.experimental.pallas import tpu as pltpu

F32 = jnp.float32
BF16 = jnp.bfloat16
HIGHEST = lax.Precision.HIGHEST
MESH_IDS = pl.DeviceIdType.MESH

D_MODEL = 1024
N_PROJ = 5408
PROJ_PAD = 5632
D_FF = 2752
FF_PAD = 2816
GLA_CHUNK = 64
SCAN_CHUNK = 16
NORM_EPS = 1e-6
HEAD_NORM_EPS = 1e-5
RWKV_GN_EPS = 64 * 1e-5
ADAM_LR, ADAM_B1, ADAM_B2, ADAM_EPS, ADAM_WD, ADAM_STEP = 0.001, 0.9, 0.999, 1e-08, 0.01, 10
VMEM_LIMIT = 56 * 1024 * 1024

FLAT_ROWS, FLAT_COLS = 128, 1024
BIG = (
    ("w_in", (1024, 1352), 1), ("gla_proj", (512, 256), 1), ("rwkv_proj", (512, 256), 1),
    ("w_out", (256, 1024), 0), ("ffn_up", (1024, 1376), 1), ("ffn_down", (688, 1024), 0),
)
SMALL = (
    ("gla_wa2_f", (16, 64), 1), ("gla_wa2_b", (16, 64), 1), ("rwkv_w2_f", (64, 128), 1),
    ("rwkv_w2_b", (64, 128), 1), ("rwkv_a2", (64, 128), 1), ("rwkv_g2", (128, 128), 1),
    ("ffn_conv_w", (3, 1376), 1),
)
SHARDED = BIG + SMALL
REPLICATED = (
    ("norm1_g", 1024), ("gla_ba_f", 256), ("gla_ba_b", 256), ("gla_norm_g", 512),
    ("rwkv_mu_prev", 1792), ("rwkv_mu_next", 1792), ("rwkv_w0_f", 512), ("rwkv_w0_b", 512),
    ("rwkv_a0", 512), ("rwkv_k_k", 512), ("rwkv_k_a", 512), ("rwkv_r_k", 512),
    ("rwkv_ln_w", 512), ("rwkv_ln_b", 512), ("norm2_g", 1024), ("ffn_conv_b", 5504),
    ("norm_f_g", 1024),
)
WEIGHT_ORDER = ("norm1_g", "w_in", "gla_wa2_f", "gla_ba_f", "gla_wa2_b", "gla_ba_b", "gla_norm_g", "gla_proj",
                "rwkv_mu_prev", "rwkv_mu_next", "rwkv_w0_f", "rwkv_w2_f", "rwkv_w0_b", "rwkv_w2_b", "rwkv_a0",
                "rwkv_a2", "rwkv_g2", "rwkv_k_k", "rwkv_k_a", "rwkv_r_k", "rwkv_ln_w", "rwkv_ln_b", "rwkv_proj",
                "w_out", "norm2_g", "ffn_up", "ffn_conv_w", "ffn_conv_b", "ffn_down", "norm_f_g")


def _cparams(**kw):
    return pltpu.CompilerParams(vmem_limit_bytes=VMEM_LIMIT, **kw)


def _pack(sharded_vals, repl_vals, loss=None):
    parts = [sharded_vals[n].reshape(-1) for n, _, _ in SMALL]
    parts += [repl_vals[n].reshape(-1) for n, _ in REPLICATED]
    parts.append(jnp.zeros((1,), F32) if loss is None else loss.reshape(1))
    used = sum(int(np.prod(s)) for _, s, _ in SMALL) + sum(w for _, w in REPLICATED) + 1
    parts.append(jnp.zeros((FLAT_ROWS * FLAT_COLS - used,), F32))
    return jnp.concatenate(parts).reshape(FLAT_ROWS, FLAT_COLS)


def _unpack(flat):
    v = flat.reshape(-1)
    out, off = {}, 0
    for n, s, _ in SMALL:
        k = int(np.prod(s))
        out[n] = v[off:off + k].reshape(s)
        off += k
    for n, w in REPLICATED:
        out[n] = v[off:off + w]
        off += w
    out["loss"] = v[off]
    return out


def _chip_peers():
    x, y, c = lax.axis_index("x"), lax.axis_index("y"), lax.axis_index("c")
    return x, y, c, ((1 - x, y), (x, 1 - y), (1 - x, 1 - y))


def _exchange_chips(arrs, name, gather):
    n = len(arrs)

    def body(*refs):
        srcs, outs = refs[:n], refs[n:2 * n]
        send_sems, recv_sems, local_sems = refs[2 * n:]
        x, y, c, peers = _chip_peers()
        me = 2 * x + y
        own = []
        for i in range(n):
            cp = pltpu.make_async_copy(srcs[i] if gather else srcs[i].at[me], outs[i].at[me], local_sems.at[i])
            cp.start()
            own.append(cp)
        sends = []
        for k, (px, py) in enumerate(peers):
            for i in range(n):
                cp = pltpu.make_async_remote_copy(
                    src_ref=srcs[i] if gather else srcs[i].at[2 * px + py], dst_ref=outs[i].at[me],
                    send_sem=send_sems.at[3 * i + k], recv_sem=recv_sems.at[3 * i + k],
                    device_id=(px, py, c), device_id_type=MESH_IDS)
                cp.start()
                sends.append(cp)
        for k, (px, py) in enumerate(peers):
            for i in range(n):
                pltpu.make_async_remote_copy(
                    src_ref=srcs[i] if gather else srcs[i].at[me], dst_ref=outs[i].at[2 * px + py],
                    send_sem=send_sems.at[3 * i + k], recv_sem=recv_sems.at[3 * i + k],
                    device_id=(px, py, c), device_id_type=MESH_IDS).wait_recv()
        for cp in sends:
            cp.wait_send()
        for cp in own:
            cp.wait()

    out_shape = [jax.ShapeDtypeStruct(((4,) + a.shape) if gather else a.shape, a.dtype) for a in arrs]
    return pl.pallas_call(
        body, name=name, out_shape=out_shape,
        in_specs=[pl.BlockSpec(memory_space=pl.ANY)] * n,
        out_specs=[pl.BlockSpec(memory_space=pl.ANY)] * n,
        scratch_shapes=[pltpu.SemaphoreType.DMA((3 * n,)), pltpu.SemaphoreType.DMA((3 * n,)),
                        pltpu.SemaphoreType.DMA((n,))],
    )(*arrs)


def _swap_with_sibling(arrs):
    n = len(arrs)

    def body(*refs):
        srcs, outs = refs[:n], refs[n:2 * n]
        send_sems, recv_sems = refs[2 * n:]
        x, y, c = lax.axis_index("x"), lax.axis_index("y"), lax.axis_index("c")
        cps = [pltpu.make_async_remote_copy(src_ref=srcs[i], dst_ref=outs[i], send_sem=send_sems.at[i],
                                            recv_sem=recv_sems.at[i], device_id=(x, y, 1 - c),
                                            device_id_type=MESH_IDS) for i in range(n)]
        for cp in cps:
            cp.start()
        for cp in cps:
            cp.wait()

    return pl.pallas_call(
        body, name="swap_sibling",
        out_shape=[jax.ShapeDtypeStruct(a.shape, a.dtype) for a in arrs],
        in_specs=[pl.BlockSpec(memory_space=pl.ANY)] * n,
        out_specs=[pl.BlockSpec(memory_space=pl.ANY)] * n,
        scratch_shapes=[pltpu.SemaphoreType.DMA((n,)), pltpu.SemaphoreType.DMA((n,))],
    )(*arrs)


def _row_tile(rows, cols):
    cap = max(8, (3 << 19) // (4 * (-(-cols // 128) * 128)))
    best = None
    for t in range(8, min(rows, cap) + 1, 8):
        if rows % t == 0:
            best = t
    return best or rows


def _sum_sources(r4, name):
    _, A, Bc = r4.shape
    ta = _row_tile(A, Bc)

    def body(r_ref, o_ref):
        f = lambda s: r_ref[s].astype(F32)
        o_ref[...] = ((f(0) + f(1)) + f(2)) + f(3)

    return pl.pallas_call(
        body, name=name, grid=(A // ta,),
        out_shape=jax.ShapeDtypeStruct((A, Bc), F32),
        in_specs=[pl.BlockSpec((4, ta, Bc), lambda i: (0, i, 0))],
        out_specs=pl.BlockSpec((ta, Bc), lambda i: (i, 0)),
        compiler_params=_cparams(),
    )(r4)


def _adamw(own, other, w, m, v, name):
    R, C = own.shape
    tr = _row_tile(R, C)

    def body(a_ref, b_ref, w_ref, m_ref, v_ref, g_out, d_out, m_out, v_out):
        g = a_ref[...] + b_ref[...]
        m_new = ADAM_B1 * m_ref[...] + (1.0 - ADAM_B1) * g
        v_new = ADAM_B2 * v_ref[...] + (1.0 - ADAM_B2) * (g * g)
        m_hat = m_new / (1.0 - ADAM_B1 ** ADAM_STEP)
        v_hat = v_new / (1.0 - ADAM_B2 ** ADAM_STEP)
        g_out[...] = g
        d_out[...] = -ADAM_LR * (m_hat / (jnp.sqrt(v_hat) + ADAM_EPS) + ADAM_WD * w_ref[...])
        m_out[...] = m_new
        v_out[...] = v_new

    spec = pl.BlockSpec((tr, C), lambda i: (i, 0))
    return pl.pallas_call(
        body, name=name, grid=(R // tr,),
        out_shape=[jax.ShapeDtypeStruct((R, C), F32)] * 4,
        in_specs=[spec] * 5, out_specs=[spec] * 4,
        compiler_params=_cparams(),
    )(own, other, w, m, v)


def _pick(n, options):
    for o in options:
        if n % o == 0:
            return o
    return n


def _div128(n, cap):
    best = None
    for t in range(128, min(n, cap) + 1, 128):
        if n % t == 0:
            best = t
    return best or n


MATMUL_VMEM = 40 * 1024 * 1024


def _matmul(a, b, name, out_dtype=F32, residual=None):
    M, K = a.shape
    _, N = b.shape
    tm, tn = _pick(M, (1024, 512)), _div128(N, 1408)
    while tm > 256 and 2 * (2 * tm * K + 2 * K * tn + (8 if residual is not None else 4) * tm * tn) > MATMUL_VMEM:
        tm //= 2

    def body(*refs):
        a_ref, b_ref = refs[0], refs[1]
        o_ref = refs[-1]
        acc = jnp.dot(a_ref[...], b_ref[...], preferred_element_type=F32)
        if residual is not None:
            acc = acc + refs[2][...]
        o_ref[...] = acc.astype(out_dtype)

    in_specs = [pl.BlockSpec((tm, K), lambda j, i: (i, 0)), pl.BlockSpec((K, tn), lambda j, i: (0, j))]
    args = [a, b]
    if residual is not None:
        in_specs.append(pl.BlockSpec((tm, tn), lambda j, i: (i, j)))
        args.append(residual)
    return pl.pallas_call(
        body, name=name, grid=(N // tn, M // tm),
        out_shape=jax.ShapeDtypeStruct((M, N), out_dtype),
        in_specs=in_specs, out_specs=pl.BlockSpec((tm, tn), lambda j, i: (i, j)),
        compiler_params=_cparams(),
    )(*args)


def _matmul_tn(a, b, name):
    R, M = a.shape
    _, N = b.shape
    tr, tm, tn = _pick(R, (2048, 1024, 512)), _div128(M, 1408), _div128(N, 1408)
    while tr > 512 and 2 * (2 * tr * tm + 2 * tr * tn + 4 * tm * tn) > MATMUL_VMEM:
        tr //= 2

    def body(a_ref, b_ref, o_ref):
        @pl.when(pl.program_id(2) == 0)
        def _():
            o_ref[...] = jnp.zeros_like(o_ref)

        o_ref[...] += lax.dot_general(a_ref[...], b_ref[...], (((0,), (0,)), ((), ())),
                                      preferred_element_type=F32)

    return pl.pallas_call(
        body, name=name, grid=(M // tm, N // tn, R // tr),
        out_shape=jax.ShapeDtypeStruct((M, N), F32),
        in_specs=[pl.BlockSpec((tr, tm), lambda i, j, r: (r, i)), pl.BlockSpec((tr, tn), lambda i, j, r: (r, j))],
        out_specs=pl.BlockSpec((tm, tn), lambda i, j, r: (i, j)),
        compiler_params=_cparams(),
    )(a, b)


def _row_spec(tm, width, col):
    return pl.BlockSpec((tm, width), lambda i: (i, col))


def _whole_spec(arr):
    nd = arr.ndim
    return pl.BlockSpec(arr.shape, lambda i: (0,) * nd)


def _rowwise(name, fn, rows, params, consts, outs, n_rows, tm):
    nr, npar, nc = len(rows), len(params), len(consts)

    def body(*refs):
        vals = [r[...].astype(F32) for r in refs[:nr]] + [r[...] for r in refs[nr:nr + npar + nc]]
        res = fn(*vals)
        for o_ref, r in zip(refs[nr + npar + nc:], res):
            o_ref[...] = r.astype(o_ref.dtype)

    return pl.pallas_call(
        body, name=name, grid=(n_rows // tm,),
        out_shape=[jax.ShapeDtypeStruct((n_rows, w), dt) for w, dt in outs],
        in_specs=[_row_spec(tm, w, c) for _, w, c in rows] + [_whole_spec(p) for p in params + consts],
        out_specs=[_row_spec(tm, w, 0) for w, _ in outs],
        compiler_params=_cparams(),
    )(*[a for a, _, _ in rows], *params, *consts)


def _rowwise_bwd(name, fn, rows, params, consts, cts, row_grads, n_rows, tm):
    nr, npar, nc = len(rows), len(params), len(consts)
    ct_flat = [p for pieces in cts for p in pieces]
    res_flat = [rg[1] for rg in row_grads if rg is not None and rg[1] is not None]
    n_ct, n_res = len(ct_flat), len(res_flat)
    n_in = nr + npar + nc + n_ct + n_res
    wanted = [k for k, rg in enumerate(row_grads) if rg is not None]

    def body(*refs):
        row_vals = [r[...].astype(F32) for r in refs[:nr]]
        par_vals = [r[...] for r in refs[nr:nr + npar]]
        const_vals = [r[...] for r in refs[nr + npar:nr + npar + nc]]
        ct_refs = refs[nr + npar + nc:nr + npar + nc + n_ct]
        res_refs = refs[nr + npar + nc + n_ct:n_in]
        out_refs = refs[n_in:]
        ct_vals, pos = [], 0
        for pieces in cts:
            acc = ct_refs[pos][...].astype(F32)
            for q in range(1, len(pieces)):
                acc = acc + ct_refs[pos + q][...].astype(F32)
            pos += len(pieces)
            ct_vals.append(acc)
        _, vjp = jax.vjp(lambda *a: tuple(fn(*a, *const_vals)), *row_vals, *par_vals)
        grads = vjp(tuple(ct_vals))
        ri = 0
        for slot, k in enumerate(wanted):
            g = grads[k]
            if row_grads[k][1] is not None:
                g = g + res_refs[ri][...].astype(F32)
                ri += 1
            out_refs[slot][...] = g.astype(out_refs[slot].dtype)

        @pl.when(pl.program_id(0) == 0)
        def _():
            for q in range(npar):
                out_refs[len(wanted) + q][...] = jnp.zeros_like(out_refs[len(wanted) + q])

        for q in range(npar):
            out_refs[len(wanted) + q][...] += grads[nr + q]

    out_shape = [jax.ShapeDtypeStruct((n_rows, rows[k][1]), row_grads[k][0]) for k in wanted]
    out_shape += [jax.ShapeDtypeStruct(p.shape, F32) for p in params]
    out_specs = [_row_spec(tm, rows[k][1], 0) for k in wanted] + [_whole_spec(p) for p in params]
    in_specs = [_row_spec(tm, w, c) for _, w, c in rows] + [_whole_spec(p) for p in params + consts]
    in_specs += [_row_spec(tm, w, c) for _, w, c in ct_flat + res_flat]
    res = pl.pallas_call(
        body, name=name, grid=(n_rows // tm,),
        out_shape=out_shape, in_specs=in_specs, out_specs=out_specs,
        compiler_params=_cparams(),
    )(*[a for a, _, _ in rows], *params, *consts, *[a for a, _, _ in ct_flat + res_flat])
    return res[:len(wanted)], res[len(wanted):]


def _sigmoid(x):
    return 0.5 * jnp.tanh(0.5 * x) + 0.5


def _softplus(x):
    return jnp.maximum(x, 0.0) + jnp.log(1.0 + jnp.exp(-jnp.abs(x)))


def _seg_dot_impl(x, seg2):
    hi = x.astype(BF16)
    lo = (x - hi.astype(F32)).astype(BF16)
    return jnp.dot(jnp.concatenate([hi, lo], axis=1), seg2, preferred_element_type=F32)


@jax.custom_vjp
def _seg_dot(x, seg2):
    return _seg_dot_impl(x, seg2)


_seg_dot.defvjp(lambda x, seg2: (_seg_dot_impl(x, seg2), seg2),
                lambda seg2, ct: (_seg_dot_impl(ct, seg2), jnp.zeros_like(seg2)))


def _fn_norm(x, g):
    r = lax.rsqrt(jnp.mean(x * x, axis=-1, keepdims=True) + NORM_EPS)
    return ((x * r) * g,)


def _fn_gla_post(o_f, o_b, og, norm_g):
    o = o_f + o_b
    heads = []
    for h in range(4):
        oh = o[:, h * 128:(h + 1) * 128]
        heads.append(oh * lax.rsqrt(jnp.mean(oh * oh, axis=-1, keepdims=True) + HEAD_NORM_EPS))
    on = jnp.concatenate(heads, axis=1) * norm_g
    return (on * (og * _sigmoid(og)),)


def _fn_rwkv_pre(s_k, s_wag, w0_f, w0_b, a0, k_k, k_a, w2_f, w2_b, a2, g2, seg64):
    wa = s_wag[:, 0:128]
    gl = s_wag[:, 128:256]
    tw = jnp.tanh(wa)
    z_f = w0_f + jnp.dot(tw, w2_f, preferred_element_type=F32)
    z_b = w0_b + jnp.dot(tw, w2_b, preferred_element_type=F32)
    w_f = jnp.exp(-jnp.exp(-_softplus(-z_f) - 0.5))
    w_b = jnp.exp(-jnp.exp(-_softplus(-z_b) - 0.5))
    a = _sigmoid(a0 + jnp.dot(wa, a2, preferred_element_type=F32))
    g = jnp.dot(_sigmoid(gl), g2, preferred_element_type=F32)
    kk = s_k * k_k
    kkn = kk / jnp.maximum(jnp.sqrt(_seg_dot(kk * kk, seg64)), 1e-12)
    k2 = s_k * (1.0 + (a - 1.0) * k_a)
    return w_f, w_b, k2, -kkn, kkn * a, g


def _fn_rwkv_post(y_f, y_b, s_r, k2, s_v, g, ln_w, ln_b, r_k, seg64):
    y = y_f + y_b
    mu = _seg_dot(y, seg64) * (1.0 / 64.0)
    yc = y - mu
    var = _seg_dot(yc * yc, seg64) * (1.0 / 64.0)
    yn = yc * lax.rsqrt(var + RWKV_GN_EPS) * ln_w + ln_b
    bonus = _seg_dot(s_r * k2 * r_k, seg64) * s_v
    return ((yn + bonus) * g,)


def _fn_merge(ga, gb, y_a, y_b):
    return (_sigmoid(ga) * y_a + _sigmoid(gb) * y_b,)


def _loss_head(x2, target, gf):
    N, Dm = x2.shape
    tm = _pick(N, (512,))

    def fn(x, g, t):
        r = lax.rsqrt(jnp.mean(x * x, axis=-1, keepdims=True) + NORM_EPS)
        err = (x * r) * g - t
        return 0.5 * jnp.sum(jnp.mean(err * err, axis=-1, keepdims=True), axis=0, keepdims=True)

    def body(x_ref, t_ref, g_ref, loss_ref, dx_ref, dg_ref):
        t = t_ref[...]
        loss, vjp = jax.vjp(lambda x, g: fn(x, g, t), x_ref[...], g_ref[...])
        dx, dg = vjp(jnp.ones((1, 1), F32))

        @pl.when(pl.program_id(0) == 0)
        def _():
            loss_ref[...] = jnp.zeros_like(loss_ref)
            dg_ref[...] = jnp.zeros_like(dg_ref)

        loss_ref[...] += jnp.broadcast_to(loss, loss_ref.shape)
        dg_ref[...] += dg
        dx_ref[...] = dx

    return pl.pallas_call(
        body, name="loss_head", grid=(N // tm,),
        out_shape=[jax.ShapeDtypeStruct((8, 128), F32), jax.ShapeDtypeStruct((N, Dm), F32),
                   jax.ShapeDtypeStruct((1, Dm), F32)],
        in_specs=[_row_spec(tm, Dm, 0), _row_spec(tm, Dm, 0), _whole_spec(gf)],
        out_specs=[pl.BlockSpec((8, 128), lambda i: (0, 0)), _row_spec(tm, Dm, 0),
                   pl.BlockSpec((1, Dm), lambda i: (0, 0))],
        compiler_params=_cparams(),
    )(x2, target, gf)


def _shift_prev(u):
    rolled = pltpu.roll(u, 1, axis=0)
    row = lax.broadcasted_iota(jnp.int32, u.shape, 0)
    return jnp.where(row == 0, 0.0, rolled)


def _shift_next(u):
    T = u.shape[0]
    rolled = pltpu.roll(u, T - 1, axis=0)
    row = lax.broadcasted_iota(jnp.int32, u.shape, 0)
    return jnp.where(row == T - 1, 0.0, rolled)


_SHIFT_BLOCKS = 7


def _shift_src_col(j):
    return jnp.where(j < 6, 6 + j, 20)


def _token_shift(p, mu_prev, mu_next, B, T):
    def body(p_ref, mp_ref, mn_ref, s_ref):
        u = p_ref[...]
        s_ref[...] = u + mp_ref[...] * (_shift_prev(u) - u) + mn_ref[...] * (_shift_next(u) - u)

    return pl.pallas_call(
        body, name="token_shift", grid=(B, _SHIFT_BLOCKS),
        out_shape=jax.ShapeDtypeStruct((B * T, 1792), F32),
        in_specs=[pl.BlockSpec((T, 256), lambda b, j: (b, _shift_src_col(j))),
                  pl.BlockSpec((1, 256), lambda b, j: (0, j)), pl.BlockSpec((1, 256), lambda b, j: (0, j))],
        out_specs=pl.BlockSpec((T, 256), lambda b, j: (b, j)),
        compiler_params=_cparams(),
    )(p, mu_prev, mu_next)


def _token_shift_bwd(p, ds, mu_prev, mu_next, B, T):
    def body(p_ref, ds_ref, mp_ref, mn_ref, dp_ref, dmp_ref, dmn_ref):
        u, d = p_ref[...], ds_ref[...]
        mp, mn = mp_ref[...], mn_ref[...]
        dp = d * (1.0 - mp - mn) + _shift_next(d * mp) + _shift_prev(d * mn)
        dp_ref[...] = dp.astype(dp_ref.dtype)

        @pl.when(pl.program_id(1) == 0)
        def _():
            dmp_ref[...] = jnp.zeros_like(dmp_ref)
            dmn_ref[...] = jnp.zeros_like(dmn_ref)

        dmp_ref[...] += jnp.sum(d * (_shift_prev(u) - u), axis=0, keepdims=True)
        dmn_ref[...] += jnp.sum(d * (_shift_next(u) - u), axis=0, keepdims=True)

    return pl.pallas_call(
        body, name="token_shift_bwd", grid=(_SHIFT_BLOCKS, B),
        out_shape=[jax.ShapeDtypeStruct((B * T, 1792), BF16), jax.ShapeDtypeStruct((1, 1792), F32),
                   jax.ShapeDtypeStruct((1, 1792), F32)],
        in_specs=[pl.BlockSpec((T, 256), lambda j, b: (b, _shift_src_col(j))),
                  pl.BlockSpec((T, 256), lambda j, b: (b, j)),
                  pl.BlockSpec((1, 256), lambda j, b: (0, j)), pl.BlockSpec((1, 256), lambda j, b: (0, j))],
        out_specs=[pl.BlockSpec((T, 256), lambda j, b: (b, j)), pl.BlockSpec((1, 256), lambda j, b: (0, j)),
                   pl.BlockSpec((1, 256), lambda j, b: (0, j))],
        compiler_params=_cparams(),
    )(p, ds, mu_prev, mu_next)


_FF_BLOCKS = FF_PAD // 256


def _conv3(u, cw, cb):
    return cw[0:1] * _shift_prev(u) + cw[1:2] * u + cw[2:3] * _shift_next(u) + cb


def _ffn_conv(u, cw, cb, B, T):
    def body(ug_ref, uv_ref, cwg_ref, cwv_ref, cbg_ref, cbv_ref, o_ref):
        cg = _conv3(ug_ref[...], cwg_ref[...], cbg_ref[...])
        cv = _conv3(uv_ref[...], cwv_ref[...], cbv_ref[...])
        o_ref[...] = (cg * _sigmoid(cg) * cv).astype(o_ref.dtype)

    nb = _FF_BLOCKS
    return pl.pallas_call(
        body, name="ffn_conv", grid=(B, nb),
        out_shape=jax.ShapeDtypeStruct((B * T, FF_PAD), BF16),
        in_specs=[pl.BlockSpec((T, 256), lambda b, j: (b, j)), pl.BlockSpec((T, 256), lambda b, j: (b, j + nb)),
                  pl.BlockSpec((3, 256), lambda b, j: (0, j)), pl.BlockSpec((3, 256), lambda b, j: (0, j + nb)),
                  pl.BlockSpec((1, 256), lambda b, j: (0, j)), pl.BlockSpec((1, 256), lambda b, j: (0, j + nb))],
        out_specs=pl.BlockSpec((T, 256), lambda b, j: (b, j)),
        compiler_params=_cparams(),
    )(u, u, cw, cw, cb, cb)


def _ffn_conv_bwd(u, dact, cw, cb, B, T):
    def half(u_, dc, cw_):
        du = _shift_next(cw_[0:1] * dc) + cw_[1:2] * dc + _shift_prev(cw_[2:3] * dc)
        dcw = jnp.concatenate([jnp.sum(dc * _shift_prev(u_), axis=0, keepdims=True),
                               jnp.sum(dc * u_, axis=0, keepdims=True),
                               jnp.sum(dc * _shift_next(u_), axis=0, keepdims=True)], axis=0)
        return du, dcw, jnp.sum(dc, axis=0, keepdims=True)

    def body(ug_ref, uv_ref, da_ref, cwg_ref, cwv_ref, cbg_ref, cbv_ref,
             dug_ref, duv_ref, dcwg_ref, dcwv_ref, dcbg_ref, dcbv_ref):
        ug, uv, da = ug_ref[...], uv_ref[...], da_ref[...]
        cwg, cwv = cwg_ref[...], cwv_ref[...]
        cg = _conv3(ug, cwg, cbg_ref[...])
        cv = _conv3(uv, cwv, cbv_ref[...])
        sg = _sigmoid(cg)
        dcv = da * (cg * sg)
        dcg = da * cv * (sg * (1.0 + cg * (1.0 - sg)))
        dug, dcwg, dcbg = half(ug, dcg, cwg)
        duv, dcwv, dcbv = half(uv, dcv, cwv)
        dug_ref[...] = dug.astype(dug_ref.dtype)
        duv_ref[...] = duv.astype(duv_ref.dtype)

        @pl.when(pl.program_id(1) == 0)
        def _():
            for r in (dcwg_ref, dcwv_ref, dcbg_ref, dcbv_ref):
                r[...] = jnp.zeros_like(r)

        dcwg_ref[...] += dcwg
        dcwv_ref[...] += dcwv
        dcbg_ref[...] += dcbg
        dcbv_ref[...] += dcbv

    nb = _FF_BLOCKS
    N = B * T
    res = pl.pallas_call(
        body, name="ffn_conv_bwd", grid=(nb, B),
        out_shape=[jax.ShapeDtypeStruct((N, FF_PAD), BF16), jax.ShapeDtypeStruct((N, FF_PAD), BF16),
                   jax.ShapeDtypeStruct((3, FF_PAD), F32), jax.ShapeDtypeStruct((3, FF_PAD), F32),
                   jax.ShapeDtypeStruct((1, FF_PAD), F32), jax.ShapeDtypeStruct((1, FF_PAD), F32)],
        in_specs=[pl.BlockSpec((T, 256), lambda j, b: (b, j)), pl.BlockSpec((T, 256), lambda j, b: (b, j + nb)),
                  pl.BlockSpec((T, 256), lambda j, b: (b, j)),
                  pl.BlockSpec((3, 256), lambda j, b: (0, j)), pl.BlockSpec((3, 256), lambda j, b: (0, j + nb)),
                  pl.BlockSpec((1, 256), lambda j, b: (0, j)), pl.BlockSpec((1, 256), lambda j, b: (0, j + nb))],
        out_specs=[pl.BlockSpec((T, 256), lambda j, b: (b, j)), pl.BlockSpec((T, 256), lambda j, b: (b, j)),
                   pl.BlockSpec((3, 256), lambda j, b: (0, j)), pl.BlockSpec((3, 256), lambda j, b: (0, j)),
                   pl.BlockSpec((1, 256), lambda j, b: (0, j)), pl.BlockSpec((1, 256), lambda j, b: (0, j))],
        compiler_params=_cparams(),
    )(u, u, dact, cw, cw, cb, cb)
    dug, duv, dcwg, dcwv, dcbg, dcbv = res
    return dug, duv, jnp.concatenate([dcwg, dcwv], axis=1), jnp.concatenate([dcbg, dcbv], axis=1)


def _gla_chunk(q, k, v, afab, wa2, ba, state, rev):
    C = GLA_CHUNK
    n = q.shape[0]
    z = jnp.dot(afab.reshape(n * C, 128), wa2, preferred_element_type=F32).reshape(n, C, 256) + ba
    la = (jnp.minimum(z, 0.0) - jnp.log(1.0 + jnp.exp(-jnp.abs(z)))) * (1.0 / 16.0)
    row = lax.broadcasted_iota(jnp.int32, (n * C, n * C), 0)
    col = lax.broadcasted_iota(jnp.int32, (n * C, n * C), 1)
    ordered = ((col & (C - 1)) - (row & (C - 1))) * (1 - 2 * rev) <= 0
    tri_all = ordered & ((row >> 6) == (col >> 6))
    b = jnp.dot(tri_all.astype(F32), la.reshape(n * C, 256), precision=HIGHEST,
                preferred_element_type=F32).reshape(n, C, 256)
    tri = (lax.broadcasted_iota(jnp.int32, (C, C), 1) - lax.broadcasted_iota(jnp.int32, (C, C), 0)) * (1 - 2 * rev) <= 0
    rows = lax.broadcasted_iota(jnp.int32, (n, C, 256), 1)
    ref_row = jnp.where(rev == 0, C // 2, C - 1 - C // 2)
    last_row = jnp.where(rev == 0, C - 1, 0)
    b_ref = jnp.sum(jnp.where(rows == ref_row, b, 0.0), axis=1, keepdims=True)
    b_last = jnp.sum(jnp.where(rows == last_row, b, 0.0), axis=1, keepdims=True)
    qs = q * 0.125
    qi = qs * jnp.exp(b - b_ref)
    ki = k * jnp.exp(b_ref - b)
    kd = k * jnp.exp(b_last - b)
    qe = qs * jnp.exp(b)
    lane = lax.broadcasted_iota(jnp.int32, (1, 1, 256), 2)
    bdot = lambda x, y, cx, cy: lax.dot_general(x, y, (((cx,), (cy,)), ((0,), (0,))), preferred_element_type=F32)
    outs = []
    upd = jnp.zeros_like(state)
    for h in range(4):
        mh = ((lane >= 64 * h) & (lane < 64 * (h + 1))).astype(F32)
        vh = v[:, :, 128 * h:128 * (h + 1)]
        a = jnp.where(tri, bdot(qi * mh, ki, 2, 2), 0.0)
        outs.append(bdot(a, vh, 2, 1) + bdot(qe, state * mh, 2, 2))
        upd = upd + bdot(vh, kd * mh, 1, 1)
    return jnp.concatenate(outs, axis=2), state * jnp.exp(b_last) + upd


def _gla_chunk_at(nC):
    return lambda d, j: j + d * (nC - 1 - 2 * j)


def _gla_fwd(p, wa2, ba, B, T):
    nC = T // GLA_CHUNK
    N = B * T
    at = _gla_chunk_at(nC)
    p3 = p.reshape(B, T, p.shape[-1])

    def body(q_ref, k_ref, v_ref, af_ref, wa_ref, ba_ref, o_ref, st_ref, state):
        @pl.when(pl.program_id(1) == 0)
        def _():
            state[...] = jnp.zeros_like(state)

        st_ref[0, :, 0] = state[...]
        o, new = _gla_chunk(q_ref[...], k_ref[...], v_ref[...], af_ref[...], wa_ref[0], ba_ref[0], state[...],
                            pl.program_id(0))
        o_ref[0] = o
        state[...] = new

    blk = lambda w, col: pl.BlockSpec((B, 64, w), lambda d, j: (0, at(d, j), col))
    o, st = pl.pallas_call(
        body, name="gla_fwd", grid=(2, nC),
        out_shape=[jax.ShapeDtypeStruct((2, B, T, 512), F32), jax.ShapeDtypeStruct((2, B, nC, 128, 256), F32)],
        in_specs=[blk(256, 0), blk(256, 1), blk(512, 1), blk(128, 42),
                  pl.BlockSpec((1, 128, 256), lambda d, j: (d, 0, 0)),
                  pl.BlockSpec((1, 1, 256), lambda d, j: (d, 0, 0))],
        out_specs=[pl.BlockSpec((1, B, 64, 512), lambda d, j: (d, 0, at(d, j), 0)),
                   pl.BlockSpec((1, B, 1, 128, 256), lambda d, j: (d, 0, at(d, j), 0, 0))],
        scratch_shapes=[pltpu.VMEM((B, 128, 256), F32)],
        compiler_params=_cparams(),
    )(p3, p3, p3, p3, wa2, ba)
    return o.reshape(2, N, 512), st


def _gla_bwd(p, do, states, wa2, ba, B, T):
    nC = T // GLA_CHUNK
    N = B * T
    at_f = _gla_chunk_at(nC)
    at = lambda d, j: at_f(d, nC - 1 - j)
    p3 = p.reshape(B, T, p.shape[-1])

    def body(q_ref, k_ref, v_ref, af_ref, wa_ref, ba_ref, do_ref, st_ref,
             dqkv_ref, daf_ref, dwa_ref, dba_ref, dstate):
        rev = pl.program_id(0)

        @pl.when(pl.program_id(1) == 0)
        def _():
            dstate[...] = jnp.zeros_like(dstate)
            dwa_ref[...] = jnp.zeros_like(dwa_ref)
            dba_ref[...] = jnp.zeros_like(dba_ref)

        f = lambda q, k, v, af, wa, bb, st: _gla_chunk(q, k, v, af, wa, bb, st, rev)
        _, vjp = jax.vjp(f, q_ref[...], k_ref[...], v_ref[...], af_ref[...], wa_ref[0], ba_ref[0], st_ref[0, :, 0])
        dq, dk, dv, daf, dwa, dba, dst = vjp((do_ref[...], dstate[...]))
        dqkv_ref[0] = jnp.concatenate([dq, dk, dv], axis=2)
        daf_ref[0] = daf
        dwa_ref[0] += dwa
        dba_ref[0] += dba
        dstate[...] = dst

    blk = lambda w, col: pl.BlockSpec((B, 64, w), lambda d, j: (0, at(d, j), col))
    out4 = lambda w: pl.BlockSpec((1, B, 64, w), lambda d, j: (d, 0, at(d, j), 0))
    dqkv, daf, dwa, dba = pl.pallas_call(
        body, name="gla_bwd", grid=(2, nC),
        out_shape=[jax.ShapeDtypeStruct((2, B, T, 1024), F32), jax.ShapeDtypeStruct((2, B, T, 128), F32),
                   jax.ShapeDtypeStruct((2, 128, 256), F32), jax.ShapeDtypeStruct((2, 1, 256), F32)],
        in_specs=[blk(256, 0), blk(256, 1), blk(512, 1), blk(128, 42),
                  pl.BlockSpec((1, 128, 256), lambda d, j: (d, 0, 0)),
                  pl.BlockSpec((1, 1, 256), lambda d, j: (d, 0, 0)),
                  blk(512, 0),
                  pl.BlockSpec((1, B, 1, 128, 256), lambda d, j: (d, 0, at(d, j), 0, 0))],
        out_specs=[out4(1024), out4(128),
                   pl.BlockSpec((1, 128, 256), lambda d, j: (d, 0, 0)),
                   pl.BlockSpec((1, 1, 256), lambda d, j: (d, 0, 0))],
        scratch_shapes=[pltpu.VMEM((B, 128, 256), F32)],
        compiler_params=_cparams(),
    )(p3, p3, p3, p3, wa2, ba, do.reshape(B, T, 512), states)
    return dqkv.reshape(2, N, 1024), daf.reshape(2, N, 128), dwa, dba


def _seg_ones():
    m = lax.broadcasted_iota(jnp.int32, (256, 128), 0)
    n = lax.broadcasted_iota(jnp.int32, (256, 128), 1)
    return (((m >> 6) & 1) == (n >> 6)).astype(BF16)


def _seg_mm(x, ones2):
    hi = x.astype(BF16)
    lo = (x - hi.astype(F32)).astype(BF16)
    return jnp.dot(jnp.concatenate([hi, lo], axis=1), ones2, preferred_element_type=F32)


def _diag_matrix():
    r = np.arange(2048)[:, None] % 64
    c = np.arange(128)[None, :] % 64
    return jnp.asarray((r == c).astype(np.float32))


def _cols8(tile, dg, ones2):
    return _seg_mm(jnp.concatenate([_rows4(tile, q) for q in range(8)], axis=0) * dg[...], ones2)


def _rows4(tile, q):
    return jnp.concatenate([jnp.broadcast_to(tile[q:q + 1, 128 * p:128 * (p + 1)], (64, 128)) for p in range(4)],
                           axis=0)


def _head_rows():
    r = lax.broadcasted_iota(jnp.int32, (16, 256), 0)
    n = lax.broadcasted_iota(jnp.int32, (16, 256), 1)
    return (r == ((n >> 6) & 1)).astype(BF16)


def _head_sums_row(x, heads2):
    hi = x.astype(BF16)
    lo = (x - hi.astype(F32)).astype(BF16)
    out = lax.dot_general(heads2, jnp.concatenate([hi, lo], axis=1), (((1,), (1,)), ((), ())),
                          preferred_element_type=F32)
    return jnp.concatenate([out[0:1], out[1:2]], axis=1)


def _pair_major(y):
    n = y.shape[0]
    return y.reshape(n, 2, 4, 64).transpose(0, 2, 1, 3).reshape(n, 512)


def _colsum4(m):
    return jnp.concatenate([jnp.sum(m[64 * p:64 * (p + 1)], axis=0, keepdims=True) for p in range(4)], axis=1)


def _time_base(gi, n_groups, rev):
    return pl.multiple_of(((n_groups - 1 - gi) if rev else gi) * 8, 8)


def _scan_fwd_mxu(s, wf, wb, k2, na, bb, B, T):
    Tc = SCAN_CHUNK
    nT = T // Tc
    nG = Tc // 8
    N = B * T
    nb = _pick(B, (4, 2))
    fwd_j = lambda j: j
    bwd_j = lambda j: nT - 1 - j

    def body(rF, vF, kF, aF, bF, wF, rB, vB, kB, aB, bB, wB, dg, yF, yB, ckF, ckB,
             SF, SB, vcF, vcB, ypF, ypB, ytF, ytB):
        @pl.when(pl.program_id(1) == 0)
        def _():
            SF[...] = jnp.zeros_like(SF)
            SB[...] = jnp.zeros_like(SB)

        ckF[...] = SF[...]
        ckB[...] = SB[...]
        ones2, heads2 = _seg_ones(), _head_rows()
        chains = []
        for n in range(nb):
            chains.append((n, SF, (rF, vF, kF, aF, bF, wF), yF, vcF, ypF, ytF, False))
            chains.append((n, SB, (rB, vB, kB, aB, bB, wB), yB, vcB, ypB, ytB, True))

        def group(gi, carry):
            tiles, states = [], []
            for n, S_ref, refs, _, vc, _, _, rev in chains:
                base = _time_base(gi, nG, rev)
                t = [ref[n, pl.ds(base, 8), :] for ref in refs]
                tiles.append(t)
                states.append(S_ref[n])
                vc[n] = _cols8(t[1], dg, ones2)
            for i8 in range(8):
                for c, (n, _, _, _, vc, yp, _, rev) in enumerate(chains):
                    q = 7 - i8 if rev else i8
                    r, v, k, a, b, w = tiles[c]
                    S = states[c]
                    sa = _seg_mm(S * _rows4(a, q), ones2)
                    S = S * _rows4(w, q) + sa * _rows4(b, q) + vc[n, 256 * q:256 * (q + 1), :] * _rows4(k, q)
                    yp[n, 256 * q:256 * (q + 1), :] = S * _rows4(r, q)
                    states[c] = S
            for c, (n, S_ref, _, y_ref, _, yp, yt, rev) in enumerate(chains):
                S_ref[n] = states[c]
                for q in range(8):
                    yt[n, q:q + 1, :] = _head_sums_row(yp[n, 256 * q:256 * (q + 1), :], heads2)
                y_ref[n, pl.ds(_time_base(gi, nG, rev), 8), :] = yt[n]
            return carry

        lax.fori_loop(0, nG, group, 0)

    row_in = lambda at, col: pl.BlockSpec((nb, Tc, 512), lambda g, j: (g, at(j), col))
    state_io = lambda at: pl.BlockSpec((nb, 256, 128), lambda g, j: (g, at(j), 0))
    in_specs = []
    for at in (fwd_j, bwd_j):
        in_specs += [row_in(at, 0), row_in(at, 2)] + [row_in(at, 0)] * 4
    big = pltpu.VMEM((nb, 8 * 256, 128), F32)
    s3 = s.reshape(B, T, s.shape[-1])
    seq = lambda a: a.reshape(B, T, 512)
    y_f, y_b, ck_f, ck_b = pl.pallas_call(
        body, name="rwkv_scan", grid=(B // nb, nT),
        out_shape=[jax.ShapeDtypeStruct((B, T, 512), F32), jax.ShapeDtypeStruct((B, T, 512), F32),
                   jax.ShapeDtypeStruct((B, nT * 256, 128), F32), jax.ShapeDtypeStruct((B, nT * 256, 128), F32)],
        in_specs=in_specs + [pl.BlockSpec((2048, 128), lambda g, j: (0, 0))],
        out_specs=[row_in(fwd_j, 0), row_in(bwd_j, 0), state_io(fwd_j), state_io(bwd_j)],
        scratch_shapes=[pltpu.VMEM((nb, 256, 128), F32), pltpu.VMEM((nb, 256, 128), F32), big, big, big, big,
                        pltpu.VMEM((nb, 8, 512), F32), pltpu.VMEM((nb, 8, 512), F32)],
        compiler_params=_cparams(),
    )(s3, s3, seq(k2), seq(na), seq(bb), seq(wf), s3, s3, seq(k2), seq(na), seq(bb), seq(wb), _diag_matrix())
    ck_shape = (B * nT * 256, 128)
    return (_pair_major(y_f.reshape(N, 512)), _pair_major(y_b.reshape(N, 512)),
            ck_f.reshape(ck_shape), ck_b.reshape(ck_shape))


def _scan_bwd_mxu(s, wf, wb, k2, na, bb, dy, ckF, ckB, B, T):
    Tc = SCAN_CHUNK
    nT = T // Tc
    nG = Tc // 8
    N = B * T
    nb = _pick(B, (2,))
    f_at = lambda j: nT - 1 - j
    b_at = lambda j: j
    n_in, n_out, n_scr = 17, 12, 12

    def body(*refs):
        (rF, vF, kF, aF, bF, wF, dyF, ckF_ref, rB, vB, kB, aB, bB, wB, dyB, ckB_ref, dg) = refs[:n_in]
        outsF, outsB = refs[n_in:n_in + 6], refs[n_in + 6:n_in + n_out]
        chains = []
        for n in range(nb):
            stF, stB, saF, saB, vcF, vcB, dSF, dSB, bigF, bigB, tileF, tileB = \
                refs[n_in + n_out + n_scr * n:n_in + n_out + n_scr * (n + 1)]
            chains.append((n, stF, dSF, ckF_ref, (rF, vF, kF, aF, bF, wF, dyF), outsF, bigF, tileF, False, saF, vcF))
            chains.append((n, stB, dSB, ckB_ref, (rB, vB, kB, aB, bB, wB, dyB), outsB, bigB, tileB, True, saB, vcB))

        @pl.when(pl.program_id(1) == 0)
        def _():
            for chain in chains:
                chain[2][...] = jnp.zeros_like(chain[2])

        ones2, heads2 = _seg_ones(), _head_rows()
        for chain in chains:
            chain[1][0] = chain[3][chain[0]]

        cols8 = lambda tile: _cols8(tile, dg, ones2)

        def recompute(gi, carry):
            tiles, states = [], []
            for n, st, _, _, ins, _, big, _, rev, _, vc_keep in chains:
                base = _time_base(gi, nG, rev)
                t = [ref[n, pl.ds(base, 8), :] for ref in ins[1:6]]
                tiles.append(t)
                states.append(st[gi * 8])
                v_cols = cols8(t[0])
                for i8 in range(8):
                    q = 7 - i8 if rev else i8
                    vc_keep[gi * 8 + i8] = v_cols[256 * q:256 * (q + 1)]
            for i8 in range(8):
                for c, (_, st, _, _, _, _, _, _, rev, sa_keep, vc_keep) in enumerate(chains):
                    q = 7 - i8 if rev else i8
                    v, k, a, b, w = tiles[c]
                    S = states[c]
                    sa = _seg_mm(S * _rows4(a, q), ones2)
                    sa_keep[gi * 8 + i8] = sa
                    S = S * _rows4(w, q) + sa * _rows4(b, q) + vc_keep[gi * 8 + i8] * _rows4(k, q)
                    st[gi * 8 + i8 + 1] = S
                    states[c] = S
            return carry

        lax.fori_loop(0, nG, recompute, 0)

        def back(gg, carry):
            gi = nG - 1 - gg
            tiles, grads = [], []
            for n, st, dS_ref, _, ins, _, big, _, rev, _, _ in chains:
                base = _time_base(gi, nG, rev)
                t = [ref[n, pl.ds(base, 8), :] for ref in ins]
                tiles.append(t)
                grads.append(dS_ref[...])
                big[0] = cols8(t[6])
            for i8 in range(7, -1, -1):
                for c, (_, st, _, _, _, _, big, tile, rev, sa_keep, vc_keep) in enumerate(chains):
                    q = 7 - i8 if rev else i8
                    r, v, k, a, b, w, _ = tiles[c]
                    i = gi * 8 + i8
                    S_prev, S_t = st[i], st[i + 1]
                    rows = slice(256 * q, 256 * (q + 1))
                    dy_col, v_col, sa = big[0, rows, :], vc_keep[i], sa_keep[i]
                    dS = grads[c] + dy_col * _rows4(r, q)
                    sb = _seg_mm(dS * _rows4(b, q), ones2)
                    big[1, rows, :] = dS * _rows4(k, q)
                    tile[0, q:q + 1, :] = _colsum4(S_t * dy_col)
                    tile[2, q:q + 1, :] = _colsum4(dS * v_col)
                    tile[3, q:q + 1, :] = _colsum4(S_prev * sb)
                    tile[4, q:q + 1, :] = _colsum4(dS * sa)
                    tile[5, q:q + 1, :] = _colsum4(S_prev * dS)
                    grads[c] = dS * _rows4(w, q) + sb * _rows4(a, q)
            for c, (n, _, dS_ref, _, _, outs, big, tile, rev, _, _) in enumerate(chains):
                dS_ref[...] = grads[c]
                for q in range(8):
                    tile[1, q:q + 1, :] = _head_sums_row(big[1, 256 * q:256 * (q + 1), :], heads2)
                base = _time_base(gi, nG, rev)
                for o, o_ref in enumerate(outs):
                    o_ref[n, pl.ds(base, 8), :] = tile[o]
            return carry

        lax.fori_loop(0, nG, back, 0)

    row_io = lambda at, col: pl.BlockSpec((nb, Tc, 512), lambda g, j: (g, at(j), col))
    in_specs = []
    for at in (f_at, b_at):
        in_specs += [row_io(at, 0), row_io(at, 2)] + [row_io(at, 0)] * 5
        in_specs.append(pl.BlockSpec((nb, 256, 128), lambda g, j, at=at: (g, at(j), 0)))
    in_specs.append(pl.BlockSpec((2048, 128), lambda g, j: (0, 0)))
    out_specs = [row_io(f_at, 0)] * 6 + [row_io(b_at, 0)] * 6
    big = pltpu.VMEM((2, 8 * 256, 128), F32)
    states = pltpu.VMEM((Tc + 1, 256, 128), F32)
    per_step = pltpu.VMEM((Tc, 256, 128), F32)
    one_slot = [states, states, per_step, per_step, per_step, per_step,
                pltpu.VMEM((256, 128), F32), pltpu.VMEM((256, 128), F32), big, big,
                pltpu.VMEM((6, 8, 512), F32), pltpu.VMEM((6, 8, 512), F32)]
    s3 = s.reshape(B, T, s.shape[-1])
    seq = lambda a: a.reshape(B, T, 512)
    ck3 = lambda a: a.reshape(B, nT * 256, 128)
    outs = pl.pallas_call(
        body, name="rwkv_scan_bwd", grid=(B // nb, nT),
        out_shape=[jax.ShapeDtypeStruct((B, T, 512), F32)] * 12,
        in_specs=in_specs, out_specs=out_specs,
        scratch_shapes=one_slot * nb,
        compiler_params=_cparams(),
    )(s3, s3, seq(k2), seq(na), seq(bb), seq(wf), seq(dy), ck3(ckF),
      s3, s3, seq(k2), seq(na), seq(bb), seq(wb), seq(dy), ck3(ckB), _diag_matrix())
    outs = [o.reshape(N, 512) for o in outs]
    outs[1], outs[7] = _pair_major(outs[1]), _pair_major(outs[7])
    return outs


def _cat_shards(g4, name, axis):
    return jnp.concatenate([g4[s][name] for s in range(4)], axis=axis)


def _split_shards(full, axis):
    return jnp.split(full, 4, axis=axis)


def kernel(x, norm1_g, w_in, gla_wa2_f, gla_ba_f, gla_wa2_b, gla_ba_b, gla_norm_g, gla_proj, rwkv_mu_prev, rwkv_mu_next, rwkv_w0_f, rwkv_w2_f, rwkv_w0_b, rwkv_w2_b, rwkv_a0, rwkv_a2, rwkv_g2, rwkv_k_k, rwkv_k_a, rwkv_r_k, rwkv_ln_w, rwkv_ln_b, rwkv_proj, w_out, norm2_g, ffn_up, ffn_conv_w, ffn_conv_b, ffn_down, norm_f_g, loss_target, m_norm1_g, m_w_in, m_gla_wa2_f, m_gla_ba_f, m_gla_wa2_b, m_gla_ba_b, m_gla_norm_g, m_gla_proj, m_rwkv_mu_prev, m_rwkv_mu_next, m_rwkv_w0_f, m_rwkv_w2_f, m_rwkv_w0_b, m_rwkv_w2_b, m_rwkv_a0, m_rwkv_a2, m_rwkv_g2, m_rwkv_k_k, m_rwkv_k_a, m_rwkv_r_k, m_rwkv_ln_w, m_rwkv_ln_b, m_rwkv_proj, m_w_out, m_norm2_g, m_ffn_up, m_ffn_conv_w, m_ffn_conv_b, m_ffn_down, m_norm_f_g, v_norm1_g, v_w_in, v_gla_wa2_f, v_gla_ba_f, v_gla_wa2_b, v_gla_ba_b, v_gla_norm_g, v_gla_proj, v_rwkv_mu_prev, v_rwkv_mu_next, v_rwkv_w0_f, v_rwkv_w2_f, v_rwkv_w0_b, v_rwkv_w2_b, v_rwkv_a0, v_rwkv_a2, v_rwkv_g2, v_rwkv_k_k, v_rwkv_k_a, v_rwkv_r_k, v_rwkv_ln_w, v_rwkv_ln_b, v_rwkv_proj, v_w_out, v_norm2_g, v_ffn_up, v_ffn_conv_w, v_ffn_conv_b, v_ffn_down, v_norm_f_g):
    args = locals()
    weights = {n: args[n] for n in WEIGHT_ORDER}
    mom_m = {n: args["m_" + n] for n in WEIGHT_ORDER}
    mom_v = {n: args["v_" + n] for n in WEIGHT_ORDER}
    shapes = {n: weights[n].shape for n in WEIGHT_ORDER}
    B, T, _ = x.shape
    N = B * T
    tm = _pick(N, (512,))

    def local(d):
        sh = {n: d[n].reshape(s) for n, s, _ in SHARDED}
        rp = {n: d[n].reshape(-1) for n, _ in REPLICATED}
        return sh, rp

    w_loc, m_loc, v_loc = local(weights), local(mom_m), local(mom_v)

    gathered = _exchange_chips([w_loc[0][n].astype(BF16) for n, _, _ in BIG] + [_pack(*w_loc)],
                               "allgather_weights", gather=True)
    small_vals = [_unpack(gathered[-1][s]) for s in range(4)]
    W = {n: jnp.concatenate([gathered[i][s] for s in range(4)], axis=ax) for i, (n, _, ax) in enumerate(BIG)}
    W.update({n: _cat_shards(small_vals, n, ax) for n, _, ax in SMALL})
    R = {n: weights[n].reshape(1, -1) for n, _ in REPLICATED}

    zc = lambda r, c, dt=F32: jnp.zeros((r, c), dt)
    w_in_full = W["w_in"]
    w_in_p = jnp.concatenate([w_in_full[:, 0:1536], w_in_full[:, 1568:3104], w_in_full[:, 3360:5408],
                              w_in_full[:, 3104:3360], w_in_full[:, 1536:1568],
                              zc(1024, PROJ_PAD - N_PROJ, BF16)], axis=1)
    w_in_b = w_in_p
    pad_ff = lambda a: jnp.concatenate([a[:, :D_FF], zc(a.shape[0], FF_PAD - D_FF, a.dtype), a[:, D_FF:],
                                        zc(a.shape[0], FF_PAD - D_FF, a.dtype)], axis=1)
    ffn_up_p = pad_ff(W["ffn_up"])
    ffn_up_b = ffn_up_p
    conv_w_p = pad_ff(W["ffn_conv_w"])
    conv_b_p = pad_ff(R["ffn_conv_b"])
    ffn_down_p = jnp.concatenate([W["ffn_down"], zc(FF_PAD - D_FF, 1024, BF16)], axis=0)
    ffn_down_b = ffn_down_p
    w_out_b = W["w_out"]
    gla_proj_b = W["gla_proj"]
    rwkv_proj_b = W["rwkv_proj"]
    wa2 = jnp.stack([jnp.concatenate([W["gla_wa2_f"], zc(112, 256)], axis=0),
                     jnp.concatenate([zc(16, 256), W["gla_wa2_b"], zc(96, 256)], axis=0)])
    ba = jnp.stack([R["gla_ba_f"], R["gla_ba_b"]])
    w2_f = jnp.concatenate([W["rwkv_w2_f"], zc(64, 512)], axis=0)
    w2_b = jnp.concatenate([W["rwkv_w2_b"], zc(64, 512)], axis=0)
    a2 = jnp.concatenate([zc(64, 512), W["rwkv_a2"]], axis=0)
    g2 = W["rwkv_g2"]
    head_ones = np.kron(np.eye(8, dtype=np.float32), np.ones((64, 64), np.float32))
    seg64 = jnp.asarray(np.concatenate([head_ones, head_ones], axis=0), dtype=BF16)

    x2d = x.reshape(N, D_MODEL)
    tgt = loss_target.reshape(N, D_MODEL)

    (h1,) = _rowwise("norm1", _fn_norm, [(x2d, 1024, 0)], [R["norm1_g"]], [], [(1024, BF16)], N, tm)
    p = _matmul(h1, w_in_b, "proj_in")
    o_gla, gla_states = _gla_fwd(p, wa2, ba, B, T)
    gla_post_rows = [(o_gla[0], 512, 0), (o_gla[1], 512, 0), (p, 512, 2)]
    (gated,) = _rowwise("gla_post", _fn_gla_post, gla_post_rows, [R["gla_norm_g"]], [], [(512, BF16)], N, tm)
    y_a = _matmul(gated, gla_proj_b, "gla_out")
    s = _token_shift(p, R["rwkv_mu_prev"], R["rwkv_mu_next"], B, T)
    pre_rows = [(s, 512, 1), (s, 256, 6)]
    pre_params = [R["rwkv_w0_f"], R["rwkv_w0_b"], R["rwkv_a0"], R["rwkv_k_k"], R["rwkv_k_a"], w2_f, w2_b, a2, g2]
    wf, wb, k2, na, bb, g = _rowwise("rwkv_pre", _fn_rwkv_pre, pre_rows, pre_params, [seg64],
                                     [(512, F32)] * 6, N, tm)
    y_f, y_b, ck_f, ck_b = _scan_fwd_mxu(s, wf, wb, k2, na, bb, B, T)
    post_rows = [(y_f, 512, 0), (y_b, 512, 0), (s, 512, 0), (k2, 512, 0), (s, 512, 2), (g, 512, 0)]
    post_params = [R["rwkv_ln_w"], R["rwkv_ln_b"], R["rwkv_r_k"]]
    (o_rwkv,) = _rowwise("rwkv_post", _fn_rwkv_post, post_rows, post_params, [seg64], [(512, BF16)], N, tm)
    y_r = _matmul(o_rwkv, rwkv_proj_b, "rwkv_out")
    merge_rows = [(p, 1024, 3), (p, 1024, 4), (y_a, 1024, 0), (y_r, 1024, 0)]
    (merged,) = _rowwise("merge", _fn_merge, merge_rows, [], [], [(1024, BF16)], N, tm)
    x1 = _matmul(merged, w_out_b, "mix_out", residual=x2d)
    (h2,) = _rowwise("norm2", _fn_norm, [(x1, 1024, 0)], [R["norm2_g"]], [], [(1024, BF16)], N, tm)
    u = _matmul(h2, ffn_up_b, "ffn_up")
    act = _ffn_conv(u, conv_w_p, conv_b_p, B, T)
    x2 = _matmul(act, ffn_down_b, "ffn_down", residual=x1)
    loss_blk, dx2, d_norm_f = _loss_head(x2, tgt, weights["norm_f_g"].reshape(1, -1))

    dx2_b = dx2.astype(BF16)
    d_act = _matmul(dx2_b, ffn_down_p.T.astype(BF16), "d_act")
    d_ffn_down = _matmul_tn(act, dx2_b, "dw_ffn_down")[:D_FF]
    du_g, du_v, d_conv_w_p, d_conv_b_p = _ffn_conv_bwd(u, d_act, conv_w_p, conv_b_p, B, T)
    up_t = ffn_up_p.T.astype(BF16)
    d_h2 = _matmul(du_v, up_t[FF_PAD:], "d_h2_v", residual=_matmul(du_g, up_t[:FF_PAD], "d_h2_g"))
    d_ffn_up = jnp.concatenate([_matmul_tn(h2, du_g, "dw_ffn_up_g")[:, :D_FF],
                                _matmul_tn(h2, du_v, "dw_ffn_up_v")[:, :D_FF]], axis=1)
    unpad_ff = lambda a: jnp.concatenate([a[:, :D_FF], a[:, FF_PAD:FF_PAD + D_FF]], axis=1)
    (dx1,), (d_norm2,) = _rowwise_bwd("norm2_bwd", _fn_norm, [(x1, 1024, 0)], [R["norm2_g"]], [],
                                      [[(d_h2, 1024, 0)]], [(F32, (dx2, 1024, 0))], N, tm)
    dx1_b = dx1.astype(BF16)
    d_merged = _matmul(dx1_b, W["w_out"].T.astype(BF16), "d_merged")
    d_w_out = _matmul_tn(merged, dx1_b, "dw_out")
    (d_ga, d_gb, d_ya, d_yr), _ = _rowwise_bwd("merge_bwd", _fn_merge, merge_rows, [], [],
                                               [[(d_merged, 1024, 0)]], [(BF16, None)] * 4, N, tm)
    d_o_rwkv = _matmul(d_yr, W["rwkv_proj"].T.astype(BF16), "d_o_rwkv")
    d_rwkv_proj = _matmul_tn(o_rwkv, d_yr, "dw_rwkv_proj")
    (d_y, d_r_bonus, d_k2_bonus, d_v_bonus, d_g), (d_ln_w, d_ln_b, d_r_k) = _rowwise_bwd(
        "rwkv_post_bwd", _fn_rwkv_post, post_rows, post_params, [seg64], [[(d_o_rwkv, 512, 0)]],
        [(F32, None), None, (F32, None), (F32, None), (F32, None), (F32, None)], N, tm)
    (drF, dvF, dkF, daF, dbF, dwF, drB, dvB, dkB, daB, dbB, dwB) = _scan_bwd_mxu(s, wf, wb, k2, na, bb, d_y, ck_f, ck_b, B, T)
    pre_cts = [[(dwF, 512, 0)], [(dwB, 512, 0)], [(dkF, 512, 0), (dkB, 512, 0), (d_k2_bonus, 512, 0)],
               [(daF, 512, 0), (daB, 512, 0)], [(dbF, 512, 0), (dbB, 512, 0)], [(d_g, 512, 0)]]
    (ds_k, ds_wag), pre_grads = _rowwise_bwd("rwkv_pre_bwd", _fn_rwkv_pre, pre_rows, pre_params, [seg64], pre_cts,
                                             [(F32, None), (F32, None)], N, tm)
    d_w0_f, d_w0_b, d_a0, d_k_k, d_k_a, d_w2_f, d_w2_b, d_a2, d_g2 = pre_grads
    ds = jnp.concatenate([drF + drB + d_r_bonus, ds_k, dvF + dvB + d_v_bonus, ds_wag], axis=1)
    d_p_rwkv, d_mu_prev, d_mu_next = _token_shift_bwd(p, ds, R["rwkv_mu_prev"], R["rwkv_mu_next"], B, T)
    d_gated = _matmul(d_ya, W["gla_proj"].T.astype(BF16), "d_gated")
    d_gla_proj = _matmul_tn(gated, d_ya, "dw_gla_proj")
    (d_o, d_og), (d_gla_norm,) = _rowwise_bwd(
        "gla_post_bwd", _fn_gla_post, gla_post_rows, [R["gla_norm_g"]], [], [[(d_gated, 512, 0)]],
        [(F32, None), None, (BF16, None)], N, tm)
    dqkv2, dafab2, d_wa2, d_ba = _gla_bwd(p, d_o, gla_states, wa2, ba, B, T)
    add2 = lambda a, b: (a + b,)
    (d_qkv,) = _rowwise("sum_dqkv", add2, [(dqkv2[0], 1024, 0), (dqkv2[1], 1024, 0)], [], [], [(1024, BF16)], N, tm)
    (d_afab,) = _rowwise("sum_dafab", add2, [(dafab2[0], 128, 0), (dafab2[1], 128, 0)], [], [], [(128, BF16)], N, tm)
    w_in_t = w_in_p.T.astype(BF16)
    w_rwkv_t = jnp.concatenate([w_in_t[1536:3072], w_in_t[5120:5376]], axis=0)
    d_h1 = _matmul(d_qkv, w_in_t[0:1024], "d_h1_qkv")
    d_h1 = _matmul(d_og, w_in_t[1024:1536], "d_h1_og", residual=d_h1)
    d_h1 = _matmul(d_p_rwkv, w_rwkv_t, "d_h1_rwkv", residual=d_h1)
    d_h1 = _matmul(d_ga, w_in_t[3072:4096], "d_h1_ga", residual=d_h1)
    d_h1 = _matmul(d_gb, w_in_t[4096:5120], "d_h1_gb", residual=d_h1)
    d_h1 = _matmul(d_afab, w_in_t[5376:5504], "d_h1_afab", residual=d_h1)
    d_w_in = jnp.concatenate([
        _matmul_tn(h1, d_qkv, "dw_in_qkv"), _matmul_tn(h1, d_og, "dw_in_og"),
        _matmul_tn(h1, d_afab, "dw_in_afab")[:, :32], _matmul_tn(h1, d_p_rwkv, "dw_in_rwkv"),
        _matmul_tn(h1, d_ga, "dw_in_ga"), _matmul_tn(h1, d_gb, "dw_in_gb")], axis=1)
    (grad_x,), (d_norm1,) = _rowwise_bwd("norm1_bwd", _fn_norm, [(x2d, 1024, 0)], [R["norm1_g"]], [],
                                         [[(d_h1, 1024, 0)]], [(F32, (dx1, 1024, 0))], N, tm)

    full_grads = {
        "w_in": d_w_in, "gla_wa2_f": d_wa2[0, 0:16], "gla_wa2_b": d_wa2[1, 16:32], "gla_proj": d_gla_proj,
        "rwkv_w2_f": d_w2_f[0:64], "rwkv_w2_b": d_w2_b[0:64], "rwkv_a2": d_a2[64:128], "rwkv_g2": d_g2,
        "rwkv_proj": d_rwkv_proj, "w_out": d_w_out, "ffn_up": d_ffn_up, "ffn_conv_w": unpad_ff(d_conv_w_p),
        "ffn_down": d_ffn_down,
    }
    repl_grads = {
        "norm1_g": d_norm1, "gla_ba_f": d_ba[0], "gla_ba_b": d_ba[1], "gla_norm_g": d_gla_norm,
        "rwkv_mu_prev": d_mu_prev, "rwkv_mu_next": d_mu_next, "rwkv_w0_f": d_w0_f, "rwkv_w0_b": d_w0_b,
        "rwkv_a0": d_a0, "rwkv_k_k": d_k_k, "rwkv_k_a": d_k_a, "rwkv_r_k": d_r_k, "rwkv_ln_w": d_ln_w,
        "rwkv_ln_b": d_ln_b, "norm2_g": d_norm2, "ffn_conv_b": unpad_ff(d_conv_b_p), "norm_f_g": d_norm_f,
    }
    split = {n: _split_shards(full_grads[n], ax) for n, _, ax in SHARDED}
    to_owners = [jnp.stack([p.astype(BF16) for p in split[n]]) for n, _, _ in BIG]
    to_owners.append(jnp.stack([_pack({n: split[n][sidx] for n, _, _ in SMALL}, repl_grads, loss_blk[0, 0])
                                for sidx in range(4)]))
    received = _exchange_chips(to_owners, "scatter_grads", gather=False)
    names = [n for n, _, _ in BIG] + ["small"]
    mine = [_sum_sources(r, "sum_" + n) for r, n in zip(received, names)]
    other = _swap_with_sibling(mine)
    packs = [_pack(*w_loc), _pack(*m_loc), _pack(*v_loc)]
    results = {}
    for i, n in enumerate(names):
        wmv = packs if n == "small" else [d[0][n] for d in (w_loc, m_loc, v_loc)]
        results[n] = _adamw(mine[i], other[i], *wmv, "adamw_" + n)
    small = [_unpack(f) for f in results["small"]]
    outs = [small[0]["loss"], grad_x.reshape(B, T, D_MODEL)]
    for kind in range(4):
        for n in WEIGHT_ORDER:
            val = results[n][kind] if n in results else small[kind][n]
            outs.append(val.reshape(shapes[n]))
    return tuple(outs)
```

```python
import functools

import jax
import jax.numpy as jnp
import numpy as np
from jax import lax
from jax.experimental import pallas as pl
from jax.experimental.pallas import tpu as pltpu

F32 = jnp.float32
BF16 = jnp.bfloat16
HIGHEST = lax.Precision.HIGHEST
MESH_IDS = pl.DeviceIdType.MESH

D_MODEL = 1024
N_PROJ = 5408
PROJ_PAD = 5632
D_FF = 2752
FF_PAD = 2816
GLA_CHUNK = 64
SCAN_CHUNK = 16
NORM_EPS = 1e-6
HEAD_NORM_EPS = 1e-5
RWKV_GN_EPS = 64 * 1e-5
ADAM_LR, ADAM_B1, ADAM_B2, ADAM_EPS, ADAM_WD, ADAM_STEP = 0.001, 0.9, 0.999, 1e-08, 0.01, 10
VMEM_LIMIT = 56 * 1024 * 1024

FLAT_ROWS, FLAT_COLS = 128, 1024
BIG = (
    ("w_in", (1024, 1352), 1), ("gla_proj", (512, 256), 1), ("rwkv_proj", (512, 256), 1),
    ("w_out", (256, 1024), 0), ("ffn_up", (1024, 1376), 1), ("ffn_down", (688, 1024), 0),
)
SMALL = (
    ("gla_wa2_f", (16, 64), 1), ("gla_wa2_b", (16, 64), 1), ("rwkv_w2_f", (64, 128), 1),
    ("rwkv_w2_b", (64, 128), 1), ("rwkv_a2", (64, 128), 1), ("rwkv_g2", (128, 128), 1),
    ("ffn_conv_w", (3, 1376), 1),
)
SHARDED = BIG + SMALL
REPLICATED = (
    ("norm1_g", 1024), ("gla_ba_f", 256), ("gla_ba_b", 256), ("gla_norm_g", 512),
    ("rwkv_mu_prev", 1792), ("rwkv_mu_next", 1792), ("rwkv_w0_f", 512), ("rwkv_w0_b", 512),
    ("rwkv_a0", 512), ("rwkv_k_k", 512), ("rwkv_k_a", 512), ("rwkv_r_k", 512),
    ("rwkv_ln_w", 512), ("rwkv_ln_b", 512), ("norm2_g", 1024), ("ffn_conv_b", 5504),
    ("norm_f_g", 1024),
)
WEIGHT_ORDER = ("norm1_g", "w_in", "gla_wa2_f", "gla_ba_f", "gla_wa2_b", "gla_ba_b", "gla_norm_g", "gla_proj",
                "rwkv_mu_prev", "rwkv_mu_next", "rwkv_w0_f", "rwkv_w2_f", "rwkv_w0_b", "rwkv_w2_b", "rwkv_a0",
                "rwkv_a2", "rwkv_g2", "rwkv_k_k", "rwkv_k_a", "rwkv_r_k", "rwkv_ln_w", "rwkv_ln_b", "rwkv_proj",
                "w_out", "norm2_g", "ffn_up", "ffn_conv_w", "ffn_conv_b", "ffn_down", "norm_f_g")


def _cparams(**kw):
    return pltpu.CompilerParams(vmem_limit_bytes=VMEM_LIMIT, **kw)


def _pack_replicated(repl_vals, loss=None):
    parts = [repl_vals[n].reshape(-1) for n, _ in REPLICATED]
    parts.append(jnp.zeros((1,), F32) if loss is None else loss.reshape(1))
    return jnp.concatenate(parts)


def _pack(sharded_vals, repl_flat):
    parts = [sharded_vals[n].reshape(-1) for n, _, _ in SMALL] + [repl_flat]
    used = sum(int(np.prod(s)) for _, s, _ in SMALL) + sum(w for _, w in REPLICATED) + 1
    parts.append(jnp.zeros((FLAT_ROWS * FLAT_COLS - used,), F32))
    return jnp.concatenate(parts).reshape(FLAT_ROWS, FLAT_COLS)


def _unpack(flat):
    v = flat.reshape(-1)
    out, off = {}, 0
    for n, s, _ in SMALL:
        k = int(np.prod(s))
        out[n] = v[off:off + k].reshape(s)
        off += k
    for n, w in REPLICATED:
        out[n] = v[off:off + w]
        off += w
    out["loss"] = v[off]
    return out


def _chip_peers():
    x, y, c = lax.axis_index("x"), lax.axis_index("y"), lax.axis_index("c")
    return x, y, c, ((1 - x, y), (x, 1 - y), (1 - x, 1 - y))


def _exchange_chips(arrs, name, gather):
    n = len(arrs)

    def body(*refs):
        srcs, outs = refs[:n], refs[n:2 * n]
        send_sems, recv_sems, local_sems = refs[2 * n:]
        x, y, c, peers = _chip_peers()
        me = 2 * x + y
        own = []
        for i in range(n):
            cp = pltpu.make_async_copy(srcs[i] if gather else srcs[i].at[me], outs[i].at[me], local_sems.at[i])
            cp.start()
            own.append(cp)
        sends = []
        for k, (px, py) in enumerate(peers):
            for i in range(n):
                cp = pltpu.make_async_remote_copy(
                    src_ref=srcs[i] if gather else srcs[i].at[2 * px + py], dst_ref=outs[i].at[me],
                    send_sem=send_sems.at[3 * i + k], recv_sem=recv_sems.at[3 * i + k],
                    device_id=(px, py, c), device_id_type=MESH_IDS)
                cp.start()
                sends.append(cp)
        for k, (px, py) in enumerate(peers):
            for i in range(n):
                pltpu.make_async_remote_copy(
                    src_ref=srcs[i] if gather else srcs[i].at[me], dst_ref=outs[i].at[2 * px + py],
                    send_sem=send_sems.at[3 * i + k], recv_sem=recv_sems.at[3 * i + k],
                    device_id=(px, py, c), device_id_type=MESH_IDS).wait_recv()
        for cp in sends:
            cp.wait_send()
        for cp in own:
            cp.wait()

    out_shape = [jax.ShapeDtypeStruct(((4,) + a.shape) if gather else a.shape, a.dtype) for a in arrs]
    return pl.pallas_call(
        body, name=name, out_shape=out_shape,
        in_specs=[pl.BlockSpec(memory_space=pl.ANY)] * n,
        out_specs=[pl.BlockSpec(memory_space=pl.ANY)] * n,
        scratch_shapes=[pltpu.SemaphoreType.DMA((3 * n,)), pltpu.SemaphoreType.DMA((3 * n,)),
                        pltpu.SemaphoreType.DMA((n,))],
    )(*arrs)


def _swap_with_sibling(arrs):
    n = len(arrs)

    def body(*refs):
        srcs, outs = refs[:n], refs[n:2 * n]
        send_sems, recv_sems = refs[2 * n:]
        x, y, c = lax.axis_index("x"), lax.axis_index("y"), lax.axis_index("c")
        cps = [pltpu.make_async_remote_copy(src_ref=srcs[i], dst_ref=outs[i], send_sem=send_sems.at[i],
                                            recv_sem=recv_sems.at[i], device_id=(x, y, 1 - c),
                                            device_id_type=MESH_IDS) for i in range(n)]
        for cp in cps:
            cp.start()
        for cp in cps:
            cp.wait()

    return pl.pallas_call(
        body, name="swap_sibling",
        out_shape=[jax.ShapeDtypeStruct(a.shape, a.dtype) for a in arrs],
        in_specs=[pl.BlockSpec(memory_space=pl.ANY)] * n,
        out_specs=[pl.BlockSpec(memory_space=pl.ANY)] * n,
        scratch_shapes=[pltpu.SemaphoreType.DMA((n,)), pltpu.SemaphoreType.DMA((n,))],
    )(*arrs)


def _row_tile(rows, cols):
    cap = max(8, (3 << 19) // (4 * (-(-cols // 128) * 128)))
    best = None
    for t in range(8, min(rows, cap) + 1, 8):
        if rows % t == 0:
            best = t
    return best or rows


def _sum_sources(r4, name):
    _, A, Bc = r4.shape
    ta = _row_tile(A, Bc)

    def body(r_ref, o_ref):
        f = lambda s: r_ref[s].astype(F32)
        o_ref[...] = ((f(0) + f(1)) + f(2)) + f(3)

    return pl.pallas_call(
        body, name=name, grid=(A // ta,),
        out_shape=jax.ShapeDtypeStruct((A, Bc), F32),
        in_specs=[pl.BlockSpec((4, ta, Bc), lambda i: (0, i, 0))],
        out_specs=pl.BlockSpec((ta, Bc), lambda i: (i, 0)),
        compiler_params=_cparams(),
    )(r4)


def _adamw(own, other, w, m, v, name):
    R, C = own.shape
    tr = _row_tile(R, C)

    def body(a_ref, b_ref, w_ref, m_ref, v_ref, g_out, d_out, m_out, v_out):
        g = a_ref[...] + b_ref[...]
        m_new = ADAM_B1 * m_ref[...] + (1.0 - ADAM_B1) * g
        v_new = ADAM_B2 * v_ref[...] + (1.0 - ADAM_B2) * (g * g)
        m_hat = m_new / (1.0 - ADAM_B1 ** ADAM_STEP)
        v_hat = v_new / (1.0 - ADAM_B2 ** ADAM_STEP)
        g_out[...] = g
        d_out[...] = -ADAM_LR * (m_hat / (jnp.sqrt(v_hat) + ADAM_EPS) + ADAM_WD * w_ref[...])
        m_out[...] = m_new
        v_out[...] = v_new

    spec = pl.BlockSpec((tr, C), lambda i: (i, 0))
    return pl.pallas_call(
        body, name=name, grid=(R // tr,),
        out_shape=[jax.ShapeDtypeStruct((R, C), F32)] * 4,
        in_specs=[spec] * 5, out_specs=[spec] * 4,
        compiler_params=_cparams(),
    )(own, other, w, m, v)


def _pick(n, options):
    for o in options:
        if n % o == 0:
            return o
    return n


def _div128(n, cap):
    best = None
    for t in range(128, min(n, cap) + 1, 128):
        if n % t == 0:
            best = t
    return best or n


MATMUL_VMEM = 40 * 1024 * 1024


def _matmul(a, b, name, out_dtype=F32, residual=None):
    M, K = a.shape
    _, N = b.shape
    tm, tn = _pick(M, (1024, 512)), _div128(N, 1408)
    while tm > 256 and 2 * (2 * tm * K + 2 * K * tn + (8 if residual is not None else 4) * tm * tn) > MATMUL_VMEM:
        tm //= 2

    def body(*refs):
        a_ref, b_ref = refs[0], refs[1]
        o_ref = refs[-1]
        acc = jnp.dot(a_ref[...], b_ref[...], preferred_element_type=F32)
        if residual is not None:
            acc = acc + refs[2][...]
        o_ref[...] = acc.astype(out_dtype)

    in_specs = [pl.BlockSpec((tm, K), lambda j, i: (i, 0)), pl.BlockSpec((K, tn), lambda j, i: (0, j))]
    args = [a, b]
    if residual is not None:
        in_specs.append(pl.BlockSpec((tm, tn), lambda j, i: (i, j)))
        args.append(residual)
    return pl.pallas_call(
        body, name=name, grid=(N // tn, M // tm),
        out_shape=jax.ShapeDtypeStruct((M, N), out_dtype),
        in_specs=in_specs, out_specs=pl.BlockSpec((tm, tn), lambda j, i: (i, j)),
        compiler_params=_cparams(),
    )(*args)


def _matmul_tn(a, b, name):
    R, M = a.shape
    _, N = b.shape
    tr, tm, tn = _pick(R, (2048, 1024, 512)), _div128(M, 1408), _div128(N, 1408)
    while tr > 512 and 2 * (2 * tr * tm + 2 * tr * tn + 4 * tm * tn) > MATMUL_VMEM:
        tr //= 2

    def body(a_ref, b_ref, o_ref):
        @pl.when(pl.program_id(2) == 0)
        def _():
            o_ref[...] = jnp.zeros_like(o_ref)

        o_ref[...] += lax.dot_general(a_ref[...], b_ref[...], (((0,), (0,)), ((), ())),
                                      preferred_element_type=F32)

    return pl.pallas_call(
        body, name=name, grid=(M // tm, N // tn, R // tr),
        out_shape=jax.ShapeDtypeStruct((M, N), F32),
        in_specs=[pl.BlockSpec((tr, tm), lambda i, j, r: (r, i)), pl.BlockSpec((tr, tn), lambda i, j, r: (r, j))],
        out_specs=pl.BlockSpec((tm, tn), lambda i, j, r: (i, j)),
        compiler_params=_cparams(),
    )(a, b)


def _row_spec(tm, width, col, lead=None):
    if lead is None:
        return pl.BlockSpec((tm, width), lambda i: (i, col))
    return pl.BlockSpec((None, tm, width), lambda i: (lead, i, col))


def _whole_spec(arr):
    nd = arr.ndim
    return pl.BlockSpec(arr.shape, lambda i: (0,) * nd)


def _rowwise(name, fn, rows, params, consts, outs, n_rows, tm):
    nr, npar, nc = len(rows), len(params), len(consts)

    def body(*refs):
        vals = [r[...].astype(F32) for r in refs[:nr]] + [r[...] for r in refs[nr:nr + npar + nc]]
        res = fn(*vals)
        for o_ref, r in zip(refs[nr + npar + nc:], res):
            o_ref[...] = r.astype(o_ref.dtype)

    return pl.pallas_call(
        body, name=name, grid=(n_rows // tm,),
        out_shape=[jax.ShapeDtypeStruct((n_rows, w), dt) for w, dt in outs],
        in_specs=[_row_spec(tm, *r[1:]) for r in rows] + [_whole_spec(p) for p in params + consts],
        out_specs=[_row_spec(tm, w, 0) for w, _ in outs],
        compiler_params=_cparams(),
    )(*[r[0] for r in rows], *params, *consts)


def _rowwise_bwd(name, fn, rows, params, consts, cts, row_grads, n_rows, tm):
    nr, npar, nc = len(rows), len(params), len(consts)
    ct_flat = [p for pieces in cts for p in pieces]
    res_flat = [rg[1] for rg in row_grads if rg is not None and rg[1] is not None]
    n_ct, n_res = len(ct_flat), len(res_flat)
    n_in = nr + npar + nc + n_ct + n_res
    wanted = [k for k, rg in enumerate(row_grads) if rg is not None]

    def body(*refs):
        row_vals = [r[...].astype(F32) for r in refs[:nr]]
        par_vals = [r[...] for r in refs[nr:nr + npar]]
        const_vals = [r[...] for r in refs[nr + npar:nr + npar + nc]]
        ct_refs = refs[nr + npar + nc:nr + npar + nc + n_ct]
        res_refs = refs[nr + npar + nc + n_ct:n_in]
        out_refs = refs[n_in:]
        ct_vals, pos = [], 0
        for pieces in cts:
            acc = ct_refs[pos][...].astype(F32)
            for q in range(1, len(pieces)):
                acc = acc + ct_refs[pos + q][...].astype(F32)
            pos += len(pieces)
            ct_vals.append(acc)
        _, vjp = jax.vjp(lambda *a: tuple(fn(*a, *const_vals)), *row_vals, *par_vals)
        grads = vjp(tuple(ct_vals))
        ri = 0
        for slot, k in enumerate(wanted):
            g = grads[k]
            if row_grads[k][1] is not None:
                g = g + res_refs[ri][...].astype(F32)
                ri += 1
            out_refs[slot][...] = g.astype(out_refs[slot].dtype)

        @pl.when(pl.program_id(0) == 0)
        def _():
            for q in range(npar):
                out_refs[len(wanted) + q][...] = jnp.zeros_like(out_refs[len(wanted) + q])

        for q in range(npar):
            out_refs[len(wanted) + q][...] += grads[nr + q]

    out_shape = [jax.ShapeDtypeStruct((n_rows, rows[k][1]), row_grads[k][0]) for k in wanted]
    out_shape += [jax.ShapeDtypeStruct(p.shape, F32) for p in params]
    out_specs = [_row_spec(tm, rows[k][1], 0) for k in wanted] + [_whole_spec(p) for p in params]
    in_specs = [_row_spec(tm, *r[1:]) for r in rows] + [_whole_spec(p) for p in params + consts]
    in_specs += [_row_spec(tm, *r[1:]) for r in ct_flat + res_flat]
    res = pl.pallas_call(
        body, name=name, grid=(n_rows // tm,),
        out_shape=out_shape, in_specs=in_specs, out_specs=out_specs,
        compiler_params=_cparams(),
    )(*[r[0] for r in rows], *params, *consts, *[r[0] for r in ct_flat + res_flat])
    return res[:len(wanted)], res[len(wanted):]


def _sigmoid(x):
    return 0.5 * jnp.tanh(0.5 * x) + 0.5


def _softplus(x):
    return jnp.maximum(x, 0.0) + jnp.log(1.0 + jnp.exp(-jnp.abs(x)))


def _seg_dot_impl(x, seg2):
    hi = x.astype(BF16)
    lo = (x - hi.astype(F32)).astype(BF16)
    return jnp.dot(jnp.concatenate([hi, lo], axis=1), seg2, preferred_element_type=F32)


@jax.custom_vjp
def _seg_dot(x, seg2):
    return _seg_dot_impl(x, seg2)


_seg_dot.defvjp(lambda x, seg2: (_seg_dot_impl(x, seg2), seg2),
                lambda seg2, ct: (_seg_dot_impl(ct, seg2), jnp.zeros_like(seg2)))


def _fn_norm(x, g):
    r = lax.rsqrt(jnp.mean(x * x, axis=-1, keepdims=True) + NORM_EPS)
    return ((x * r) * g,)


def _fn_gla_post(o_f, o_b, og, norm_g):
    o = o_f + o_b
    heads = []
    for h in range(4):
        oh = o[:, h * 128:(h + 1) * 128]
        heads.append(oh * lax.rsqrt(jnp.mean(oh * oh, axis=-1, keepdims=True) + HEAD_NORM_EPS))
    on = jnp.concatenate(heads, axis=1) * norm_g
    return (on * (og * _sigmoid(og)),)


def _fn_rwkv_pre(s_k, s_wag, w0_f, w0_b, a0, k_k, k_a, w2_f, w2_b, a2, g2, seg64):
    wa = s_wag[:, 0:128]
    gl = s_wag[:, 128:256]
    tw = jnp.tanh(wa)
    z_f = w0_f + jnp.dot(tw, w2_f, preferred_element_type=F32)
    z_b = w0_b + jnp.dot(tw, w2_b, preferred_element_type=F32)
    w_f = jnp.exp(-jnp.exp(-_softplus(-z_f) - 0.5))
    w_b = jnp.exp(-jnp.exp(-_softplus(-z_b) - 0.5))
    a = _sigmoid(a0 + jnp.dot(wa, a2, preferred_element_type=F32))
    g = jnp.dot(_sigmoid(gl), g2, preferred_element_type=F32)
    kk = s_k * k_k
    kkn = kk / jnp.maximum(jnp.sqrt(_seg_dot(kk * kk, seg64)), 1e-12)
    k2 = s_k * (1.0 + (a - 1.0) * k_a)
    return w_f, w_b, k2, -kkn, kkn * a, g


def _fn_rwkv_post(y_f, y_b, s_r, k2, s_v, g, ln_w, ln_b, r_k, seg64):
    y = y_f + y_b
    mu = _seg_dot(y, seg64) * (1.0 / 64.0)
    yc = y - mu
    var = _seg_dot(yc * yc, seg64) * (1.0 / 64.0)
    yn = yc * lax.rsqrt(var + RWKV_GN_EPS) * ln_w + ln_b
    bonus = _seg_dot(s_r * k2 * r_k, seg64) * s_v
    return ((yn + bonus) * g,)


def _fn_merge(ga, gb, y_a, y_b):
    return (_sigmoid(ga) * y_a + _sigmoid(gb) * y_b,)


def _loss_head(x2, target, gf):
    N, Dm = x2.shape
    tm = _pick(N, (512,))

    def fn(x, g, t):
        r = lax.rsqrt(jnp.mean(x * x, axis=-1, keepdims=True) + NORM_EPS)
        err = (x * r) * g - t
        return 0.5 * jnp.sum(jnp.mean(err * err, axis=-1, keepdims=True), axis=0, keepdims=True)

    def body(x_ref, t_ref, g_ref, loss_ref, dx_ref, dg_ref):
        t = t_ref[...]
        loss, vjp = jax.vjp(lambda x, g: fn(x, g, t), x_ref[...], g_ref[...])
        dx, dg = vjp(jnp.ones((1, 1), F32))

        @pl.when(pl.program_id(0) == 0)
        def _():
            loss_ref[...] = jnp.zeros_like(loss_ref)
            dg_ref[...] = jnp.zeros_like(dg_ref)

        loss_ref[...] += jnp.broadcast_to(loss, loss_ref.shape)
        dg_ref[...] += dg
        dx_ref[...] = dx

    return pl.pallas_call(
        body, name="loss_head", grid=(N // tm,),
        out_shape=[jax.ShapeDtypeStruct((8, 128), F32), jax.ShapeDtypeStruct((N, Dm), F32),
                   jax.ShapeDtypeStruct((1, Dm), F32)],
        in_specs=[_row_spec(tm, Dm, 0), _row_spec(tm, Dm, 0), _whole_spec(gf)],
        out_specs=[pl.BlockSpec((8, 128), lambda i: (0, 0)), _row_spec(tm, Dm, 0),
                   pl.BlockSpec((1, Dm), lambda i: (0, 0))],
        compiler_params=_cparams(),
    )(x2, target, gf)


def _shift_prev(u):
    rolled = pltpu.roll(u, 1, axis=0)
    row = lax.broadcasted_iota(jnp.int32, u.shape, 0)
    return jnp.where(row == 0, 0.0, rolled)


def _shift_next(u):
    T = u.shape[0]
    rolled = pltpu.roll(u, T - 1, axis=0)
    row = lax.broadcasted_iota(jnp.int32, u.shape, 0)
    return jnp.where(row == T - 1, 0.0, rolled)


_SHIFT_BLOCKS = 7


def _shift_src_col(j):
    return jnp.where(j < 6, 6 + j, 20)


def _token_shift(p, mu_prev, mu_next, B, T):
    def body(p_ref, mp_ref, mn_ref, s_ref):
        u = p_ref[...]
        s_ref[...] = u + mp_ref[...] * (_shift_prev(u) - u) + mn_ref[...] * (_shift_next(u) - u)

    return pl.pallas_call(
        body, name="token_shift", grid=(B, _SHIFT_BLOCKS),
        out_shape=jax.ShapeDtypeStruct((B * T, 1792), F32),
        in_specs=[pl.BlockSpec((T, 256), lambda b, j: (b, _shift_src_col(j))),
                  pl.BlockSpec((1, 256), lambda b, j: (0, j)), pl.BlockSpec((1, 256), lambda b, j: (0, j))],
        out_specs=pl.BlockSpec((T, 256), lambda b, j: (b, j)),
        compiler_params=_cparams(),
    )(p, mu_prev, mu_next)


def _token_shift_bwd(p, ds, mu_prev, mu_next, B, T):
    def body(p_ref, ds_ref, mp_ref, mn_ref, dp_ref, dmp_ref, dmn_ref):
        u, d = p_ref[...], ds_ref[...]
        mp, mn = mp_ref[...], mn_ref[...]
        dp = d * (1.0 - mp - mn) + _shift_next(d * mp) + _shift_prev(d * mn)
        dp_ref[...] = dp.astype(dp_ref.dtype)

        @pl.when(pl.program_id(1) == 0)
        def _():
            dmp_ref[...] = jnp.zeros_like(dmp_ref)
            dmn_ref[...] = jnp.zeros_like(dmn_ref)

        dmp_ref[...] += jnp.sum(d * (_shift_prev(u) - u), axis=0, keepdims=True)
        dmn_ref[...] += jnp.sum(d * (_shift_next(u) - u), axis=0, keepdims=True)

    return pl.pallas_call(
        body, name="token_shift_bwd", grid=(_SHIFT_BLOCKS, B),
        out_shape=[jax.ShapeDtypeStruct((B * T, 1792), BF16), jax.ShapeDtypeStruct((1, 1792), F32),
                   jax.ShapeDtypeStruct((1, 1792), F32)],
        in_specs=[pl.BlockSpec((T, 256), lambda j, b: (b, _shift_src_col(j))),
                  pl.BlockSpec((T, 256), lambda j, b: (b, j)),
                  pl.BlockSpec((1, 256), lambda j, b: (0, j)), pl.BlockSpec((1, 256), lambda j, b: (0, j))],
        out_specs=[pl.BlockSpec((T, 256), lambda j, b: (b, j)), pl.BlockSpec((1, 256), lambda j, b: (0, j)),
                   pl.BlockSpec((1, 256), lambda j, b: (0, j))],
        compiler_params=_cparams(),
    )(p, ds, mu_prev, mu_next)


_FF_BLOCKS = FF_PAD // 256


def _conv3(u, cw, cb):
    return cw[0:1] * _shift_prev(u) + cw[1:2] * u + cw[2:3] * _shift_next(u) + cb


def _ffn_conv(u, cw, cb, B, T):
    def body(ug_ref, uv_ref, cwg_ref, cwv_ref, cbg_ref, cbv_ref, o_ref):
        cg = _conv3(ug_ref[...], cwg_ref[...], cbg_ref[...])
        cv = _conv3(uv_ref[...], cwv_ref[...], cbv_ref[...])
        o_ref[...] = (cg * _sigmoid(cg) * cv).astype(o_ref.dtype)

    nb = _FF_BLOCKS
    return pl.pallas_call(
        body, name="ffn_conv", grid=(B, nb),
        out_shape=jax.ShapeDtypeStruct((B * T, FF_PAD), BF16),
        in_specs=[pl.BlockSpec((T, 256), lambda b, j: (b, j)), pl.BlockSpec((T, 256), lambda b, j: (b, j + nb)),
                  pl.BlockSpec((3, 256), lambda b, j: (0, j)), pl.BlockSpec((3, 256), lambda b, j: (0, j + nb)),
                  pl.BlockSpec((1, 256), lambda b, j: (0, j)), pl.BlockSpec((1, 256), lambda b, j: (0, j + nb))],
        out_specs=pl.BlockSpec((T, 256), lambda b, j: (b, j)),
        compiler_params=_cparams(),
    )(u, u, cw, cw, cb, cb)


def _ffn_conv_bwd(u, dact, cw, cb, B, T):
    def half(u_, dc, cw_):
        du = _shift_next(cw_[0:1] * dc) + cw_[1:2] * dc + _shift_prev(cw_[2:3] * dc)
        dcw = jnp.concatenate([jnp.sum(dc * _shift_prev(u_), axis=0, keepdims=True),
                               jnp.sum(dc * u_, axis=0, keepdims=True),
                               jnp.sum(dc * _shift_next(u_), axis=0, keepdims=True)], axis=0)
        return du, dcw, jnp.sum(dc, axis=0, keepdims=True)

    def body(ug_ref, uv_ref, da_ref, cwg_ref, cwv_ref, cbg_ref, cbv_ref,
             dug_ref, duv_ref, dcwg_ref, dcwv_ref, dcbg_ref, dcbv_ref):
        ug, uv, da = ug_ref[...], uv_ref[...], da_ref[...]
        cwg, cwv = cwg_ref[...], cwv_ref[...]
        cg = _conv3(ug, cwg, cbg_ref[...])
        cv = _conv3(uv, cwv, cbv_ref[...])
        sg = _sigmoid(cg)
        dcv = da * (cg * sg)
        dcg = da * cv * (sg * (1.0 + cg * (1.0 - sg)))
        dug, dcwg, dcbg = half(ug, dcg, cwg)
        duv, dcwv, dcbv = half(uv, dcv, cwv)
        dug_ref[...] = dug.astype(dug_ref.dtype)
        duv_ref[...] = duv.astype(duv_ref.dtype)

        @pl.when(pl.program_id(1) == 0)
        def _():
            for r in (dcwg_ref, dcwv_ref, dcbg_ref, dcbv_ref):
                r[...] = jnp.zeros_like(r)

        dcwg_ref[...] += dcwg
        dcwv_ref[...] += dcwv
        dcbg_ref[...] += dcbg
        dcbv_ref[...] += dcbv

    nb = _FF_BLOCKS
    N = B * T
    res = pl.pallas_call(
        body, name="ffn_conv_bwd", grid=(nb, B),
        out_shape=[jax.ShapeDtypeStruct((N, FF_PAD), BF16), jax.ShapeDtypeStruct((N, FF_PAD), BF16),
                   jax.ShapeDtypeStruct((3, FF_PAD), F32), jax.ShapeDtypeStruct((3, FF_PAD), F32),
                   jax.ShapeDtypeStruct((1, FF_PAD), F32), jax.ShapeDtypeStruct((1, FF_PAD), F32)],
        in_specs=[pl.BlockSpec((T, 256), lambda j, b: (b, j)), pl.BlockSpec((T, 256), lambda j, b: (b, j + nb)),
                  pl.BlockSpec((T, 256), lambda j, b: (b, j)),
                  pl.BlockSpec((3, 256), lambda j, b: (0, j)), pl.BlockSpec((3, 256), lambda j, b: (0, j + nb)),
                  pl.BlockSpec((1, 256), lambda j, b: (0, j)), pl.BlockSpec((1, 256), lambda j, b: (0, j + nb))],
        out_specs=[pl.BlockSpec((T, 256), lambda j, b: (b, j)), pl.BlockSpec((T, 256), lambda j, b: (b, j)),
                   pl.BlockSpec((3, 256), lambda j, b: (0, j)), pl.BlockSpec((3, 256), lambda j, b: (0, j)),
                   pl.BlockSpec((1, 256), lambda j, b: (0, j)), pl.BlockSpec((1, 256), lambda j, b: (0, j))],
        compiler_params=_cparams(),
    )(u, u, dact, cw, cw, cb, cb)
    dug, duv, dcwg, dcwv, dcbg, dcbv = res
    return dug, duv, jnp.concatenate([dcwg, dcwv], axis=1), jnp.concatenate([dcbg, dcbv], axis=1)


def _gla_chunk(q, k, v, afab, wa2, ba, state, rev):
    C = GLA_CHUNK
    n = q.shape[0]
    z = jnp.dot(afab.reshape(n * C, 128), wa2, preferred_element_type=F32).reshape(n, C, 256) + ba
    la = (jnp.minimum(z, 0.0) - jnp.log(1.0 + jnp.exp(-jnp.abs(z)))) * (1.0 / 16.0)
    row = lax.broadcasted_iota(jnp.int32, (n * C, n * C), 0)
    col = lax.broadcasted_iota(jnp.int32, (n * C, n * C), 1)
    ordered = ((col & (C - 1)) - (row & (C - 1))) * (1 - 2 * rev) <= 0
    tri_all = ordered & ((row >> 6) == (col >> 6))
    b = jnp.dot(tri_all.astype(F32), la.reshape(n * C, 256), precision=HIGHEST,
                preferred_element_type=F32).reshape(n, C, 256)
    tri = (lax.broadcasted_iota(jnp.int32, (C, C), 1) - lax.broadcasted_iota(jnp.int32, (C, C), 0)) * (1 - 2 * rev) <= 0
    rows = lax.broadcasted_iota(jnp.int32, (n, C, 256), 1)
    ref_row = jnp.where(rev == 0, C // 2, C - 1 - C // 2)
    last_row = jnp.where(rev == 0, C - 1, 0)
    b_ref = jnp.sum(jnp.where(rows == ref_row, b, 0.0), axis=1, keepdims=True)
    b_last = jnp.sum(jnp.where(rows == last_row, b, 0.0), axis=1, keepdims=True)
    qs = q * 0.125
    qi = qs * jnp.exp(b - b_ref)
    ki = k * jnp.exp(b_ref - b)
    kd = k * jnp.exp(b_last - b)
    qe = qs * jnp.exp(b)
    lane = lax.broadcasted_iota(jnp.int32, (1, 1, 256), 2)
    bdot = lambda x, y, cx, cy: lax.dot_general(x, y, (((cx,), (cy,)), ((0,), (0,))), preferred_element_type=F32)
    outs = []
    upd = jnp.zeros_like(state)
    for h in range(4):
        mh = ((lane >= 64 * h) & (lane < 64 * (h + 1))).astype(F32)
        vh = v[:, :, 128 * h:128 * (h + 1)]
        a = jnp.where(tri, bdot(qi * mh, ki, 2, 2), 0.0)
        outs.append(bdot(a, vh, 2, 1) + bdot(qe, state * mh, 2, 2))
        upd = upd + bdot(vh, kd * mh, 1, 1)
    return jnp.concatenate(outs, axis=2), state * jnp.exp(b_last) + upd


def _gla_chunk_at(nC):
    return lambda d, j: j + d * (nC - 1 - 2 * j)


def _gla_fwd(p, wa2, ba, B, T):
    nC = T // GLA_CHUNK
    N = B * T
    at = _gla_chunk_at(nC)
    p3 = p.reshape(B, T, p.shape[-1])

    def body(q_ref, k_ref, v_ref, af_ref, wa_ref, ba_ref, o_ref, st_ref, state):
        @pl.when(pl.program_id(1) == 0)
        def _():
            state[...] = jnp.zeros_like(state)

        st_ref[0, :, 0] = state[...]
        o, new = _gla_chunk(q_ref[...], k_ref[...], v_ref[...], af_ref[...], wa_ref[0], ba_ref[0], state[...],
                            pl.program_id(0))
        o_ref[0] = o
        state[...] = new

    blk = lambda w, col: pl.BlockSpec((B, 64, w), lambda d, j: (0, at(d, j), col))
    o, st = pl.pallas_call(
        body, name="gla_fwd", grid=(2, nC),
        out_shape=[jax.ShapeDtypeStruct((2, B, T, 512), F32), jax.ShapeDtypeStruct((2, B, nC, 128, 256), F32)],
        in_specs=[blk(256, 0), blk(256, 1), blk(512, 1), blk(128, 42),
                  pl.BlockSpec((1, 128, 256), lambda d, j: (d, 0, 0)),
                  pl.BlockSpec((1, 1, 256), lambda d, j: (d, 0, 0))],
        out_specs=[pl.BlockSpec((1, B, 64, 512), lambda d, j: (d, 0, at(d, j), 0)),
                   pl.BlockSpec((1, B, 1, 128, 256), lambda d, j: (d, 0, at(d, j), 0, 0))],
        scratch_shapes=[pltpu.VMEM((B, 128, 256), F32)],
        compiler_params=_cparams(),
    )(p3, p3, p3, p3, wa2, ba)
    return o.reshape(2, N, 512), st


def _gla_bwd(p, do, states, wa2, ba, B, T):
    nC = T // GLA_CHUNK
    N = B * T
    at_f = _gla_chunk_at(nC)
    at = lambda d, j: at_f(d, nC - 1 - j)
    p3 = p.reshape(B, T, p.shape[-1])

    def body(q_ref, k_ref, v_ref, af_ref, wa_ref, ba_ref, do_ref, st_ref,
             dqkv_ref, daf_ref, dwa_ref, dba_ref, dstate):
        rev = pl.program_id(0)

        @pl.when(pl.program_id(1) == 0)
        def _():
            dstate[...] = jnp.zeros_like(dstate)
            dwa_ref[...] = jnp.zeros_like(dwa_ref)
            dba_ref[...] = jnp.zeros_like(dba_ref)

        f = lambda q, k, v, af, wa, bb, st: _gla_chunk(q, k, v, af, wa, bb, st, rev)
        _, vjp = jax.vjp(f, q_ref[...], k_ref[...], v_ref[...], af_ref[...], wa_ref[0], ba_ref[0], st_ref[0, :, 0])
        dq, dk, dv, daf, dwa, dba, dst = vjp((do_ref[...], dstate[...]))
        dqkv_ref[0] = jnp.concatenate([dq, dk, dv], axis=2)
        daf_ref[0] = daf
        dwa_ref[0] += dwa
        dba_ref[0] += dba
        dstate[...] = dst

    blk = lambda w, col: pl.BlockSpec((B, 64, w), lambda d, j: (0, at(d, j), col))
    out4 = lambda w: pl.BlockSpec((1, B, 64, w), lambda d, j: (d, 0, at(d, j), 0))
    dqkv, daf, dwa, dba = pl.pallas_call(
        body, name="gla_bwd", grid=(2, nC),
        out_shape=[jax.ShapeDtypeStruct((2, B, T, 1024), F32), jax.ShapeDtypeStruct((2, B, T, 128), F32),
                   jax.ShapeDtypeStruct((2, 128, 256), F32), jax.ShapeDtypeStruct((2, 1, 256), F32)],
        in_specs=[blk(256, 0), blk(256, 1), blk(512, 1), blk(128, 42),
                  pl.BlockSpec((1, 128, 256), lambda d, j: (d, 0, 0)),
                  pl.BlockSpec((1, 1, 256), lambda d, j: (d, 0, 0)),
                  blk(512, 0),
                  pl.BlockSpec((1, B, 1, 128, 256), lambda d, j: (d, 0, at(d, j), 0, 0))],
        out_specs=[out4(1024), out4(128),
                   pl.BlockSpec((1, 128, 256), lambda d, j: (d, 0, 0)),
                   pl.BlockSpec((1, 1, 256), lambda d, j: (d, 0, 0))],
        scratch_shapes=[pltpu.VMEM((B, 128, 256), F32)],
        compiler_params=_cparams(),
    )(p3, p3, p3, p3, wa2, ba, do.reshape(B, T, 512), states)
    return dqkv.reshape(2, N, 1024), daf.reshape(2, N, 128), dwa, dba


def _seg_ones():
    m = lax.broadcasted_iota(jnp.int32, (256, 128), 0)
    n = lax.broadcasted_iota(jnp.int32, (256, 128), 1)
    return (((m >> 6) & 1) == (n >> 6)).astype(BF16)


def _seg_mm(x, ones2):
    hi = x.astype(BF16)
    lo = (x - hi.astype(F32)).astype(BF16)
    return jnp.dot(jnp.concatenate([hi, lo], axis=1), ones2, preferred_element_type=F32)


def _diag_matrix():
    r = np.arange(2048)[:, None] % 64
    c = np.arange(128)[None, :] % 64
    return jnp.asarray((r == c).astype(np.float32))


def _cols8(tile, dg, ones2):
    return _seg_mm(jnp.concatenate([_rows4(tile, q) for q in range(8)], axis=0) * dg[...], ones2)


def _rows4(tile, q):
    return jnp.concatenate([jnp.broadcast_to(tile[q:q + 1, 128 * p:128 * (p + 1)], (64, 128)) for p in range(4)],
                           axis=0)


def _head_rows():
    r = lax.broadcasted_iota(jnp.int32, (16, 256), 0)
    n = lax.broadcasted_iota(jnp.int32, (16, 256), 1)
    return (r == ((n >> 6) & 1)).astype(BF16)


def _head_sums_row(x, heads2):
    hi = x.astype(BF16)
    lo = (x - hi.astype(F32)).astype(BF16)
    out = lax.dot_general(heads2, jnp.concatenate([hi, lo], axis=1), (((1,), (1,)), ((), ())),
                          preferred_element_type=F32)
    return jnp.concatenate([out[0:1], out[1:2]], axis=1)


def _pair_major(y):
    n = y.shape[0]
    return y.reshape(n, 2, 4, 64).transpose(0, 2, 1, 3).reshape(n, 512)


def _colsum4(m):
    return jnp.concatenate([jnp.sum(m[64 * p:64 * (p + 1)], axis=0, keepdims=True) for p in range(4)], axis=1)


def _time_base(gi, n_groups, rev):
    return pl.multiple_of(((n_groups - 1 - gi) if rev else gi) * 8, 8)


def _scan_fwd_mxu(s, wf, wb, k2, na, bb, B, T):
    Tc = SCAN_CHUNK
    nT = T // Tc
    nG = Tc // 8
    N = B * T
    nb = _pick(B, (4, 2))
    fwd_j = lambda j: j
    bwd_j = lambda j: nT - 1 - j

    def body(rF, vF, kF, aF, bF, wF, rB, vB, kB, aB, bB, wB, dg, yF, yB, ckF, ckB,
             SF, SB, vcF, vcB, ypF, ypB, ytF, ytB):
        @pl.when(pl.program_id(1) == 0)
        def _():
            SF[...] = jnp.zeros_like(SF)
            SB[...] = jnp.zeros_like(SB)

        ckF[...] = SF[...]
        ckB[...] = SB[...]
        ones2, heads2 = _seg_ones(), _head_rows()
        chains = []
        for n in range(nb):
            chains.append((n, SF, (rF, vF, kF, aF, bF, wF), yF, vcF, ypF, ytF, False))
            chains.append((n, SB, (rB, vB, kB, aB, bB, wB), yB, vcB, ypB, ytB, True))

        def group(gi, carry):
            tiles, states = [], []
            for n, S_ref, refs, _, vc, _, _, rev in chains:
                base = _time_base(gi, nG, rev)
                t = [ref[n, pl.ds(base, 8), :] for ref in refs]
                tiles.append(t)
                states.append(S_ref[n])
                vc[n] = _cols8(t[1], dg, ones2)
            for i8 in range(8):
                for c, (n, _, _, _, vc, yp, _, rev) in enumerate(chains):
                    q = 7 - i8 if rev else i8
                    r, v, k, a, b, w = tiles[c]
                    S = states[c]
                    sa = _seg_mm(S * _rows4(a, q), ones2)
                    S = S * _rows4(w, q) + sa * _rows4(b, q) + vc[n, 256 * q:256 * (q + 1), :] * _rows4(k, q)
                    yp[n, 256 * q:256 * (q + 1), :] = S * _rows4(r, q)
                    states[c] = S
            for c, (n, S_ref, _, y_ref, _, yp, yt, rev) in enumerate(chains):
                S_ref[n] = states[c]
                for q in range(8):
                    yt[n, q:q + 1, :] = _head_sums_row(yp[n, 256 * q:256 * (q + 1), :], heads2)
                y_ref[n, pl.ds(_time_base(gi, nG, rev), 8), :] = yt[n]
            return carry

        lax.fori_loop(0, nG, group, 0)

    row_in = lambda at, col: pl.BlockSpec((nb, Tc, 512), lambda g, j: (g, at(j), col))
    state_io = lambda at: pl.BlockSpec((nb, 256, 128), lambda g, j: (g, at(j), 0))
    in_specs = []
    for at in (fwd_j, bwd_j):
        in_specs += [row_in(at, 0), row_in(at, 2)] + [row_in(at, 0)] * 4
    big = pltpu.VMEM((nb, 8 * 256, 128), F32)
    s3 = s.reshape(B, T, s.shape[-1])
    seq = lambda a: a.reshape(B, T, 512)
    y_f, y_b, ck_f, ck_b = pl.pallas_call(
        body, name="rwkv_scan", grid=(B // nb, nT),
        out_shape=[jax.ShapeDtypeStruct((B, T, 512), F32), jax.ShapeDtypeStruct((B, T, 512), F32),
                   jax.ShapeDtypeStruct((B, nT * 256, 128), F32), jax.ShapeDtypeStruct((B, nT * 256, 128), F32)],
        in_specs=in_specs + [pl.BlockSpec((2048, 128), lambda g, j: (0, 0))],
        out_specs=[row_in(fwd_j, 0), row_in(bwd_j, 0), state_io(fwd_j), state_io(bwd_j)],
        scratch_shapes=[pltpu.VMEM((nb, 256, 128), F32), pltpu.VMEM((nb, 256, 128), F32), big, big, big, big,
                        pltpu.VMEM((nb, 8, 512), F32), pltpu.VMEM((nb, 8, 512), F32)],
        compiler_params=_cparams(),
    )(s3, s3, seq(k2), seq(na), seq(bb), seq(wf), s3, s3, seq(k2), seq(na), seq(bb), seq(wb), _diag_matrix())
    ck_shape = (B * nT * 256, 128)
    return (_pair_major(y_f.reshape(N, 512)), _pair_major(y_b.reshape(N, 512)),
            ck_f.reshape(ck_shape), ck_b.reshape(ck_shape))


def _scan_bwd_mxu(s, wf, wb, k2, na, bb, dy, ckF, ckB, B, T):
    Tc = SCAN_CHUNK
    nT = T // Tc
    nG = Tc // 8
    N = B * T
    nb = _pick(B, (2,))
    f_at = lambda j: nT - 1 - j
    b_at = lambda j: j
    n_in, n_out, n_scr = 17, 12, 12

    def body(*refs):
        (rF, vF, kF, aF, bF, wF, dyF, ckF_ref, rB, vB, kB, aB, bB, wB, dyB, ckB_ref, dg) = refs[:n_in]
        outsF, outsB = refs[n_in:n_in + 6], refs[n_in + 6:n_in + n_out]
        chains = []
        for n in range(nb):
            stF, stB, saF, saB, vcF, vcB, dSF, dSB, bigF, bigB, tileF, tileB = \
                refs[n_in + n_out + n_scr * n:n_in + n_out + n_scr * (n + 1)]
            chains.append((n, stF, dSF, ckF_ref, (rF, vF, kF, aF, bF, wF, dyF), outsF, bigF, tileF, False, saF, vcF))
            chains.append((n, stB, dSB, ckB_ref, (rB, vB, kB, aB, bB, wB, dyB), outsB, bigB, tileB, True, saB, vcB))

        @pl.when(pl.program_id(1) == 0)
        def _():
            for chain in chains:
                chain[2][...] = jnp.zeros_like(chain[2])

        ones2, heads2 = _seg_ones(), _head_rows()
        for chain in chains:
            chain[1][0] = chain[3][chain[0]]

        cols8 = lambda tile: _cols8(tile, dg, ones2)

        def recompute(gi, carry):
            tiles, states = [], []
            for n, st, _, _, ins, _, big, _, rev, _, vc_keep in chains:
                base = _time_base(gi, nG, rev)
                t = [ref[n, pl.ds(base, 8), :] for ref in ins[1:6]]
                tiles.append(t)
                states.append(st[gi * 8])
                v_cols = cols8(t[0])
                for i8 in range(8):
                    q = 7 - i8 if rev else i8
                    vc_keep[gi * 8 + i8] = v_cols[256 * q:256 * (q + 1)]
            for i8 in range(8):
                for c, (_, st, _, _, _, _, _, _, rev, sa_keep, vc_keep) in enumerate(chains):
                    q = 7 - i8 if rev else i8
                    v, k, a, b, w = tiles[c]
                    S = states[c]
                    sa = _seg_mm(S * _rows4(a, q), ones2)
                    sa_keep[gi * 8 + i8] = sa
                    S = S * _rows4(w, q) + sa * _rows4(b, q) + vc_keep[gi * 8 + i8] * _rows4(k, q)
                    st[gi * 8 + i8 + 1] = S
                    states[c] = S
            return carry

        lax.fori_loop(0, nG, recompute, 0)

        def back(gg, carry):
            gi = nG - 1 - gg
            tiles, grads = [], []
            for n, st, dS_ref, _, ins, _, big, _, rev, _, _ in chains:
                base = _time_base(gi, nG, rev)
                t = [ref[n, pl.ds(base, 8), :] for ref in ins]
                tiles.append(t)
                grads.append(dS_ref[...])
                big[0] = cols8(t[6])
            for i8 in range(7, -1, -1):
                for c, (_, st, _, _, _, _, big, tile, rev, sa_keep, vc_keep) in enumerate(chains):
                    q = 7 - i8 if rev else i8
                    r, v, k, a, b, w, _ = tiles[c]
                    i = gi * 8 + i8
                    S_prev, S_t = st[i], st[i + 1]
                    rows = slice(256 * q, 256 * (q + 1))
                    dy_col, v_col, sa = big[0, rows, :], vc_keep[i], sa_keep[i]
                    dS = grads[c] + dy_col * _rows4(r, q)
                    sb = _seg_mm(dS * _rows4(b, q), ones2)
                    big[1, rows, :] = dS * _rows4(k, q)
                    tile[0, q:q + 1, :] = _colsum4(S_t * dy_col)
                    tile[2, q:q + 1, :] = _colsum4(dS * v_col)
                    tile[3, q:q + 1, :] = _colsum4(S_prev * sb)
                    tile[4, q:q + 1, :] = _colsum4(dS * sa)
                    tile[5, q:q + 1, :] = _colsum4(S_prev * dS)
                    grads[c] = dS * _rows4(w, q) + sb * _rows4(a, q)
            for c, (n, _, dS_ref, _, _, outs, big, tile, rev, _, _) in enumerate(chains):
                dS_ref[...] = grads[c]
                for q in range(8):
                    tile[1, q:q + 1, :] = _head_sums_row(big[1, 256 * q:256 * (q + 1), :], heads2)
                base = _time_base(gi, nG, rev)
                for o, o_ref in enumerate(outs):
                    o_ref[n, pl.ds(base, 8), :] = tile[o]
            return carry

        lax.fori_loop(0, nG, back, 0)

    row_io = lambda at, col: pl.BlockSpec((nb, Tc, 512), lambda g, j: (g, at(j), col))
    in_specs = []
    for at in (f_at, b_at):
        in_specs += [row_io(at, 0), row_io(at, 2)] + [row_io(at, 0)] * 5
        in_specs.append(pl.BlockSpec((nb, 256, 128), lambda g, j, at=at: (g, at(j), 0)))
    in_specs.append(pl.BlockSpec((2048, 128), lambda g, j: (0, 0)))
    out_specs = [row_io(f_at, 0)] * 6 + [row_io(b_at, 0)] * 6
    big = pltpu.VMEM((2, 8 * 256, 128), F32)
    states = pltpu.VMEM((Tc + 1, 256, 128), F32)
    per_step = pltpu.VMEM((Tc, 256, 128), F32)
    one_slot = [states, states, per_step, per_step, per_step, per_step,
                pltpu.VMEM((256, 128), F32), pltpu.VMEM((256, 128), F32), big, big,
                pltpu.VMEM((6, 8, 512), F32), pltpu.VMEM((6, 8, 512), F32)]
    s3 = s.reshape(B, T, s.shape[-1])
    seq = lambda a: a.reshape(B, T, 512)
    ck3 = lambda a: a.reshape(B, nT * 256, 128)
    outs = pl.pallas_call(
        body, name="rwkv_scan_bwd", grid=(B // nb, nT),
        out_shape=[jax.ShapeDtypeStruct((B, T, 512), F32)] * 12,
        in_specs=in_specs, out_specs=out_specs,
        scratch_shapes=one_slot * nb,
        compiler_params=_cparams(),
    )(s3, s3, seq(k2), seq(na), seq(bb), seq(wf), seq(dy), ck3(ckF),
      s3, s3, seq(k2), seq(na), seq(bb), seq(wb), seq(dy), ck3(ckB), _diag_matrix())
    outs = [o.reshape(N, 512) for o in outs]
    outs[1], outs[7] = _pair_major(outs[1]), _pair_major(outs[7])
    return outs


def _cat_shards(g4, name, axis):
    return jnp.concatenate([g4[s][name] for s in range(4)], axis=axis)


def _split_shards(full, axis):
    return jnp.split(full, 4, axis=axis)


def kernel(x, norm1_g, w_in, gla_wa2_f, gla_ba_f, gla_wa2_b, gla_ba_b, gla_norm_g, gla_proj, rwkv_mu_prev, rwkv_mu_next, rwkv_w0_f, rwkv_w2_f, rwkv_w0_b, rwkv_w2_b, rwkv_a0, rwkv_a2, rwkv_g2, rwkv_k_k, rwkv_k_a, rwkv_r_k, rwkv_ln_w, rwkv_ln_b, rwkv_proj, w_out, norm2_g, ffn_up, ffn_conv_w, ffn_conv_b, ffn_down, norm_f_g, loss_target, m_norm1_g, m_w_in, m_gla_wa2_f, m_gla_ba_f, m_gla_wa2_b, m_gla_ba_b, m_gla_norm_g, m_gla_proj, m_rwkv_mu_prev, m_rwkv_mu_next, m_rwkv_w0_f, m_rwkv_w2_f, m_rwkv_w0_b, m_rwkv_w2_b, m_rwkv_a0, m_rwkv_a2, m_rwkv_g2, m_rwkv_k_k, m_rwkv_k_a, m_rwkv_r_k, m_rwkv_ln_w, m_rwkv_ln_b, m_rwkv_proj, m_w_out, m_norm2_g, m_ffn_up, m_ffn_conv_w, m_ffn_conv_b, m_ffn_down, m_norm_f_g, v_norm1_g, v_w_in, v_gla_wa2_f, v_gla_ba_f, v_gla_wa2_b, v_gla_ba_b, v_gla_norm_g, v_gla_proj, v_rwkv_mu_prev, v_rwkv_mu_next, v_rwkv_w0_f, v_rwkv_w2_f, v_rwkv_w0_b, v_rwkv_w2_b, v_rwkv_a0, v_rwkv_a2, v_rwkv_g2, v_rwkv_k_k, v_rwkv_k_a, v_rwkv_r_k, v_rwkv_ln_w, v_rwkv_ln_b, v_rwkv_proj, v_w_out, v_norm2_g, v_ffn_up, v_ffn_conv_w, v_ffn_conv_b, v_ffn_down, v_norm_f_g):
    args = locals()
    weights = {n: args[n] for n in WEIGHT_ORDER}
    mom_m = {n: args["m_" + n] for n in WEIGHT_ORDER}
    mom_v = {n: args["v_" + n] for n in WEIGHT_ORDER}
    shapes = {n: weights[n].shape for n in WEIGHT_ORDER}
    B, T, _ = x.shape
    N = B * T
    tm = _pick(N, (512,))

    def local(d):
        sh = {n: d[n].reshape(s) for n, s, _ in SHARDED}
        rp = {n: d[n].reshape(-1) for n, _ in REPLICATED}
        return sh, rp

    w_loc, m_loc, v_loc = local(weights), local(mom_m), local(mom_v)

    small_of = lambda loc: _pack(loc[0], _pack_replicated(loc[1]))
    w_small = small_of(w_loc)
    gathered = _exchange_chips([w_loc[0][n].astype(BF16) for n, _, _ in BIG] + [w_small],
                               "allgather_weights", gather=True)
    small_vals = [_unpack(gathered[-1][s]) for s in range(4)]
    W = {n: jnp.concatenate([gathered[i][s] for s in range(4)], axis=ax) for i, (n, _, ax) in enumerate(BIG)}
    W.update({n: _cat_shards(small_vals, n, ax) for n, _, ax in SMALL})
    R = {n: weights[n].reshape(1, -1) for n, _ in REPLICATED}

    zc = lambda r, c, dt=F32: jnp.zeros((r, c), dt)
    w_in_full = W["w_in"]
    w_in_p = jnp.concatenate([w_in_full[:, 0:1536], w_in_full[:, 1568:3104], w_in_full[:, 3360:5408],
                              w_in_full[:, 3104:3360], w_in_full[:, 1536:1568],
                              zc(1024, PROJ_PAD - N_PROJ, BF16)], axis=1)
    w_in_b = w_in_p
    pad_ff = lambda a: jnp.concatenate([a[:, :D_FF], zc(a.shape[0], FF_PAD - D_FF, a.dtype), a[:, D_FF:],
                                        zc(a.shape[0], FF_PAD - D_FF, a.dtype)], axis=1)
    ffn_up_p = pad_ff(W["ffn_up"])
    ffn_up_b = ffn_up_p
    conv_w_p = pad_ff(W["ffn_conv_w"])
    conv_b_p = pad_ff(R["ffn_conv_b"])
    ffn_down_p = jnp.concatenate([W["ffn_down"], zc(FF_PAD - D_FF, 1024, BF16)], axis=0)
    ffn_down_b = ffn_down_p
    w_out_b = W["w_out"]
    gla_proj_b = W["gla_proj"]
    rwkv_proj_b = W["rwkv_proj"]
    wa2 = jnp.stack([jnp.concatenate([W["gla_wa2_f"], zc(112, 256)], axis=0),
                     jnp.concatenate([zc(16, 256), W["gla_wa2_b"], zc(96, 256)], axis=0)])
    ba = jnp.stack([R["gla_ba_f"], R["gla_ba_b"]])
    w2_f = jnp.concatenate([W["rwkv_w2_f"], zc(64, 512)], axis=0)
    w2_b = jnp.concatenate([W["rwkv_w2_b"], zc(64, 512)], axis=0)
    a2 = jnp.concatenate([zc(64, 512), W["rwkv_a2"]], axis=0)
    g2 = W["rwkv_g2"]
    head_ones = np.kron(np.eye(8, dtype=np.float32), np.ones((64, 64), np.float32))
    seg64 = jnp.asarray(np.concatenate([head_ones, head_ones], axis=0), dtype=BF16)

    x2d = x.reshape(N, D_MODEL)
    tgt = loss_target.reshape(N, D_MODEL)

    (h1,) = _rowwise("norm1", _fn_norm, [(x2d, 1024, 0)], [R["norm1_g"]], [], [(1024, BF16)], N, tm)
    p = _matmul(h1, w_in_b, "proj_in")
    o_gla, gla_states = _gla_fwd(p, wa2, ba, B, T)
    gla_post_rows = [(o_gla, 512, 0, 0), (o_gla, 512, 0, 1), (p, 512, 2)]
    (gated,) = _rowwise("gla_post", _fn_gla_post, gla_post_rows, [R["gla_norm_g"]], [], [(512, BF16)], N, tm)
    y_a = _matmul(gated, gla_proj_b, "gla_out")
    s = _token_shift(p, R["rwkv_mu_prev"], R["rwkv_mu_next"], B, T)
    pre_rows = [(s, 512, 1), (s, 256, 6)]
    pre_params = [R["rwkv_w0_f"], R["rwkv_w0_b"], R["rwkv_a0"], R["rwkv_k_k"], R["rwkv_k_a"], w2_f, w2_b, a2, g2]
    wf, wb, k2, na, bb, g = _rowwise("rwkv_pre", _fn_rwkv_pre, pre_rows, pre_params, [seg64],
                                     [(512, F32)] * 6, N, tm)
    y_f, y_b, ck_f, ck_b = _scan_fwd_mxu(s, wf, wb, k2, na, bb, B, T)
    post_rows = [(y_f, 512, 0), (y_b, 512, 0), (s, 512, 0), (k2, 512, 0), (s, 512, 2), (g, 512, 0)]
    post_params = [R["rwkv_ln_w"], R["rwkv_ln_b"], R["rwkv_r_k"]]
    (o_rwkv,) = _rowwise("rwkv_post", _fn_rwkv_post, post_rows, post_params, [seg64], [(512, BF16)], N, tm)
    y_r = _matmul(o_rwkv, rwkv_proj_b, "rwkv_out")
    merge_rows = [(p, 1024, 3), (p, 1024, 4), (y_a, 1024, 0), (y_r, 1024, 0)]
    (merged,) = _rowwise("merge", _fn_merge, merge_rows, [], [], [(1024, BF16)], N, tm)
    x1 = _matmul(merged, w_out_b, "mix_out", residual=x2d)
    (h2,) = _rowwise("norm2", _fn_norm, [(x1, 1024, 0)], [R["norm2_g"]], [], [(1024, BF16)], N, tm)
    u = _matmul(h2, ffn_up_b, "ffn_up")
    act = _ffn_conv(u, conv_w_p, conv_b_p, B, T)
    x2 = _matmul(act, ffn_down_b, "ffn_down", residual=x1)
    loss_blk, dx2, d_norm_f = _loss_head(x2, tgt, weights["norm_f_g"].reshape(1, -1))

    dx2_b = dx2.astype(BF16)
    d_act = _matmul(dx2_b, ffn_down_p.T.astype(BF16), "d_act")
    d_ffn_down = _matmul_tn(act, dx2_b, "dw_ffn_down")[:D_FF]
    du_g, du_v, d_conv_w_p, d_conv_b_p = _ffn_conv_bwd(u, d_act, conv_w_p, conv_b_p, B, T)
    up_t = ffn_up_p.T.astype(BF16)
    d_h2 = _matmul(du_v, up_t[FF_PAD:], "d_h2_v", residual=_matmul(du_g, up_t[:FF_PAD], "d_h2_g"))
    d_ffn_up = jnp.concatenate([_matmul_tn(h2, du_g, "dw_ffn_up_g")[:, :D_FF],
                                _matmul_tn(h2, du_v, "dw_ffn_up_v")[:, :D_FF]], axis=1)
    unpad_ff = lambda a: jnp.concatenate([a[:, :D_FF], a[:, FF_PAD:FF_PAD + D_FF]], axis=1)
    (dx1,), (d_norm2,) = _rowwise_bwd("norm2_bwd", _fn_norm, [(x1, 1024, 0)], [R["norm2_g"]], [],
                                      [[(d_h2, 1024, 0)]], [(F32, (dx2, 1024, 0))], N, tm)
    dx1_b = dx1.astype(BF16)
    d_merged = _matmul(dx1_b, W["w_out"].T.astype(BF16), "d_merged")
    d_w_out = _matmul_tn(merged, dx1_b, "dw_out")
    (d_ga, d_gb, d_ya, d_yr), _ = _rowwise_bwd("merge_bwd", _fn_merge, merge_rows, [], [],
                                               [[(d_merged, 1024, 0)]], [(BF16, None)] * 4, N, tm)
    d_o_rwkv = _matmul(d_yr, W["rwkv_proj"].T.astype(BF16), "d_o_rwkv")
    d_rwkv_proj = _matmul_tn(o_rwkv, d_yr, "dw_rwkv_proj")
    (d_y, d_r_bonus, d_k2_bonus, d_v_bonus, d_g), (d_ln_w, d_ln_b, d_r_k) = _rowwise_bwd(
        "rwkv_post_bwd", _fn_rwkv_post, post_rows, post_params, [seg64], [[(d_o_rwkv, 512, 0)]],
        [(F32, None), None, (F32, None), (F32, None), (F32, None), (F32, None)], N, tm)
    (drF, dvF, dkF, daF, dbF, dwF, drB, dvB, dkB, daB, dbB, dwB) = _scan_bwd_mxu(s, wf, wb, k2, na, bb, d_y, ck_f, ck_b, B, T)
    pre_cts = [[(dwF, 512, 0)], [(dwB, 512, 0)], [(dkF, 512, 0), (dkB, 512, 0), (d_k2_bonus, 512, 0)],
               [(daF, 512, 0), (daB, 512, 0)], [(dbF, 512, 0), (dbB, 512, 0)], [(d_g, 512, 0)]]
    (ds_k, ds_wag), pre_grads = _rowwise_bwd("rwkv_pre_bwd", _fn_rwkv_pre, pre_rows, pre_params, [seg64], pre_cts,
                                             [(F32, None), (F32, None)], N, tm)
    d_w0_f, d_w0_b, d_a0, d_k_k, d_k_a, d_w2_f, d_w2_b, d_a2, d_g2 = pre_grads
    ds = jnp.concatenate([drF + drB + d_r_bonus, ds_k, dvF + dvB + d_v_bonus, ds_wag], axis=1)
    d_p_rwkv, d_mu_prev, d_mu_next = _token_shift_bwd(p, ds, R["rwkv_mu_prev"], R["rwkv_mu_next"], B, T)
    d_gated = _matmul(d_ya, W["gla_proj"].T.astype(BF16), "d_gated")
    d_gla_proj = _matmul_tn(gated, d_ya, "dw_gla_proj")
    (d_o, d_og), (d_gla_norm,) = _rowwise_bwd(
        "gla_post_bwd", _fn_gla_post, gla_post_rows, [R["gla_norm_g"]], [], [[(d_gated, 512, 0)]],
        [(F32, None), None, (BF16, None)], N, tm)
    dqkv2, dafab2, d_wa2, d_ba = _gla_bwd(p, d_o, gla_states, wa2, ba, B, T)
    add2 = lambda a, b: (a + b,)
    (d_qkv,) = _rowwise("sum_dqkv", add2, [(dqkv2, 1024, 0, 0), (dqkv2, 1024, 0, 1)], [], [], [(1024, BF16)], N, tm)
    (d_afab,) = _rowwise("sum_dafab", add2, [(dafab2, 128, 0, 0), (dafab2, 128, 0, 1)], [], [], [(128, BF16)], N, tm)
    w_in_t = w_in_p.T.astype(BF16)
    w_rwkv_t = jnp.concatenate([w_in_t[1536:3072], w_in_t[5120:5376]], axis=0)
    d_h1 = _matmul(d_qkv, w_in_t[0:1024], "d_h1_qkv")
    d_h1 = _matmul(d_og, w_in_t[1024:1536], "d_h1_og", residual=d_h1)
    d_h1 = _matmul(d_p_rwkv, w_rwkv_t, "d_h1_rwkv", residual=d_h1)
    d_h1 = _matmul(d_ga, w_in_t[3072:4096], "d_h1_ga", residual=d_h1)
    d_h1 = _matmul(d_gb, w_in_t[4096:5120], "d_h1_gb", residual=d_h1)
    d_h1 = _matmul(d_afab, w_in_t[5376:5504], "d_h1_afab", residual=d_h1)
    d_w_in = jnp.concatenate([
        _matmul_tn(h1, d_qkv, "dw_in_qkv"), _matmul_tn(h1, d_og, "dw_in_og"),
        _matmul_tn(h1, d_afab, "dw_in_afab")[:, :32], _matmul_tn(h1, d_p_rwkv, "dw_in_rwkv"),
        _matmul_tn(h1, d_ga, "dw_in_ga"), _matmul_tn(h1, d_gb, "dw_in_gb")], axis=1)
    (grad_x,), (d_norm1,) = _rowwise_bwd("norm1_bwd", _fn_norm, [(x2d, 1024, 0)], [R["norm1_g"]], [],
                                         [[(d_h1, 1024, 0)]], [(F32, (dx1, 1024, 0))], N, tm)

    full_grads = {
        "w_in": d_w_in, "gla_wa2_f": d_wa2[0, 0:16], "gla_wa2_b": d_wa2[1, 16:32], "gla_proj": d_gla_proj,
        "rwkv_w2_f": d_w2_f[0:64], "rwkv_w2_b": d_w2_b[0:64], "rwkv_a2": d_a2[64:128], "rwkv_g2": d_g2,
        "rwkv_proj": d_rwkv_proj, "w_out": d_w_out, "ffn_up": d_ffn_up, "ffn_conv_w": unpad_ff(d_conv_w_p),
        "ffn_down": d_ffn_down,
    }
    repl_grads = {
        "norm1_g": d_norm1, "gla_ba_f": d_ba[0], "gla_ba_b": d_ba[1], "gla_norm_g": d_gla_norm,
        "rwkv_mu_prev": d_mu_prev, "rwkv_mu_next": d_mu_next, "rwkv_w0_f": d_w0_f, "rwkv_w0_b": d_w0_b,
        "rwkv_a0": d_a0, "rwkv_k_k": d_k_k, "rwkv_k_a": d_k_a, "rwkv_r_k": d_r_k, "rwkv_ln_w": d_ln_w,
        "rwkv_ln_b": d_ln_b, "norm2_g": d_norm2, "ffn_conv_b": unpad_ff(d_conv_b_p), "norm_f_g": d_norm_f,
    }
    split = {n: _split_shards(full_grads[n], ax) for n, _, ax in SHARDED}
    to_owners = [jnp.stack([p.astype(BF16) for p in split[n]]) for n, _, _ in BIG]
    repl_flat = _pack_replicated(repl_grads, loss_blk[0, 0])
    to_owners.append(jnp.stack([_pack({n: split[n][sidx] for n, _, _ in SMALL}, repl_flat) for sidx in range(4)]))
    received = _exchange_chips(to_owners, "scatter_grads", gather=False)
    names = [n for n, _, _ in BIG] + ["small"]
    mine = [_sum_sources(r, "sum_" + n) for r, n in zip(received, names)]
    other = _swap_with_sibling(mine)
    packs = [w_small, small_of(m_loc), small_of(v_loc)]
    results = {}
    for i, n in enumerate(names):
        wmv = packs if n == "small" else [d[0][n] for d in (w_loc, m_loc, v_loc)]
        results[n] = _adamw(mine[i], other[i], *wmv, "adamw_" + n)
    small = [_unpack(f) for f in results["small"]]
    outs = [small[0]["loss"], grad_x.reshape(B, T, D_MODEL)]
    for kind in range(4):
        for n in WEIGHT_ORDER:
            val = results[n][kind] if n in results else small[kind][n]
            outs.append(val.reshape(shapes[n]))
    return tuple(outs)
```

```python
import functools

import jax
import jax.numpy as jnp
import numpy as np
from jax import lax
from jax.experimental import pallas as pl
from jax.experimental.pallas import tpu as pltpu

F32 = jnp.float32
BF16 = jnp.bfloat16
HIGHEST = lax.Precision.HIGHEST
MESH_IDS = pl.DeviceIdType.MESH

D_MODEL = 1024
N_PROJ = 5408
PROJ_PAD = 5632
D_FF = 2752
FF_PAD = 2816
GLA_CHUNK = 64
SCAN_CHUNK = 16
NORM_EPS = 1e-6
HEAD_NORM_EPS = 1e-5
RWKV_GN_EPS = 64 * 1e-5
ADAM_LR, ADAM_B1, ADAM_B2, ADAM_EPS, ADAM_WD, ADAM_STEP = 0.001, 0.9, 0.999, 1e-08, 0.01, 10
VMEM_LIMIT = 56 * 1024 * 1024

FLAT_ROWS, FLAT_COLS = 128, 1024
BIG = (
    ("w_in", (1024, 1352), 1), ("gla_proj", (512, 256), 1), ("rwkv_proj", (512, 256), 1),
    ("w_out", (256, 1024), 0), ("ffn_up", (1024, 1376), 1), ("ffn_down", (688, 1024), 0),
)
SMALL = (
    ("gla_wa2_f", (16, 64), 1), ("gla_wa2_b", (16, 64), 1), ("rwkv_w2_f", (64, 128), 1),
    ("rwkv_w2_b", (64, 128), 1), ("rwkv_a2", (64, 128), 1), ("rwkv_g2", (128, 128), 1),
    ("ffn_conv_w", (3, 1376), 1),
)
SHARDED = BIG + SMALL
REPLICATED = (
    ("norm1_g", 1024), ("gla_ba_f", 256), ("gla_ba_b", 256), ("gla_norm_g", 512),
    ("rwkv_mu_prev", 1792), ("rwkv_mu_next", 1792), ("rwkv_w0_f", 512), ("rwkv_w0_b", 512),
    ("rwkv_a0", 512), ("rwkv_k_k", 512), ("rwkv_k_a", 512), ("rwkv_r_k", 512),
    ("rwkv_ln_w", 512), ("rwkv_ln_b", 512), ("norm2_g", 1024), ("ffn_conv_b", 5504),
    ("norm_f_g", 1024),
)
WEIGHT_ORDER = ("norm1_g", "w_in", "gla_wa2_f", "gla_ba_f", "gla_wa2_b", "gla_ba_b", "gla_norm_g", "gla_proj",
                "rwkv_mu_prev", "rwkv_mu_next", "rwkv_w0_f", "rwkv_w2_f", "rwkv_w0_b", "rwkv_w2_b", "rwkv_a0",
                "rwkv_a2", "rwkv_g2", "rwkv_k_k", "rwkv_k_a", "rwkv_r_k", "rwkv_ln_w", "rwkv_ln_b", "rwkv_proj",
                "w_out", "norm2_g", "ffn_up", "ffn_conv_w", "ffn_conv_b", "ffn_down", "norm_f_g")


def _cparams(**kw):
    return pltpu.CompilerParams(vmem_limit_bytes=VMEM_LIMIT, **kw)


def _pack_replicated(repl_vals, loss=None):
    parts = [repl_vals[n].reshape(-1) for n, _ in REPLICATED]
    parts.append(jnp.zeros((1,), F32) if loss is None else loss.reshape(1))
    return jnp.concatenate(parts)


def _pack(sharded_vals, repl_flat):
    parts = [sharded_vals[n].reshape(-1) for n, _, _ in SMALL] + [repl_flat]
    used = sum(int(np.prod(s)) for _, s, _ in SMALL) + sum(w for _, w in REPLICATED) + 1
    parts.append(jnp.zeros((FLAT_ROWS * FLAT_COLS - used,), F32))
    return jnp.concatenate(parts).reshape(FLAT_ROWS, FLAT_COLS)


def _unpack(flat):
    v = flat.reshape(-1)
    out, off = {}, 0
    for n, s, _ in SMALL:
        k = int(np.prod(s))
        out[n] = v[off:off + k].reshape(s)
        off += k
    for n, w in REPLICATED:
        out[n] = v[off:off + w]
        off += w
    out["loss"] = v[off]
    return out


def _chip_peers():
    x, y, c = lax.axis_index("x"), lax.axis_index("y"), lax.axis_index("c")
    return x, y, c, ((1 - x, y), (x, 1 - y), (1 - x, 1 - y))


def _exchange_chips(arrs, name, gather):
    n = len(arrs)

    def body(*refs):
        srcs, outs = refs[:n], refs[n:2 * n]
        send_sems, recv_sems, local_sems = refs[2 * n:]
        x, y, c, peers = _chip_peers()
        me = 2 * x + y
        own = []
        for i in range(n):
            cp = pltpu.make_async_copy(srcs[i] if gather else srcs[i].at[me], outs[i].at[me], local_sems.at[i])
            cp.start()
            own.append(cp)
        sends = []
        for k, (px, py) in enumerate(peers):
            for i in range(n):
                cp = pltpu.make_async_remote_copy(
                    src_ref=srcs[i] if gather else srcs[i].at[2 * px + py], dst_ref=outs[i].at[me],
                    send_sem=send_sems.at[3 * i + k], recv_sem=recv_sems.at[3 * i + k],
                    device_id=(px, py, c), device_id_type=MESH_IDS)
                cp.start()
                sends.append(cp)
        for k, (px, py) in enumerate(peers):
            for i in range(n):
                pltpu.make_async_remote_copy(
                    src_ref=srcs[i] if gather else srcs[i].at[me], dst_ref=outs[i].at[2 * px + py],
                    send_sem=send_sems.at[3 * i + k], recv_sem=recv_sems.at[3 * i + k],
                    device_id=(px, py, c), device_id_type=MESH_IDS).wait_recv()
        for cp in sends:
            cp.wait_send()
        for cp in own:
            cp.wait()

    out_shape = [jax.ShapeDtypeStruct(((4,) + a.shape) if gather else a.shape, a.dtype) for a in arrs]
    return pl.pallas_call(
        body, name=name, out_shape=out_shape,
        in_specs=[pl.BlockSpec(memory_space=pl.ANY)] * n,
        out_specs=[pl.BlockSpec(memory_space=pl.ANY)] * n,
        scratch_shapes=[pltpu.SemaphoreType.DMA((3 * n,)), pltpu.SemaphoreType.DMA((3 * n,)),
                        pltpu.SemaphoreType.DMA((n,))],
    )(*arrs)


def _swap_with_sibling(arrs):
    n = len(arrs)

    def body(*refs):
        srcs, outs = refs[:n], refs[n:2 * n]
        send_sems, recv_sems = refs[2 * n:]
        x, y, c = lax.axis_index("x"), lax.axis_index("y"), lax.axis_index("c")
        cps = [pltpu.make_async_remote_copy(src_ref=srcs[i], dst_ref=outs[i], send_sem=send_sems.at[i],
                                            recv_sem=recv_sems.at[i], device_id=(x, y, 1 - c),
                                            device_id_type=MESH_IDS) for i in range(n)]
        for cp in cps:
            cp.start()
        for cp in cps:
            cp.wait()

    return pl.pallas_call(
        body, name="swap_sibling",
        out_shape=[jax.ShapeDtypeStruct(a.shape, a.dtype) for a in arrs],
        in_specs=[pl.BlockSpec(memory_space=pl.ANY)] * n,
        out_specs=[pl.BlockSpec(memory_space=pl.ANY)] * n,
        scratch_shapes=[pltpu.SemaphoreType.DMA((n,)), pltpu.SemaphoreType.DMA((n,))],
    )(*arrs)


def _row_tile(rows, cols):
    cap = max(8, (3 << 19) // (4 * (-(-cols // 128) * 128)))
    best = None
    for t in range(8, min(rows, cap) + 1, 8):
        if rows % t == 0:
            best = t
    return best or rows


def _sum_sources(r4, name):
    _, A, Bc = r4.shape
    ta = _row_tile(A, Bc)

    def body(r_ref, o_ref):
        f = lambda s: r_ref[s].astype(F32)
        o_ref[...] = ((f(0) + f(1)) + f(2)) + f(3)

    return pl.pallas_call(
        body, name=name, grid=(A // ta,),
        out_shape=jax.ShapeDtypeStruct((A, Bc), F32),
        in_specs=[pl.BlockSpec((4, ta, Bc), lambda i: (0, i, 0))],
        out_specs=pl.BlockSpec((ta, Bc), lambda i: (i, 0)),
        compiler_params=_cparams(),
    )(r4)


def _adamw(own, other, w, m, v, name):
    R, C = own.shape
    tr = _row_tile(R, C)

    def body(a_ref, b_ref, w_ref, m_ref, v_ref, g_out, d_out, m_out, v_out):
        g = a_ref[...] + b_ref[...]
        m_new = ADAM_B1 * m_ref[...] + (1.0 - ADAM_B1) * g
        v_new = ADAM_B2 * v_ref[...] + (1.0 - ADAM_B2) * (g * g)
        m_hat = m_new / (1.0 - ADAM_B1 ** ADAM_STEP)
        v_hat = v_new / (1.0 - ADAM_B2 ** ADAM_STEP)
        g_out[...] = g
        d_out[...] = -ADAM_LR * (m_hat / (jnp.sqrt(v_hat) + ADAM_EPS) + ADAM_WD * w_ref[...])
        m_out[...] = m_new
        v_out[...] = v_new

    spec = pl.BlockSpec((tr, C), lambda i: (i, 0))
    return pl.pallas_call(
        body, name=name, grid=(R // tr,),
        out_shape=[jax.ShapeDtypeStruct((R, C), F32)] * 4,
        in_specs=[spec] * 5, out_specs=[spec] * 4,
        compiler_params=_cparams(),
    )(own, other, w, m, v)


def _pick(n, options):
    for o in options:
        if n % o == 0:
            return o
    return n


def _div128(n, cap):
    best = None
    for t in range(128, min(n, cap) + 1, 128):
        if n % t == 0:
            best = t
    return best or n


MATMUL_VMEM = 40 * 1024 * 1024


def _matmul(a, b, name, out_dtype=F32, residual=None):
    M, K = a.shape
    _, N = b.shape
    tm, tn = _pick(M, (1024, 512)), _div128(N, 1408)
    while tm > 256 and 2 * (2 * tm * K + 2 * K * tn + (8 if residual is not None else 4) * tm * tn) > MATMUL_VMEM:
        tm //= 2

    def body(*refs):
        a_ref, b_ref = refs[0], refs[1]
        o_ref = refs[-1]
        acc = jnp.dot(a_ref[...], b_ref[...], preferred_element_type=F32)
        if residual is not None:
            acc = acc + refs[2][...]
        o_ref[...] = acc.astype(out_dtype)

    in_specs = [pl.BlockSpec((tm, K), lambda j, i: (i, 0)), pl.BlockSpec((K, tn), lambda j, i: (0, j))]
    args = [a, b]
    if residual is not None:
        in_specs.append(pl.BlockSpec((tm, tn), lambda j, i: (i, j)))
        args.append(residual)
    return pl.pallas_call(
        body, name=name, grid=(N // tn, M // tm),
        out_shape=jax.ShapeDtypeStruct((M, N), out_dtype),
        in_specs=in_specs, out_specs=pl.BlockSpec((tm, tn), lambda j, i: (i, j)),
        compiler_params=_cparams(),
    )(*args)


def _matmul_tn(a, b, name):
    R, M = a.shape
    _, N = b.shape
    tr, tm, tn = _pick(R, (2048, 1024, 512)), _div128(M, 1408), _div128(N, 1408)
    while tr > 512 and 2 * (2 * tr * tm + 2 * tr * tn + 4 * tm * tn) > MATMUL_VMEM:
        tr //= 2

    def body(a_ref, b_ref, o_ref):
        @pl.when(pl.program_id(2) == 0)
        def _():
            o_ref[...] = jnp.zeros_like(o_ref)

        o_ref[...] += lax.dot_general(a_ref[...], b_ref[...], (((0,), (0,)), ((), ())),
                                      preferred_element_type=F32)

    return pl.pallas_call(
        body, name=name, grid=(M // tm, N // tn, R // tr),
        out_shape=jax.ShapeDtypeStruct((M, N), F32),
        in_specs=[pl.BlockSpec((tr, tm), lambda i, j, r: (r, i)), pl.BlockSpec((tr, tn), lambda i, j, r: (r, j))],
        out_specs=pl.BlockSpec((tm, tn), lambda i, j, r: (i, j)),
        compiler_params=_cparams(),
    )(a, b)


def _row_spec(tm, width, col, lead=None):
    if lead is None:
        return pl.BlockSpec((tm, width), lambda i: (i, col))
    return pl.BlockSpec((None, tm, width), lambda i: (lead, i, col))


def _whole_spec(arr):
    nd = arr.ndim
    return pl.BlockSpec(arr.shape, lambda i: (0,) * nd)


def _rowwise(name, fn, rows, params, consts, outs, n_rows, tm):
    nr, npar, nc = len(rows), len(params), len(consts)

    def body(*refs):
        vals = [r[...].astype(F32) for r in refs[:nr]] + [r[...] for r in refs[nr:nr + npar + nc]]
        res = fn(*vals)
        for o_ref, r in zip(refs[nr + npar + nc:], res):
            o_ref[...] = r.astype(o_ref.dtype)

    return pl.pallas_call(
        body, name=name, grid=(n_rows // tm,),
        out_shape=[jax.ShapeDtypeStruct((n_rows, w), dt) for w, dt in outs],
        in_specs=[_row_spec(tm, *r[1:]) for r in rows] + [_whole_spec(p) for p in params + consts],
        out_specs=[_row_spec(tm, w, 0) for w, _ in outs],
        compiler_params=_cparams(),
    )(*[r[0] for r in rows], *params, *consts)


def _rowwise_bwd(name, fn, rows, params, consts, cts, row_grads, n_rows, tm):
    nr, npar, nc = len(rows), len(params), len(consts)
    ct_flat = [p for pieces in cts for p in pieces]
    res_flat = [rg[1] for rg in row_grads if rg is not None and rg[1] is not None]
    n_ct, n_res = len(ct_flat), len(res_flat)
    n_in = nr + npar + nc + n_ct + n_res
    wanted = [k for k, rg in enumerate(row_grads) if rg is not None]

    def body(*refs):
        row_vals = [r[...].astype(F32) for r in refs[:nr]]
        par_vals = [r[...] for r in refs[nr:nr + npar]]
        const_vals = [r[...] for r in refs[nr + npar:nr + npar + nc]]
        ct_refs = refs[nr + npar + nc:nr + npar + nc + n_ct]
        res_refs = refs[nr + npar + nc + n_ct:n_in]
        out_refs = refs[n_in:]
        ct_vals, pos = [], 0
        for pieces in cts:
            acc = ct_refs[pos][...].astype(F32)
            for q in range(1, len(pieces)):
                acc = acc + ct_refs[pos + q][...].astype(F32)
            pos += len(pieces)
            ct_vals.append(acc)
        _, vjp = jax.vjp(lambda *a: tuple(fn(*a, *const_vals)), *row_vals, *par_vals)
        grads = vjp(tuple(ct_vals))
        ri = 0
        for slot, k in enumerate(wanted):
            g = grads[k]
            if row_grads[k][1] is not None:
                g = g + res_refs[ri][...].astype(F32)
                ri += 1
            out_refs[slot][...] = g.astype(out_refs[slot].dtype)

        @pl.when(pl.program_id(0) == 0)
        def _():
            for q in range(npar):
                out_refs[len(wanted) + q][...] = jnp.zeros_like(out_refs[len(wanted) + q])

        for q in range(npar):
            out_refs[len(wanted) + q][...] += grads[nr + q]

    out_shape = [jax.ShapeDtypeStruct((n_rows, rows[k][1]), row_grads[k][0]) for k in wanted]
    out_shape += [jax.ShapeDtypeStruct(p.shape, F32) for p in params]
    out_specs = [_row_spec(tm, rows[k][1], 0) for k in wanted] + [_whole_spec(p) for p in params]
    in_specs = [_row_spec(tm, *r[1:]) for r in rows] + [_whole_spec(p) for p in params + consts]
    in_specs += [_row_spec(tm, *r[1:]) for r in ct_flat + res_flat]
    res = pl.pallas_call(
        body, name=name, grid=(n_rows // tm,),
        out_shape=out_shape, in_specs=in_specs, out_specs=out_specs,
        compiler_params=_cparams(),
    )(*[r[0] for r in rows], *params, *consts, *[r[0] for r in ct_flat + res_flat])
    return res[:len(wanted)], res[len(wanted):]


def _sigmoid(x):
    return 0.5 * jnp.tanh(0.5 * x) + 0.5


def _softplus(x):
    return jnp.maximum(x, 0.0) + jnp.log(1.0 + jnp.exp(-jnp.abs(x)))


def _seg_dot_impl(x, seg2):
    hi = x.astype(BF16)
    lo = (x - hi.astype(F32)).astype(BF16)
    return jnp.dot(jnp.concatenate([hi, lo], axis=1), seg2, preferred_element_type=F32)


@jax.custom_vjp
def _seg_dot(x, seg2):
    return _seg_dot_impl(x, seg2)


_seg_dot.defvjp(lambda x, seg2: (_seg_dot_impl(x, seg2), seg2),
                lambda seg2, ct: (_seg_dot_impl(ct, seg2), jnp.zeros_like(seg2)))


def _fn_norm(x, g):
    r = lax.rsqrt(jnp.mean(x * x, axis=-1, keepdims=True) + NORM_EPS)
    return ((x * r) * g,)


def _fn_gla_post(o_f, o_b, og, norm_g):
    o = o_f + o_b
    heads = []
    for h in range(4):
        oh = o[:, h * 128:(h + 1) * 128]
        heads.append(oh * lax.rsqrt(jnp.mean(oh * oh, axis=-1, keepdims=True) + HEAD_NORM_EPS))
    on = jnp.concatenate(heads, axis=1) * norm_g
    return (on * (og * _sigmoid(og)),)


def _fn_rwkv_pre(s_k, s_wag, w0_f, w0_b, a0, k_k, k_a, w2_f, w2_b, a2, g2, seg64):
    wa = s_wag[:, 0:128]
    gl = s_wag[:, 128:256]
    tw = jnp.tanh(wa)
    z_f = w0_f + jnp.dot(tw, w2_f, preferred_element_type=F32)
    z_b = w0_b + jnp.dot(tw, w2_b, preferred_element_type=F32)
    w_f = jnp.exp(-jnp.exp(-_softplus(-z_f) - 0.5))
    w_b = jnp.exp(-jnp.exp(-_softplus(-z_b) - 0.5))
    a = _sigmoid(a0 + jnp.dot(wa, a2, preferred_element_type=F32))
    g = jnp.dot(_sigmoid(gl), g2, preferred_element_type=F32)
    kk = s_k * k_k
    kkn = kk / jnp.maximum(jnp.sqrt(_seg_dot(kk * kk, seg64)), 1e-12)
    k2 = s_k * (1.0 + (a - 1.0) * k_a)
    return w_f, w_b, k2, -kkn, kkn * a, g


def _fn_rwkv_post(y_f, y_b, s_r, k2, s_v, g, ln_w, ln_b, r_k, seg64):
    y = y_f + y_b
    mu = _seg_dot(y, seg64) * (1.0 / 64.0)
    yc = y - mu
    var = _seg_dot(yc * yc, seg64) * (1.0 / 64.0)
    yn = yc * lax.rsqrt(var + RWKV_GN_EPS) * ln_w + ln_b
    bonus = _seg_dot(s_r * k2 * r_k, seg64) * s_v
    return ((yn + bonus) * g,)


def _fn_merge(ga, gb, y_a, y_b):
    return (_sigmoid(ga) * y_a + _sigmoid(gb) * y_b,)


def _loss_head(x2, target, gf):
    N, Dm = x2.shape
    tm = _pick(N, (512,))

    def fn(x, g, t):
        r = lax.rsqrt(jnp.mean(x * x, axis=-1, keepdims=True) + NORM_EPS)
        err = (x * r) * g - t
        return 0.5 * jnp.sum(jnp.mean(err * err, axis=-1, keepdims=True), axis=0, keepdims=True)

    def body(x_ref, t_ref, g_ref, loss_ref, dx_ref, dxb_ref, dg_ref):
        t = t_ref[...]
        loss, vjp = jax.vjp(lambda x, g: fn(x, g, t), x_ref[...], g_ref[...])
        dx, dg = vjp(jnp.ones((1, 1), F32))

        @pl.when(pl.program_id(0) == 0)
        def _():
            loss_ref[...] = jnp.zeros_like(loss_ref)
            dg_ref[...] = jnp.zeros_like(dg_ref)

        loss_ref[...] += jnp.broadcast_to(loss, loss_ref.shape)
        dg_ref[...] += dg
        dx_ref[...] = dx
        dxb_ref[...] = dx.astype(BF16)

    return pl.pallas_call(
        body, name="loss_head", grid=(N // tm,),
        out_shape=[jax.ShapeDtypeStruct((8, 128), F32), jax.ShapeDtypeStruct((N, Dm), F32),
                   jax.ShapeDtypeStruct((N, Dm), BF16), jax.ShapeDtypeStruct((1, Dm), F32)],
        in_specs=[_row_spec(tm, Dm, 0), _row_spec(tm, Dm, 0), _whole_spec(gf)],
        out_specs=[pl.BlockSpec((8, 128), lambda i: (0, 0)), _row_spec(tm, Dm, 0), _row_spec(tm, Dm, 0),
                   pl.BlockSpec((1, Dm), lambda i: (0, 0))],
        compiler_params=_cparams(),
    )(x2, target, gf)


def _shift_prev(u):
    rolled = pltpu.roll(u, 1, axis=0)
    row = lax.broadcasted_iota(jnp.int32, u.shape, 0)
    return jnp.where(row == 0, 0.0, rolled)


def _shift_next(u):
    T = u.shape[0]
    rolled = pltpu.roll(u, T - 1, axis=0)
    row = lax.broadcasted_iota(jnp.int32, u.shape, 0)
    return jnp.where(row == T - 1, 0.0, rolled)


_SHIFT_BLOCKS = 7


def _shift_src_col(j):
    return jnp.where(j < 6, 6 + j, 20)


def _token_shift(p, mu_prev, mu_next, B, T):
    def body(p_ref, mp_ref, mn_ref, s_ref):
        u = p_ref[...]
        s_ref[...] = u + mp_ref[...] * (_shift_prev(u) - u) + mn_ref[...] * (_shift_next(u) - u)

    return pl.pallas_call(
        body, name="token_shift", grid=(B, _SHIFT_BLOCKS),
        out_shape=jax.ShapeDtypeStruct((B * T, 1792), F32),
        in_specs=[pl.BlockSpec((T, 256), lambda b, j: (b, _shift_src_col(j))),
                  pl.BlockSpec((1, 256), lambda b, j: (0, j)), pl.BlockSpec((1, 256), lambda b, j: (0, j))],
        out_specs=pl.BlockSpec((T, 256), lambda b, j: (b, j)),
        compiler_params=_cparams(),
    )(p, mu_prev, mu_next)


def _token_shift_bwd(p, ds, mu_prev, mu_next, B, T):
    def body(p_ref, ds_ref, mp_ref, mn_ref, dp_ref, dmp_ref, dmn_ref):
        u, d = p_ref[...], ds_ref[...]
        mp, mn = mp_ref[...], mn_ref[...]
        dp = d * (1.0 - mp - mn) + _shift_next(d * mp) + _shift_prev(d * mn)
        dp_ref[...] = dp.astype(dp_ref.dtype)

        @pl.when(pl.program_id(1) == 0)
        def _():
            dmp_ref[...] = jnp.zeros_like(dmp_ref)
            dmn_ref[...] = jnp.zeros_like(dmn_ref)

        dmp_ref[...] += jnp.sum(d * (_shift_prev(u) - u), axis=0, keepdims=True)
        dmn_ref[...] += jnp.sum(d * (_shift_next(u) - u), axis=0, keepdims=True)

    return pl.pallas_call(
        body, name="token_shift_bwd", grid=(_SHIFT_BLOCKS, B),
        out_shape=[jax.ShapeDtypeStruct((B * T, 1792), BF16), jax.ShapeDtypeStruct((1, 1792), F32),
                   jax.ShapeDtypeStruct((1, 1792), F32)],
        in_specs=[pl.BlockSpec((T, 256), lambda j, b: (b, _shift_src_col(j))),
                  pl.BlockSpec((T, 256), lambda j, b: (b, j)),
                  pl.BlockSpec((1, 256), lambda j, b: (0, j)), pl.BlockSpec((1, 256), lambda j, b: (0, j))],
        out_specs=[pl.BlockSpec((T, 256), lambda j, b: (b, j)), pl.BlockSpec((1, 256), lambda j, b: (0, j)),
                   pl.BlockSpec((1, 256), lambda j, b: (0, j))],
        compiler_params=_cparams(),
    )(p, ds, mu_prev, mu_next)


_FF_BLOCKS = FF_PAD // 256


def _conv3(u, cw, cb):
    return cw[0:1] * _shift_prev(u) + cw[1:2] * u + cw[2:3] * _shift_next(u) + cb


def _ffn_conv(u, cw, cb, B, T):
    def body(ug_ref, uv_ref, cwg_ref, cwv_ref, cbg_ref, cbv_ref, o_ref):
        cg = _conv3(ug_ref[...], cwg_ref[...], cbg_ref[...])
        cv = _conv3(uv_ref[...], cwv_ref[...], cbv_ref[...])
        o_ref[...] = (cg * _sigmoid(cg) * cv).astype(o_ref.dtype)

    nb = _FF_BLOCKS
    return pl.pallas_call(
        body, name="ffn_conv", grid=(B, nb),
        out_shape=jax.ShapeDtypeStruct((B * T, FF_PAD), BF16),
        in_specs=[pl.BlockSpec((T, 256), lambda b, j: (b, j)), pl.BlockSpec((T, 256), lambda b, j: (b, j + nb)),
                  pl.BlockSpec((3, 256), lambda b, j: (0, j)), pl.BlockSpec((3, 256), lambda b, j: (0, j + nb)),
                  pl.BlockSpec((1, 256), lambda b, j: (0, j)), pl.BlockSpec((1, 256), lambda b, j: (0, j + nb))],
        out_specs=pl.BlockSpec((T, 256), lambda b, j: (b, j)),
        compiler_params=_cparams(),
    )(u, u, cw, cw, cb, cb)


def _ffn_conv_bwd(u, dact, cw, cb, B, T):
    def half(u_, dc, cw_):
        du = _shift_next(cw_[0:1] * dc) + cw_[1:2] * dc + _shift_prev(cw_[2:3] * dc)
        dcw = jnp.concatenate([jnp.sum(dc * _shift_prev(u_), axis=0, keepdims=True),
                               jnp.sum(dc * u_, axis=0, keepdims=True),
                               jnp.sum(dc * _shift_next(u_), axis=0, keepdims=True)], axis=0)
        return du, dcw, jnp.sum(dc, axis=0, keepdims=True)

    def body(ug_ref, uv_ref, da_ref, cwg_ref, cwv_ref, cbg_ref, cbv_ref,
             dug_ref, duv_ref, dcwg_ref, dcwv_ref, dcbg_ref, dcbv_ref):
        ug, uv, da = ug_ref[...], uv_ref[...], da_ref[...]
        cwg, cwv = cwg_ref[...], cwv_ref[...]
        cg = _conv3(ug, cwg, cbg_ref[...])
        cv = _conv3(uv, cwv, cbv_ref[...])
        sg = _sigmoid(cg)
        dcv = da * (cg * sg)
        dcg = da * cv * (sg * (1.0 + cg * (1.0 - sg)))
        dug, dcwg, dcbg = half(ug, dcg, cwg)
        duv, dcwv, dcbv = half(uv, dcv, cwv)
        dug_ref[...] = dug.astype(dug_ref.dtype)
        duv_ref[...] = duv.astype(duv_ref.dtype)

        @pl.when(pl.program_id(1) == 0)
        def _():
            for r in (dcwg_ref, dcwv_ref, dcbg_ref, dcbv_ref):
                r[...] = jnp.zeros_like(r)

        dcwg_ref[...] += dcwg
        dcwv_ref[...] += dcwv
        dcbg_ref[...] += dcbg
        dcbv_ref[...] += dcbv

    nb = _FF_BLOCKS
    N = B * T
    res = pl.pallas_call(
        body, name="ffn_conv_bwd", grid=(nb, B),
        out_shape=[jax.ShapeDtypeStruct((N, FF_PAD), BF16), jax.ShapeDtypeStruct((N, FF_PAD), BF16),
                   jax.ShapeDtypeStruct((3, FF_PAD), F32), jax.ShapeDtypeStruct((3, FF_PAD), F32),
                   jax.ShapeDtypeStruct((1, FF_PAD), F32), jax.ShapeDtypeStruct((1, FF_PAD), F32)],
        in_specs=[pl.BlockSpec((T, 256), lambda j, b: (b, j)), pl.BlockSpec((T, 256), lambda j, b: (b, j + nb)),
                  pl.BlockSpec((T, 256), lambda j, b: (b, j)),
                  pl.BlockSpec((3, 256), lambda j, b: (0, j)), pl.BlockSpec((3, 256), lambda j, b: (0, j + nb)),
                  pl.BlockSpec((1, 256), lambda j, b: (0, j)), pl.BlockSpec((1, 256), lambda j, b: (0, j + nb))],
        out_specs=[pl.BlockSpec((T, 256), lambda j, b: (b, j)), pl.BlockSpec((T, 256), lambda j, b: (b, j)),
                   pl.BlockSpec((3, 256), lambda j, b: (0, j)), pl.BlockSpec((3, 256), lambda j, b: (0, j)),
                   pl.BlockSpec((1, 256), lambda j, b: (0, j)), pl.BlockSpec((1, 256), lambda j, b: (0, j))],
        compiler_params=_cparams(),
    )(u, u, dact, cw, cw, cb, cb)
    dug, duv, dcwg, dcwv, dcbg, dcbv = res
    return dug, duv, jnp.concatenate([dcwg, dcwv], axis=1), jnp.concatenate([dcbg, dcbv], axis=1)


def _gla_chunk(q, k, v, afab, wa2, ba, state, rev):
    C = GLA_CHUNK
    n = q.shape[0]
    z = jnp.dot(afab.reshape(n * C, 128), wa2, preferred_element_type=F32).reshape(n, C, 256) + ba
    la = (jnp.minimum(z, 0.0) - jnp.log(1.0 + jnp.exp(-jnp.abs(z)))) * (1.0 / 16.0)
    row = lax.broadcasted_iota(jnp.int32, (n * C, n * C), 0)
    col = lax.broadcasted_iota(jnp.int32, (n * C, n * C), 1)
    ordered = ((col & (C - 1)) - (row & (C - 1))) * (1 - 2 * rev) <= 0
    tri_all = ordered & ((row >> 6) == (col >> 6))
    b = jnp.dot(tri_all.astype(F32), la.reshape(n * C, 256), precision=HIGHEST,
                preferred_element_type=F32).reshape(n, C, 256)
    tri = (lax.broadcasted_iota(jnp.int32, (C, C), 1) - lax.broadcasted_iota(jnp.int32, (C, C), 0)) * (1 - 2 * rev) <= 0
    rows = lax.broadcasted_iota(jnp.int32, (n, C, 256), 1)
    ref_row = jnp.where(rev == 0, C // 2, C - 1 - C // 2)
    last_row = jnp.where(rev == 0, C - 1, 0)
    b_ref = jnp.sum(jnp.where(rows == ref_row, b, 0.0), axis=1, keepdims=True)
    b_last = jnp.sum(jnp.where(rows == last_row, b, 0.0), axis=1, keepdims=True)
    qs = q * 0.125
    qi = qs * jnp.exp(b - b_ref)
    ki = k * jnp.exp(b_ref - b)
    kd = k * jnp.exp(b_last - b)
    qe = qs * jnp.exp(b)
    lane = lax.broadcasted_iota(jnp.int32, (1, 1, 256), 2)
    bdot = lambda x, y, cx, cy: lax.dot_general(x, y, (((cx,), (cy,)), ((0,), (0,))), preferred_element_type=F32)
    outs = []
    upd = jnp.zeros_like(state)
    for h in range(4):
        mh = ((lane >= 64 * h) & (lane < 64 * (h + 1))).astype(F32)
        vh = v[:, :, 128 * h:128 * (h + 1)]
        a = jnp.where(tri, bdot(qi * mh, ki, 2, 2), 0.0)
        outs.append(bdot(a, vh, 2, 1) + bdot(qe, state * mh, 2, 2))
        upd = upd + bdot(vh, kd * mh, 1, 1)
    return jnp.concatenate(outs, axis=2), state * jnp.exp(b_last) + upd


def _gla_chunk_at(nC):
    return lambda d, j: j + d * (nC - 1 - 2 * j)


def _gla_fwd(p, wa2, ba, B, T):
    nC = T // GLA_CHUNK
    N = B * T
    at = _gla_chunk_at(nC)
    p3 = p.reshape(B, T, p.shape[-1])

    def body(q_ref, k_ref, v_ref, af_ref, wa_ref, ba_ref, o_ref, st_ref, state):
        @pl.when(pl.program_id(1) == 0)
        def _():
            state[...] = jnp.zeros_like(state)

        st_ref[0, :, 0] = state[...]
        o, new = _gla_chunk(q_ref[...], k_ref[...], v_ref[...], af_ref[...], wa_ref[0], ba_ref[0], state[...],
                            pl.program_id(0))
        o_ref[0] = o
        state[...] = new

    blk = lambda w, col: pl.BlockSpec((B, 64, w), lambda d, j: (0, at(d, j), col))
    o, st = pl.pallas_call(
        body, name="gla_fwd", grid=(2, nC),
        out_shape=[jax.ShapeDtypeStruct((2, B, T, 512), F32), jax.ShapeDtypeStruct((2, B, nC, 128, 256), F32)],
        in_specs=[blk(256, 0), blk(256, 1), blk(512, 1), blk(128, 42),
                  pl.BlockSpec((1, 128, 256), lambda d, j: (d, 0, 0)),
                  pl.BlockSpec((1, 1, 256), lambda d, j: (d, 0, 0))],
        out_specs=[pl.BlockSpec((1, B, 64, 512), lambda d, j: (d, 0, at(d, j), 0)),
                   pl.BlockSpec((1, B, 1, 128, 256), lambda d, j: (d, 0, at(d, j), 0, 0))],
        scratch_shapes=[pltpu.VMEM((B, 128, 256), F32)],
        compiler_params=_cparams(),
    )(p3, p3, p3, p3, wa2, ba)
    return o.reshape(2, N, 512), st


def _gla_bwd(p, do, states, wa2, ba, B, T):
    nC = T // GLA_CHUNK
    N = B * T
    at_f = _gla_chunk_at(nC)
    at = lambda d, j: at_f(d, nC - 1 - j)
    p3 = p.reshape(B, T, p.shape[-1])

    def body(q_ref, k_ref, v_ref, af_ref, wa_ref, ba_ref, do_ref, st_ref,
             dqkv_ref, daf_ref, dwa_ref, dba_ref, dstate):
        rev = pl.program_id(0)

        @pl.when(pl.program_id(1) == 0)
        def _():
            dstate[...] = jnp.zeros_like(dstate)
            dwa_ref[...] = jnp.zeros_like(dwa_ref)
            dba_ref[...] = jnp.zeros_like(dba_ref)

        f = lambda q, k, v, af, wa, bb, st: _gla_chunk(q, k, v, af, wa, bb, st, rev)
        _, vjp = jax.vjp(f, q_ref[...], k_ref[...], v_ref[...], af_ref[...], wa_ref[0], ba_ref[0], st_ref[0, :, 0])
        dq, dk, dv, daf, dwa, dba, dst = vjp((do_ref[...], dstate[...]))
        dqkv_ref[0] = jnp.concatenate([dq, dk, dv], axis=2)
        daf_ref[0] = daf
        dwa_ref[0] += dwa
        dba_ref[0] += dba
        dstate[...] = dst

    blk = lambda w, col: pl.BlockSpec((B, 64, w), lambda d, j: (0, at(d, j), col))
    out4 = lambda w: pl.BlockSpec((1, B, 64, w), lambda d, j: (d, 0, at(d, j), 0))
    dqkv, daf, dwa, dba = pl.pallas_call(
        body, name="gla_bwd", grid=(2, nC),
        out_shape=[jax.ShapeDtypeStruct((2, B, T, 1024), F32), jax.ShapeDtypeStruct((2, B, T, 128), F32),
                   jax.ShapeDtypeStruct((2, 128, 256), F32), jax.ShapeDtypeStruct((2, 1, 256), F32)],
        in_specs=[blk(256, 0), blk(256, 1), blk(512, 1), blk(128, 42),
                  pl.BlockSpec((1, 128, 256), lambda d, j: (d, 0, 0)),
                  pl.BlockSpec((1, 1, 256), lambda d, j: (d, 0, 0)),
                  blk(512, 0),
                  pl.BlockSpec((1, B, 1, 128, 256), lambda d, j: (d, 0, at(d, j), 0, 0))],
        out_specs=[out4(1024), out4(128),
                   pl.BlockSpec((1, 128, 256), lambda d, j: (d, 0, 0)),
                   pl.BlockSpec((1, 1, 256), lambda d, j: (d, 0, 0))],
        scratch_shapes=[pltpu.VMEM((B, 128, 256), F32)],
        compiler_params=_cparams(),
    )(p3, p3, p3, p3, wa2, ba, do.reshape(B, T, 512), states)
    return dqkv.reshape(2, N, 1024), daf.reshape(2, N, 128), dwa, dba


def _seg_ones():
    m = lax.broadcasted_iota(jnp.int32, (256, 128), 0)
    n = lax.broadcasted_iota(jnp.int32, (256, 128), 1)
    return (((m >> 6) & 1) == (n >> 6)).astype(BF16)


def _seg_mm(x, ones2):
    hi = x.astype(BF16)
    lo = (x - hi.astype(F32)).astype(BF16)
    return jnp.dot(jnp.concatenate([hi, lo], axis=1), ones2, preferred_element_type=F32)


def _diag_matrix():
    r = np.arange(2048)[:, None] % 64
    c = np.arange(128)[None, :] % 64
    return jnp.asarray((r == c).astype(np.float32))


def _cols8(tile, dg, ones2):
    return _seg_mm(jnp.concatenate([_rows4(tile, q) for q in range(8)], axis=0) * dg[...], ones2)


def _rows4(tile, q):
    return jnp.concatenate([jnp.broadcast_to(tile[q:q + 1, 128 * p:128 * (p + 1)], (64, 128)) for p in range(4)],
                           axis=0)


def _head_rows():
    r = lax.broadcasted_iota(jnp.int32, (16, 256), 0)
    n = lax.broadcasted_iota(jnp.int32, (16, 256), 1)
    return (r == ((n >> 6) & 1)).astype(BF16)


def _head_sums_row(x, heads2):
    hi = x.astype(BF16)
    lo = (x - hi.astype(F32)).astype(BF16)
    out = lax.dot_general(heads2, jnp.concatenate([hi, lo], axis=1), (((1,), (1,)), ((), ())),
                          preferred_element_type=F32)
    return jnp.concatenate([out[h:h + 1, 64 * p:64 * (p + 1)] for p in range(4) for h in range(2)], axis=1)


def _colsum4(m):
    return jnp.concatenate([jnp.sum(m[64 * p:64 * (p + 1)], axis=0, keepdims=True) for p in range(4)], axis=1)


def _time_base(gi, n_groups, rev):
    return pl.multiple_of(((n_groups - 1 - gi) if rev else gi) * 8, 8)


def _scan_fwd_mxu(s, wf, wb, k2, na, bb, B, T):
    Tc = SCAN_CHUNK
    nT = T // Tc
    nG = Tc // 8
    N = B * T
    nb = _pick(B, (4, 2))
    fwd_j = lambda j: j
    bwd_j = lambda j: nT - 1 - j

    def body(rF, vF, kF, aF, bF, wF, rB, vB, kB, aB, bB, wB, dg, yF, yB, ckF, ckB,
             SF, SB, vcF, vcB, ypF, ypB, ytF, ytB):
        @pl.when(pl.program_id(1) == 0)
        def _():
            SF[...] = jnp.zeros_like(SF)
            SB[...] = jnp.zeros_like(SB)

        ckF[...] = SF[...]
        ckB[...] = SB[...]
        ones2, heads2 = _seg_ones(), _head_rows()
        chains = []
        for n in range(nb):
            chains.append((n, SF, (rF, vF, kF, aF, bF, wF), yF, vcF, ypF, ytF, False))
            chains.append((n, SB, (rB, vB, kB, aB, bB, wB), yB, vcB, ypB, ytB, True))

        def group(gi, carry):
            tiles, states = [], []
            for n, S_ref, refs, _, vc, _, _, rev in chains:
                base = _time_base(gi, nG, rev)
                t = [ref[n, pl.ds(base, 8), :] for ref in refs]
                tiles.append(t)
                states.append(S_ref[n])
                vc[n] = _cols8(t[1], dg, ones2)
            for i8 in range(8):
                for c, (n, _, _, _, vc, yp, _, rev) in enumerate(chains):
                    q = 7 - i8 if rev else i8
                    r, v, k, a, b, w = tiles[c]
                    S = states[c]
                    sa = _seg_mm(S * _rows4(a, q), ones2)
                    S = S * _rows4(w, q) + sa * _rows4(b, q) + vc[n, 256 * q:256 * (q + 1), :] * _rows4(k, q)
                    yp[n, 256 * q:256 * (q + 1), :] = S * _rows4(r, q)
                    states[c] = S
            for c, (n, S_ref, _, y_ref, _, yp, yt, rev) in enumerate(chains):
                S_ref[n] = states[c]
                for q in range(8):
                    yt[n, q:q + 1, :] = _head_sums_row(yp[n, 256 * q:256 * (q + 1), :], heads2)
                y_ref[n, pl.ds(_time_base(gi, nG, rev), 8), :] = yt[n]
            return carry

        lax.fori_loop(0, nG, group, 0)

    row_in = lambda at, col: pl.BlockSpec((nb, Tc, 512), lambda g, j: (g, at(j), col))
    state_io = lambda at: pl.BlockSpec((nb, 256, 128), lambda g, j: (g, at(j), 0))
    in_specs = []
    for at in (fwd_j, bwd_j):
        in_specs += [row_in(at, 0), row_in(at, 2)] + [row_in(at, 0)] * 4
    big = pltpu.VMEM((nb, 8 * 256, 128), F32)
    s3 = s.reshape(B, T, s.shape[-1])
    seq = lambda a: a.reshape(B, T, 512)
    y_f, y_b, ck_f, ck_b = pl.pallas_call(
        body, name="rwkv_scan", grid=(B // nb, nT),
        out_shape=[jax.ShapeDtypeStruct((B, T, 512), F32), jax.ShapeDtypeStruct((B, T, 512), F32),
                   jax.ShapeDtypeStruct((B, nT * 256, 128), F32), jax.ShapeDtypeStruct((B, nT * 256, 128), F32)],
        in_specs=in_specs + [pl.BlockSpec((2048, 128), lambda g, j: (0, 0))],
        out_specs=[row_in(fwd_j, 0), row_in(bwd_j, 0), state_io(fwd_j), state_io(bwd_j)],
        scratch_shapes=[pltpu.VMEM((nb, 256, 128), F32), pltpu.VMEM((nb, 256, 128), F32), big, big, big, big,
                        pltpu.VMEM((nb, 8, 512), F32), pltpu.VMEM((nb, 8, 512), F32)],
        compiler_params=_cparams(),
    )(s3, s3, seq(k2), seq(na), seq(bb), seq(wf), s3, s3, seq(k2), seq(na), seq(bb), seq(wb), _diag_matrix())
    ck_shape = (B * nT * 256, 128)
    return (y_f.reshape(N, 512), y_b.reshape(N, 512),
            ck_f.reshape(ck_shape), ck_b.reshape(ck_shape))


def _scan_bwd_mxu(s, wf, wb, k2, na, bb, dy, ckF, ckB, B, T):
    Tc = SCAN_CHUNK
    nT = T // Tc
    nG = Tc // 8
    N = B * T
    nb = _pick(B, (2,))
    f_at = lambda j: nT - 1 - j
    b_at = lambda j: j
    n_in, n_out, n_scr = 17, 12, 12

    def body(*refs):
        (rF, vF, kF, aF, bF, wF, dyF, ckF_ref, rB, vB, kB, aB, bB, wB, dyB, ckB_ref, dg) = refs[:n_in]
        outsF, outsB = refs[n_in:n_in + 6], refs[n_in + 6:n_in + n_out]
        chains = []
        for n in range(nb):
            stF, stB, saF, saB, vcF, vcB, dSF, dSB, bigF, bigB, tileF, tileB = \
                refs[n_in + n_out + n_scr * n:n_in + n_out + n_scr * (n + 1)]
            chains.append((n, stF, dSF, ckF_ref, (rF, vF, kF, aF, bF, wF, dyF), outsF, bigF, tileF, False, saF, vcF))
            chains.append((n, stB, dSB, ckB_ref, (rB, vB, kB, aB, bB, wB, dyB), outsB, bigB, tileB, True, saB, vcB))

        @pl.when(pl.program_id(1) == 0)
        def _():
            for chain in chains:
                chain[2][...] = jnp.zeros_like(chain[2])

        ones2, heads2 = _seg_ones(), _head_rows()
        for chain in chains:
            chain[1][0] = chain[3][chain[0]]

        cols8 = lambda tile: _cols8(tile, dg, ones2)

        def recompute(gi, carry):
            tiles, states = [], []
            for n, st, _, _, ins, _, big, _, rev, _, vc_keep in chains:
                base = _time_base(gi, nG, rev)
                t = [ref[n, pl.ds(base, 8), :] for ref in ins[1:6]]
                tiles.append(t)
                states.append(st[gi * 8])
                v_cols = cols8(t[0])
                for i8 in range(8):
                    q = 7 - i8 if rev else i8
                    vc_keep[gi * 8 + i8] = v_cols[256 * q:256 * (q + 1)]
            for i8 in range(8):
                for c, (_, st, _, _, _, _, _, _, rev, sa_keep, vc_keep) in enumerate(chains):
                    q = 7 - i8 if rev else i8
                    v, k, a, b, w = tiles[c]
                    S = states[c]
                    sa = _seg_mm(S * _rows4(a, q), ones2)
                    sa_keep[gi * 8 + i8] = sa
                    S = S * _rows4(w, q) + sa * _rows4(b, q) + vc_keep[gi * 8 + i8] * _rows4(k, q)
                    st[gi * 8 + i8 + 1] = S
                    states[c] = S
            return carry

        lax.fori_loop(0, nG, recompute, 0)

        def back(gg, carry):
            gi = nG - 1 - gg
            tiles, grads = [], []
            for n, st, dS_ref, _, ins, _, big, _, rev, _, _ in chains:
                base = _time_base(gi, nG, rev)
                t = [ref[n, pl.ds(base, 8), :] for ref in ins]
                tiles.append(t)
                grads.append(dS_ref[...])
                big[0] = cols8(t[6])
            for i8 in range(7, -1, -1):
                for c, (_, st, _, _, _, _, big, tile, rev, sa_keep, vc_keep) in enumerate(chains):
                    q = 7 - i8 if rev else i8
                    r, v, k, a, b, w, _ = tiles[c]
                    i = gi * 8 + i8
                    S_prev, S_t = st[i], st[i + 1]
                    rows = slice(256 * q, 256 * (q + 1))
                    dy_col, v_col, sa = big[0, rows, :], vc_keep[i], sa_keep[i]
                    dS = grads[c] + dy_col * _rows4(r, q)
                    sb = _seg_mm(dS * _rows4(b, q), ones2)
                    big[1, rows, :] = dS * _rows4(k, q)
                    tile[0, q:q + 1, :] = _colsum4(S_t * dy_col)
                    tile[2, q:q + 1, :] = _colsum4(dS * v_col)
                    tile[3, q:q + 1, :] = _colsum4(S_prev * sb)
                    tile[4, q:q + 1, :] = _colsum4(dS * sa)
                    tile[5, q:q + 1, :] = _colsum4(S_prev * dS)
                    grads[c] = dS * _rows4(w, q) + sb * _rows4(a, q)
            for c, (n, _, dS_ref, _, _, outs, big, tile, rev, _, _) in enumerate(chains):
                dS_ref[...] = grads[c]
                for q in range(8):
                    tile[1, q:q + 1, :] = _head_sums_row(big[1, 256 * q:256 * (q + 1), :], heads2)
                base = _time_base(gi, nG, rev)
                for o, o_ref in enumerate(outs):
                    o_ref[n, pl.ds(base, 8), :] = tile[o]
            return carry

        lax.fori_loop(0, nG, back, 0)

    row_io = lambda at, col: pl.BlockSpec((nb, Tc, 512), lambda g, j: (g, at(j), col))
    in_specs = []
    for at in (f_at, b_at):
        in_specs += [row_io(at, 0), row_io(at, 2)] + [row_io(at, 0)] * 5
        in_specs.append(pl.BlockSpec((nb, 256, 128), lambda g, j, at=at: (g, at(j), 0)))
    in_specs.append(pl.BlockSpec((2048, 128), lambda g, j: (0, 0)))
    out_specs = [row_io(f_at, 0)] * 6 + [row_io(b_at, 0)] * 6
    big = pltpu.VMEM((2, 8 * 256, 128), F32)
    states = pltpu.VMEM((Tc + 1, 256, 128), F32)
    per_step = pltpu.VMEM((Tc, 256, 128), F32)
    one_slot = [states, states, per_step, per_step, per_step, per_step,
                pltpu.VMEM((256, 128), F32), pltpu.VMEM((256, 128), F32), big, big,
                pltpu.VMEM((6, 8, 512), F32), pltpu.VMEM((6, 8, 512), F32)]
    s3 = s.reshape(B, T, s.shape[-1])
    seq = lambda a: a.reshape(B, T, 512)
    ck3 = lambda a: a.reshape(B, nT * 256, 128)
    outs = pl.pallas_call(
        body, name="rwkv_scan_bwd", grid=(B // nb, nT),
        out_shape=[jax.ShapeDtypeStruct((B, T, 512), F32)] * 12,
        in_specs=in_specs, out_specs=out_specs,
        scratch_shapes=one_slot * nb,
        compiler_params=_cparams(),
    )(s3, s3, seq(k2), seq(na), seq(bb), seq(wf), seq(dy), ck3(ckF),
      s3, s3, seq(k2), seq(na), seq(bb), seq(wb), seq(dy), ck3(ckB), _diag_matrix())
    return [o.reshape(N, 512) for o in outs]


def _cat_shards(g4, name, axis):
    return jnp.concatenate([g4[s][name] for s in range(4)], axis=axis)


def _split_shards(full, axis):
    return jnp.split(full, 4, axis=axis)


def kernel(x, norm1_g, w_in, gla_wa2_f, gla_ba_f, gla_wa2_b, gla_ba_b, gla_norm_g, gla_proj, rwkv_mu_prev, rwkv_mu_next, rwkv_w0_f, rwkv_w2_f, rwkv_w0_b, rwkv_w2_b, rwkv_a0, rwkv_a2, rwkv_g2, rwkv_k_k, rwkv_k_a, rwkv_r_k, rwkv_ln_w, rwkv_ln_b, rwkv_proj, w_out, norm2_g, ffn_up, ffn_conv_w, ffn_conv_b, ffn_down, norm_f_g, loss_target, m_norm1_g, m_w_in, m_gla_wa2_f, m_gla_ba_f, m_gla_wa2_b, m_gla_ba_b, m_gla_norm_g, m_gla_proj, m_rwkv_mu_prev, m_rwkv_mu_next, m_rwkv_w0_f, m_rwkv_w2_f, m_rwkv_w0_b, m_rwkv_w2_b, m_rwkv_a0, m_rwkv_a2, m_rwkv_g2, m_rwkv_k_k, m_rwkv_k_a, m_rwkv_r_k, m_rwkv_ln_w, m_rwkv_ln_b, m_rwkv_proj, m_w_out, m_norm2_g, m_ffn_up, m_ffn_conv_w, m_ffn_conv_b, m_ffn_down, m_norm_f_g, v_norm1_g, v_w_in, v_gla_wa2_f, v_gla_ba_f, v_gla_wa2_b, v_gla_ba_b, v_gla_norm_g, v_gla_proj, v_rwkv_mu_prev, v_rwkv_mu_next, v_rwkv_w0_f, v_rwkv_w2_f, v_rwkv_w0_b, v_rwkv_w2_b, v_rwkv_a0, v_rwkv_a2, v_rwkv_g2, v_rwkv_k_k, v_rwkv_k_a, v_rwkv_r_k, v_rwkv_ln_w, v_rwkv_ln_b, v_rwkv_proj, v_w_out, v_norm2_g, v_ffn_up, v_ffn_conv_w, v_ffn_conv_b, v_ffn_down, v_norm_f_g):
    args = locals()
    weights = {n: args[n] for n in WEIGHT_ORDER}
    mom_m = {n: args["m_" + n] for n in WEIGHT_ORDER}
    mom_v = {n: args["v_" + n] for n in WEIGHT_ORDER}
    shapes = {n: weights[n].shape for n in WEIGHT_ORDER}
    B, T, _ = x.shape
    N = B * T
    tm = _pick(N, (512,))

    def local(d):
        sh = {n: d[n].reshape(s) for n, s, _ in SHARDED}
        rp = {n: d[n].reshape(-1) for n, _ in REPLICATED}
        return sh, rp

    w_loc, m_loc, v_loc = local(weights), local(mom_m), local(mom_v)

    small_of = lambda loc: _pack(loc[0], _pack_replicated(loc[1]))
    w_small = small_of(w_loc)
    gathered = _exchange_chips([w_loc[0][n].astype(BF16) for n, _, _ in BIG] + [w_small],
                               "allgather_weights", gather=True)
    small_vals = [_unpack(gathered[-1][s]) for s in range(4)]
    W = {n: jnp.concatenate([gathered[i][s] for s in range(4)], axis=ax) for i, (n, _, ax) in enumerate(BIG)}
    W.update({n: _cat_shards(small_vals, n, ax) for n, _, ax in SMALL})
    R = {n: weights[n].reshape(1, -1) for n, _ in REPLICATED}

    zc = lambda r, c, dt=F32: jnp.zeros((r, c), dt)
    w_in_full = W["w_in"]
    w_in_p = jnp.concatenate([w_in_full[:, 0:1536], w_in_full[:, 1568:3104], w_in_full[:, 3360:5408],
                              w_in_full[:, 3104:3360], w_in_full[:, 1536:1568],
                              zc(1024, PROJ_PAD - N_PROJ, BF16)], axis=1)
    w_in_b = w_in_p
    pad_ff = lambda a: jnp.concatenate([a[:, :D_FF], zc(a.shape[0], FF_PAD - D_FF, a.dtype), a[:, D_FF:],
                                        zc(a.shape[0], FF_PAD - D_FF, a.dtype)], axis=1)
    ffn_up_p = pad_ff(W["ffn_up"])
    ffn_up_b = ffn_up_p
    conv_w_p = pad_ff(W["ffn_conv_w"])
    conv_b_p = pad_ff(R["ffn_conv_b"])
    ffn_down_p = jnp.concatenate([W["ffn_down"], zc(FF_PAD - D_FF, 1024, BF16)], axis=0)
    ffn_down_b = ffn_down_p
    w_out_b = W["w_out"]
    gla_proj_b = W["gla_proj"]
    rwkv_proj_b = W["rwkv_proj"]
    wa2 = jnp.stack([jnp.concatenate([W["gla_wa2_f"], zc(112, 256)], axis=0),
                     jnp.concatenate([zc(16, 256), W["gla_wa2_b"], zc(96, 256)], axis=0)])
    ba = jnp.stack([R["gla_ba_f"], R["gla_ba_b"]])
    w2_f = jnp.concatenate([W["rwkv_w2_f"], zc(64, 512)], axis=0)
    w2_b = jnp.concatenate([W["rwkv_w2_b"], zc(64, 512)], axis=0)
    a2 = jnp.concatenate([zc(64, 512), W["rwkv_a2"]], axis=0)
    g2 = W["rwkv_g2"]
    head_ones = np.kron(np.eye(8, dtype=np.float32), np.ones((64, 64), np.float32))
    seg64 = jnp.asarray(np.concatenate([head_ones, head_ones], axis=0), dtype=BF16)

    x2d = x.reshape(N, D_MODEL)
    tgt = loss_target.reshape(N, D_MODEL)

    (h1,) = _rowwise("norm1", _fn_norm, [(x2d, 1024, 0)], [R["norm1_g"]], [], [(1024, BF16)], N, tm)
    p = _matmul(h1, w_in_b, "proj_in")
    o_gla, gla_states = _gla_fwd(p, wa2, ba, B, T)
    gla_post_rows = [(o_gla, 512, 0, 0), (o_gla, 512, 0, 1), (p, 512, 2)]
    (gated,) = _rowwise("gla_post", _fn_gla_post, gla_post_rows, [R["gla_norm_g"]], [], [(512, BF16)], N, tm)
    y_a = _matmul(gated, gla_proj_b, "gla_out")
    s = _token_shift(p, R["rwkv_mu_prev"], R["rwkv_mu_next"], B, T)
    pre_rows = [(s, 512, 1), (s, 256, 6)]
    pre_params = [R["rwkv_w0_f"], R["rwkv_w0_b"], R["rwkv_a0"], R["rwkv_k_k"], R["rwkv_k_a"], w2_f, w2_b, a2, g2]
    wf, wb, k2, na, bb, g = _rowwise("rwkv_pre", _fn_rwkv_pre, pre_rows, pre_params, [seg64],
                                     [(512, F32)] * 6, N, tm)
    y_f, y_b, ck_f, ck_b = _scan_fwd_mxu(s, wf, wb, k2, na, bb, B, T)
    post_rows = [(y_f, 512, 0), (y_b, 512, 0), (s, 512, 0), (k2, 512, 0), (s, 512, 2), (g, 512, 0)]
    post_params = [R["rwkv_ln_w"], R["rwkv_ln_b"], R["rwkv_r_k"]]
    (o_rwkv,) = _rowwise("rwkv_post", _fn_rwkv_post, post_rows, post_params, [seg64], [(512, BF16)], N, tm)
    y_r = _matmul(o_rwkv, rwkv_proj_b, "rwkv_out")
    merge_rows = [(p, 1024, 3), (p, 1024, 4), (y_a, 1024, 0), (y_r, 1024, 0)]
    (merged,) = _rowwise("merge", _fn_merge, merge_rows, [], [], [(1024, BF16)], N, tm)
    x1 = _matmul(merged, w_out_b, "mix_out", residual=x2d)
    (h2,) = _rowwise("norm2", _fn_norm, [(x1, 1024, 0)], [R["norm2_g"]], [], [(1024, BF16)], N, tm)
    u = _matmul(h2, ffn_up_b, "ffn_up")
    act = _ffn_conv(u, conv_w_p, conv_b_p, B, T)
    x2 = _matmul(act, ffn_down_b, "ffn_down", residual=x1)
    loss_blk, dx2, dx2_b, d_norm_f = _loss_head(x2, tgt, weights["norm_f_g"].reshape(1, -1))

    d_act = _matmul(dx2_b, ffn_down_p.T.astype(BF16), "d_act")
    d_ffn_down = _matmul_tn(act, dx2_b, "dw_ffn_down")[:D_FF]
    du_g, du_v, d_conv_w_p, d_conv_b_p = _ffn_conv_bwd(u, d_act, conv_w_p, conv_b_p, B, T)
    up_t = ffn_up_p.T.astype(BF16)
    d_h2 = _matmul(du_v, up_t[FF_PAD:], "d_h2_v", residual=_matmul(du_g, up_t[:FF_PAD], "d_h2_g"))
    d_ffn_up = jnp.concatenate([_matmul_tn(h2, du_g, "dw_ffn_up_g")[:, :D_FF],
                                _matmul_tn(h2, du_v, "dw_ffn_up_v")[:, :D_FF]], axis=1)
    unpad_ff = lambda a: jnp.concatenate([a[:, :D_FF], a[:, FF_PAD:FF_PAD + D_FF]], axis=1)
    (dx1,), (d_norm2,) = _rowwise_bwd("norm2_bwd", _fn_norm, [(x1, 1024, 0)], [R["norm2_g"]], [],
                                      [[(d_h2, 1024, 0)]], [(F32, (dx2, 1024, 0))], N, tm)
    dx1_b = dx1.astype(BF16)
    d_merged = _matmul(dx1_b, W["w_out"].T.astype(BF16), "d_merged")
    d_w_out = _matmul_tn(merged, dx1_b, "dw_out")
    (d_ga, d_gb, d_ya, d_yr), _ = _rowwise_bwd("merge_bwd", _fn_merge, merge_rows, [], [],
                                               [[(d_merged, 1024, 0)]], [(BF16, None)] * 4, N, tm)
    d_o_rwkv = _matmul(d_yr, W["rwkv_proj"].T.astype(BF16), "d_o_rwkv")
    d_rwkv_proj = _matmul_tn(o_rwkv, d_yr, "dw_rwkv_proj")
    (d_y, d_r_bonus, d_k2_bonus, d_v_bonus, d_g), (d_ln_w, d_ln_b, d_r_k) = _rowwise_bwd(
        "rwkv_post_bwd", _fn_rwkv_post, post_rows, post_params, [seg64], [[(d_o_rwkv, 512, 0)]],
        [(F32, None), None, (F32, None), (F32, None), (F32, None), (F32, None)], N, tm)
    (drF, dvF, dkF, daF, dbF, dwF, drB, dvB, dkB, daB, dbB, dwB) = _scan_bwd_mxu(s, wf, wb, k2, na, bb, d_y, ck_f, ck_b, B, T)
    pre_cts = [[(dwF, 512, 0)], [(dwB, 512, 0)], [(dkF, 512, 0), (dkB, 512, 0), (d_k2_bonus, 512, 0)],
               [(daF, 512, 0), (daB, 512, 0)], [(dbF, 512, 0), (dbB, 512, 0)], [(d_g, 512, 0)]]
    (ds_k, ds_wag), pre_grads = _rowwise_bwd("rwkv_pre_bwd", _fn_rwkv_pre, pre_rows, pre_params, [seg64], pre_cts,
                                             [(F32, None), (F32, None)], N, tm)
    d_w0_f, d_w0_b, d_a0, d_k_k, d_k_a, d_w2_f, d_w2_b, d_a2, d_g2 = pre_grads
    ds = jnp.concatenate([drF + drB + d_r_bonus, ds_k, dvF + dvB + d_v_bonus, ds_wag], axis=1)
    d_p_rwkv, d_mu_prev, d_mu_next = _token_shift_bwd(p, ds, R["rwkv_mu_prev"], R["rwkv_mu_next"], B, T)
    d_gated = _matmul(d_ya, W["gla_proj"].T.astype(BF16), "d_gated")
    d_gla_proj = _matmul_tn(gated, d_ya, "dw_gla_proj")
    (d_o, d_og), (d_gla_norm,) = _rowwise_bwd(
        "gla_post_bwd", _fn_gla_post, gla_post_rows, [R["gla_norm_g"]], [], [[(d_gated, 512, 0)]],
        [(F32, None), None, (BF16, None)], N, tm)
    dqkv2, dafab2, d_wa2, d_ba = _gla_bwd(p, d_o, gla_states, wa2, ba, B, T)
    add2 = lambda a, b: (a + b,)
    (d_qkv,) = _rowwise("sum_dqkv", add2, [(dqkv2, 1024, 0, 0), (dqkv2, 1024, 0, 1)], [], [], [(1024, BF16)], N, tm)
    (d_afab,) = _rowwise("sum_dafab", add2, [(dafab2, 128, 0, 0), (dafab2, 128, 0, 1)], [], [], [(128, BF16)], N, tm)
    w_in_t = w_in_p.T.astype(BF16)
    w_rwkv_t = jnp.concatenate([w_in_t[1536:3072], w_in_t[5120:5376]], axis=0)
    d_h1 = _matmul(d_qkv, w_in_t[0:1024], "d_h1_qkv")
    d_h1 = _matmul(d_og, w_in_t[1024:1536], "d_h1_og", residual=d_h1)
    d_h1 = _matmul(d_p_rwkv, w_rwkv_t, "d_h1_rwkv", residual=d_h1)
    d_h1 = _matmul(d_ga, w_in_t[3072:4096], "d_h1_ga", residual=d_h1)
    d_h1 = _matmul(d_gb, w_in_t[4096:5120], "d_h1_gb", residual=d_h1)
    d_h1 = _matmul(d_afab, w_in_t[5376:5504], "d_h1_afab", residual=d_h1)
    d_w_in = jnp.concatenate([
        _matmul_tn(h1, d_qkv, "dw_in_qkv"), _matmul_tn(h1, d_og, "dw_in_og"),
        _matmul_tn(h1, d_afab, "dw_in_afab")[:, :32], _matmul_tn(h1, d_p_rwkv, "dw_in_rwkv"),
        _matmul_tn(h1, d_ga, "dw_in_ga"), _matmul_tn(h1, d_gb, "dw_in_gb")], axis=1)
    (grad_x,), (d_norm1,) = _rowwise_bwd("norm1_bwd", _fn_norm, [(x2d, 1024, 0)], [R["norm1_g"]], [],
                                         [[(d_h1, 1024, 0)]], [(F32, (dx1, 1024, 0))], N, tm)

    full_grads = {
        "w_in": d_w_in, "gla_wa2_f": d_wa2[0, 0:16], "gla_wa2_b": d_wa2[1, 16:32], "gla_proj": d_gla_proj,
        "rwkv_w2_f": d_w2_f[0:64], "rwkv_w2_b": d_w2_b[0:64], "rwkv_a2": d_a2[64:128], "rwkv_g2": d_g2,
        "rwkv_proj": d_rwkv_proj, "w_out": d_w_out, "ffn_up": d_ffn_up, "ffn_conv_w": unpad_ff(d_conv_w_p),
        "ffn_down": d_ffn_down,
    }
    repl_grads = {
        "norm1_g": d_norm1, "gla_ba_f": d_ba[0], "gla_ba_b": d_ba[1], "gla_norm_g": d_gla_norm,
        "rwkv_mu_prev": d_mu_prev, "rwkv_mu_next": d_mu_next, "rwkv_w0_f": d_w0_f, "rwkv_w0_b": d_w0_b,
        "rwkv_a0": d_a0, "rwkv_k_k": d_k_k, "rwkv_k_a": d_k_a, "rwkv_r_k": d_r_k, "rwkv_ln_w": d_ln_w,
        "rwkv_ln_b": d_ln_b, "norm2_g": d_norm2, "ffn_conv_b": unpad_ff(d_conv_b_p), "norm_f_g": d_norm_f,
    }
    split = {n: _split_shards(full_grads[n], ax) for n, _, ax in SHARDED}
    to_owners = [jnp.stack([p.astype(BF16) for p in split[n]]) for n, _, _ in BIG]
    repl_flat = _pack_replicated(repl_grads, loss_blk[0, 0])
    to_owners.append(jnp.stack([_pack({n: split[n][sidx] for n, _, _ in SMALL}, repl_flat) for sidx in range(4)]))
    received = _exchange_chips(to_owners, "scatter_grads", gather=False)
    names = [n for n, _, _ in BIG] + ["small"]
    mine = [_sum_sources(r, "sum_" + n) for r, n in zip(received, names)]
    other = _swap_with_sibling(mine)
    packs = [w_small, small_of(m_loc), small_of(v_loc)]
    results = {}
    for i, n in enumerate(names):
        wmv = packs if n == "small" else [d[0][n] for d in (w_loc, m_loc, v_loc)]
        results[n] = _adamw(mine[i], other[i], *wmv, "adamw_" + n)
    small = [_unpack(f) for f in results["small"]]
    outs = [small[0]["loss"], grad_x.reshape(B, T, D_MODEL)]
    for kind in range(4):
        for n in WEIGHT_ORDER:
            val = results[n][kind] if n in results else small[kind][n]
            outs.append(val.reshape(shapes[n]))
    return tuple(outs)
```

```python
import functools

import jax
import jax.numpy as jnp
import numpy as np
from jax import lax
from jax.experimental import pallas as pl
from jax.experimental.pallas import tpu as pltpu

F32 = jnp.float32
BF16 = jnp.bfloat16
HIGHEST = lax.Precision.HIGHEST
MESH_IDS = pl.DeviceIdType.MESH

D_MODEL = 1024
N_PROJ = 5408
PROJ_PAD = 5632
D_FF = 2752
FF_PAD = 2816
GLA_CHUNK = 64
SCAN_CHUNK = 16
NORM_EPS = 1e-6
HEAD_NORM_EPS = 1e-5
RWKV_GN_EPS = 64 * 1e-5
ADAM_LR, ADAM_B1, ADAM_B2, ADAM_EPS, ADAM_WD, ADAM_STEP = 0.001, 0.9, 0.999, 1e-08, 0.01, 10
VMEM_LIMIT = 56 * 1024 * 1024

FLAT_ROWS, FLAT_COLS = 128, 1024
BIG = (
    ("w_in", (1024, 1352), 1), ("gla_proj", (512, 256), 1), ("rwkv_proj", (512, 256), 1),
    ("w_out", (256, 1024), 0), ("ffn_up", (1024, 1376), 1), ("ffn_down", (688, 1024), 0),
)
SMALL = (
    ("gla_wa2_f", (16, 64), 1), ("gla_wa2_b", (16, 64), 1), ("rwkv_w2_f", (64, 128), 1),
    ("rwkv_w2_b", (64, 128), 1), ("rwkv_a2", (64, 128), 1), ("rwkv_g2", (128, 128), 1),
    ("ffn_conv_w", (3, 1376), 1),
)
SHARDED = BIG + SMALL
REPLICATED = (
    ("norm1_g", 1024), ("gla_ba_f", 256), ("gla_ba_b", 256), ("gla_norm_g", 512),
    ("rwkv_mu_prev", 1792), ("rwkv_mu_next", 1792), ("rwkv_w0_f", 512), ("rwkv_w0_b", 512),
    ("rwkv_a0", 512), ("rwkv_k_k", 512), ("rwkv_k_a", 512), ("rwkv_r_k", 512),
    ("rwkv_ln_w", 512), ("rwkv_ln_b", 512), ("norm2_g", 1024), ("ffn_conv_b", 5504),
    ("norm_f_g", 1024),
)
WEIGHT_ORDER = ("norm1_g", "w_in", "gla_wa2_f", "gla_ba_f", "gla_wa2_b", "gla_ba_b", "gla_norm_g", "gla_proj",
                "rwkv_mu_prev", "rwkv_mu_next", "rwkv_w0_f", "rwkv_w2_f", "rwkv_w0_b", "rwkv_w2_b", "rwkv_a0",
                "rwkv_a2", "rwkv_g2", "rwkv_k_k", "rwkv_k_a", "rwkv_r_k", "rwkv_ln_w", "rwkv_ln_b", "rwkv_proj",
                "w_out", "norm2_g", "ffn_up", "ffn_conv_w", "ffn_conv_b", "ffn_down", "norm_f_g")


def _cparams(**kw):
    return pltpu.CompilerParams(vmem_limit_bytes=VMEM_LIMIT, **kw)


def _pack_replicated(repl_vals, loss=None):
    parts = [repl_vals[n].reshape(-1) for n, _ in REPLICATED]
    parts.append(jnp.zeros((1,), F32) if loss is None else loss.reshape(1))
    return jnp.concatenate(parts)


def _pack(sharded_vals, repl_flat):
    parts = [sharded_vals[n].reshape(-1) for n, _, _ in SMALL] + [repl_flat]
    used = sum(int(np.prod(s)) for _, s, _ in SMALL) + sum(w for _, w in REPLICATED) + 1
    parts.append(jnp.zeros((FLAT_ROWS * FLAT_COLS - used,), F32))
    return jnp.concatenate(parts).reshape(FLAT_ROWS, FLAT_COLS)


def _unpack(flat):
    v = flat.reshape(-1)
    out, off = {}, 0
    for n, s, _ in SMALL:
        k = int(np.prod(s))
        out[n] = v[off:off + k].reshape(s)
        off += k
    for n, w in REPLICATED:
        out[n] = v[off:off + w]
        off += w
    out["loss"] = v[off]
    return out


def _chip_peers():
    x, y, c = lax.axis_index("x"), lax.axis_index("y"), lax.axis_index("c")
    return x, y, c, ((1 - x, y), (x, 1 - y), (1 - x, 1 - y))


def _exchange_chips(arrs, name, gather):
    n = len(arrs)

    def body(*refs):
        srcs, outs = refs[:n], refs[n:2 * n]
        send_sems, recv_sems, local_sems = refs[2 * n:]
        x, y, c, peers = _chip_peers()
        me = 2 * x + y
        own = []
        for i in range(n):
            cp = pltpu.make_async_copy(srcs[i] if gather else srcs[i].at[me], outs[i].at[me], local_sems.at[i])
            cp.start()
            own.append(cp)
        sends = []
        for k, (px, py) in enumerate(peers):
            for i in range(n):
                cp = pltpu.make_async_remote_copy(
                    src_ref=srcs[i] if gather else srcs[i].at[2 * px + py], dst_ref=outs[i].at[me],
                    send_sem=send_sems.at[3 * i + k], recv_sem=recv_sems.at[3 * i + k],
                    device_id=(px, py, c), device_id_type=MESH_IDS)
                cp.start()
                sends.append(cp)
        for k, (px, py) in enumerate(peers):
            for i in range(n):
                pltpu.make_async_remote_copy(
                    src_ref=srcs[i] if gather else srcs[i].at[me], dst_ref=outs[i].at[2 * px + py],
                    send_sem=send_sems.at[3 * i + k], recv_sem=recv_sems.at[3 * i + k],
                    device_id=(px, py, c), device_id_type=MESH_IDS).wait_recv()
        for cp in sends:
            cp.wait_send()
        for cp in own:
            cp.wait()

    out_shape = [jax.ShapeDtypeStruct(((4,) + a.shape) if gather else a.shape, a.dtype) for a in arrs]
    return pl.pallas_call(
        body, name=name, out_shape=out_shape,
        in_specs=[pl.BlockSpec(memory_space=pl.ANY)] * n,
        out_specs=[pl.BlockSpec(memory_space=pl.ANY)] * n,
        scratch_shapes=[pltpu.SemaphoreType.DMA((3 * n,)), pltpu.SemaphoreType.DMA((3 * n,)),
                        pltpu.SemaphoreType.DMA((n,))],
    )(*arrs)


def _allgather_chips(arrs):
    n = len(arrs)
    halves = [a.shape[0] // 2 for a in arrs]

    def body(*refs):
        srcs, outs = refs[:n], refs[n:2 * n]
        ici_send, ici_recv, d2d_send, d2d_recv, local_sems = refs[2 * n:]
        x, y, c, peers = _chip_peers()
        me = 2 * x + y
        rows = lambda i, who: pl.ds(who * halves[i], halves[i])
        own = []
        for i in range(n):
            cp = pltpu.make_async_copy(srcs[i], outs[i].at[me], local_sems.at[i])
            cp.start()
            own.append(cp)

        def over_ici(k, i, slot):
            px, py = peers[k]
            return pltpu.make_async_remote_copy(
                src_ref=srcs[i].at[rows(i, c)], dst_ref=outs[i].at[slot, rows(i, c)],
                send_sem=ici_send.at[3 * i + k], recv_sem=ici_recv.at[3 * i + k],
                device_id=(px, py, c), device_id_type=MESH_IDS)

        def over_d2d(k, i, half):
            px, py = peers[k]
            where = outs[i].at[2 * px + py, rows(i, half)]
            return pltpu.make_async_remote_copy(
                src_ref=where, dst_ref=where, send_sem=d2d_send.at[3 * i + k], recv_sem=d2d_recv.at[3 * i + k],
                device_id=(x, y, 1 - c), device_id_type=MESH_IDS)

        sends = [over_ici(k, i, me) for k in range(3) for i in range(n)]
        for cp in sends:
            cp.start()
        passed = []
        for k, (px, py) in enumerate(peers):
            for i in range(n):
                over_ici(k, i, 2 * px + py).wait_recv()
                cp = over_d2d(k, i, c)
                cp.start()
                passed.append(cp)
        for k in range(3):
            for i in range(n):
                over_d2d(k, i, 1 - c).wait_recv()
        for cp in sends + passed:
            cp.wait_send()
        for cp in own:
            cp.wait()

    return pl.pallas_call(
        body, name="allgather_weights",
        out_shape=[jax.ShapeDtypeStruct((4,) + a.shape, a.dtype) for a in arrs],
        in_specs=[pl.BlockSpec(memory_space=pl.ANY)] * n,
        out_specs=[pl.BlockSpec(memory_space=pl.ANY)] * n,
        scratch_shapes=[pltpu.SemaphoreType.DMA((3 * n,))] * 4 + [pltpu.SemaphoreType.DMA((n,))],
    )(*arrs)


def _swap_with_sibling(arrs):
    n = len(arrs)

    def body(*refs):
        srcs, outs = refs[:n], refs[n:2 * n]
        send_sems, recv_sems = refs[2 * n:]
        x, y, c = lax.axis_index("x"), lax.axis_index("y"), lax.axis_index("c")
        cps = [pltpu.make_async_remote_copy(src_ref=srcs[i], dst_ref=outs[i], send_sem=send_sems.at[i],
                                            recv_sem=recv_sems.at[i], device_id=(x, y, 1 - c),
                                            device_id_type=MESH_IDS) for i in range(n)]
        for cp in cps:
            cp.start()
        for cp in cps:
            cp.wait()

    return pl.pallas_call(
        body, name="swap_sibling",
        out_shape=[jax.ShapeDtypeStruct(a.shape, a.dtype) for a in arrs],
        in_specs=[pl.BlockSpec(memory_space=pl.ANY)] * n,
        out_specs=[pl.BlockSpec(memory_space=pl.ANY)] * n,
        scratch_shapes=[pltpu.SemaphoreType.DMA((n,)), pltpu.SemaphoreType.DMA((n,))],
    )(*arrs)


def _row_tile(rows, cols):
    cap = max(8, (3 << 19) // (4 * (-(-cols // 128) * 128)))
    best = None
    for t in range(8, min(rows, cap) + 1, 8):
        if rows % t == 0:
            best = t
    return best or rows


def _sum_sources(r4, name):
    _, A, Bc = r4.shape
    ta = _row_tile(A, Bc)

    def body(r_ref, o_ref):
        f = lambda s: r_ref[s].astype(F32)
        o_ref[...] = ((f(0) + f(1)) + f(2)) + f(3)

    return pl.pallas_call(
        body, name=name, grid=(A // ta,),
        out_shape=jax.ShapeDtypeStruct((A, Bc), F32),
        in_specs=[pl.BlockSpec((4, ta, Bc), lambda i: (0, i, 0))],
        out_specs=pl.BlockSpec((ta, Bc), lambda i: (i, 0)),
        compiler_params=_cparams(),
    )(r4)


def _adamw(own, other, w, m, v, name):
    R, C = own.shape
    tr = _row_tile(R, C)

    def body(a_ref, b_ref, w_ref, m_ref, v_ref, g_out, d_out, m_out, v_out):
        g = a_ref[...] + b_ref[...]
        m_new = ADAM_B1 * m_ref[...] + (1.0 - ADAM_B1) * g
        v_new = ADAM_B2 * v_ref[...] + (1.0 - ADAM_B2) * (g * g)
        m_hat = m_new / (1.0 - ADAM_B1 ** ADAM_STEP)
        v_hat = v_new / (1.0 - ADAM_B2 ** ADAM_STEP)
        g_out[...] = g
        d_out[...] = -ADAM_LR * (m_hat / (jnp.sqrt(v_hat) + ADAM_EPS) + ADAM_WD * w_ref[...])
        m_out[...] = m_new
        v_out[...] = v_new

    spec = pl.BlockSpec((tr, C), lambda i: (i, 0))
    return pl.pallas_call(
        body, name=name, grid=(R // tr,),
        out_shape=[jax.ShapeDtypeStruct((R, C), F32)] * 4,
        in_specs=[spec] * 5, out_specs=[spec] * 4,
        compiler_params=_cparams(),
    )(own, other, w, m, v)


def _pick(n, options):
    for o in options:
        if n % o == 0:
            return o
    return n


def _div128(n, cap):
    best = None
    for t in range(128, min(n, cap) + 1, 128):
        if n % t == 0:
            best = t
    return best or n


MATMUL_VMEM = 40 * 1024 * 1024


def _matmul(a, b, name, out_dtype=F32, residual=None):
    M, K = a.shape
    _, N = b.shape
    tm, tn = _pick(M, (1024, 512)), _div128(N, 1408)
    while tm > 256 and 2 * (2 * tm * K + 2 * K * tn + (8 if residual is not None else 4) * tm * tn) > MATMUL_VMEM:
        tm //= 2

    def body(*refs):
        a_ref, b_ref = refs[0], refs[1]
        o_ref = refs[-1]
        acc = jnp.dot(a_ref[...], b_ref[...], preferred_element_type=F32)
        if residual is not None:
            acc = acc + refs[2][...]
        o_ref[...] = acc.astype(out_dtype)

    in_specs = [pl.BlockSpec((tm, K), lambda j, i: (i, 0)), pl.BlockSpec((K, tn), lambda j, i: (0, j))]
    args = [a, b]
    if residual is not None:
        in_specs.append(pl.BlockSpec((tm, tn), lambda j, i: (i, j)))
        args.append(residual)
    return pl.pallas_call(
        body, name=name, grid=(N // tn, M // tm),
        out_shape=jax.ShapeDtypeStruct((M, N), out_dtype),
        in_specs=in_specs, out_specs=pl.BlockSpec((tm, tn), lambda j, i: (i, j)),
        compiler_params=_cparams(),
    )(*args)


def _matmul_tn(a, b, name):
    R, M = a.shape
    _, N = b.shape
    tr, tm, tn = _pick(R, (2048, 1024, 512)), _div128(M, 1408), _div128(N, 1408)
    while tr > 512 and 2 * (2 * tr * tm + 2 * tr * tn + 4 * tm * tn) > MATMUL_VMEM:
        tr //= 2

    def body(a_ref, b_ref, o_ref):
        @pl.when(pl.program_id(2) == 0)
        def _():
            o_ref[...] = jnp.zeros_like(o_ref)

        o_ref[...] += lax.dot_general(a_ref[...], b_ref[...], (((0,), (0,)), ((), ())),
                                      preferred_element_type=F32)

    return pl.pallas_call(
        body, name=name, grid=(M // tm, N // tn, R // tr),
        out_shape=jax.ShapeDtypeStruct((M, N), F32),
        in_specs=[pl.BlockSpec((tr, tm), lambda i, j, r: (r, i)), pl.BlockSpec((tr, tn), lambda i, j, r: (r, j))],
        out_specs=pl.BlockSpec((tm, tn), lambda i, j, r: (i, j)),
        compiler_params=_cparams(),
    )(a, b)


def _row_spec(tm, width, col, lead=None):
    if lead is None:
        return pl.BlockSpec((tm, width), lambda i: (i, col))
    return pl.BlockSpec((None, tm, width), lambda i: (lead, i, col))


def _whole_spec(arr):
    nd = arr.ndim
    return pl.BlockSpec(arr.shape, lambda i: (0,) * nd)


def _rowwise(name, fn, rows, params, consts, outs, n_rows, tm):
    nr, npar, nc = len(rows), len(params), len(consts)

    def body(*refs):
        vals = [r[...].astype(F32) for r in refs[:nr]] + [r[...] for r in refs[nr:nr + npar + nc]]
        res = fn(*vals)
        for o_ref, r in zip(refs[nr + npar + nc:], res):
            o_ref[...] = r.astype(o_ref.dtype)

    return pl.pallas_call(
        body, name=name, grid=(n_rows // tm,),
        out_shape=[jax.ShapeDtypeStruct((n_rows, w), dt) for w, dt in outs],
        in_specs=[_row_spec(tm, *r[1:]) for r in rows] + [_whole_spec(p) for p in params + consts],
        out_specs=[_row_spec(tm, w, 0) for w, _ in outs],
        compiler_params=_cparams(),
    )(*[r[0] for r in rows], *params, *consts)


def _rowwise_bwd(name, fn, rows, params, consts, cts, row_grads, n_rows, tm):
    nr, npar, nc = len(rows), len(params), len(consts)
    ct_flat = [p for pieces in cts for p in pieces]
    res_flat = [rg[1] for rg in row_grads if rg is not None and rg[1] is not None]
    n_ct, n_res = len(ct_flat), len(res_flat)
    n_in = nr + npar + nc + n_ct + n_res
    wanted = [k for k, rg in enumerate(row_grads) if rg is not None]

    def body(*refs):
        row_vals = [r[...].astype(F32) for r in refs[:nr]]
        par_vals = [r[...] for r in refs[nr:nr + npar]]
        const_vals = [r[...] for r in refs[nr + npar:nr + npar + nc]]
        ct_refs = refs[nr + npar + nc:nr + npar + nc + n_ct]
        res_refs = refs[nr + npar + nc + n_ct:n_in]
        out_refs = refs[n_in:]
        ct_vals, pos = [], 0
        for pieces in cts:
            acc = ct_refs[pos][...].astype(F32)
            for q in range(1, len(pieces)):
                acc = acc + ct_refs[pos + q][...].astype(F32)
            pos += len(pieces)
            ct_vals.append(acc)
        _, vjp = jax.vjp(lambda *a: tuple(fn(*a, *const_vals)), *row_vals, *par_vals)
        grads = vjp(tuple(ct_vals))
        ri = 0
        for slot, k in enumerate(wanted):
            g = grads[k]
            if row_grads[k][1] is not None:
                g = g + res_refs[ri][...].astype(F32)
                ri += 1
            out_refs[slot][...] = g.astype(out_refs[slot].dtype)

        @pl.when(pl.program_id(0) == 0)
        def _():
            for q in range(npar):
                out_refs[len(wanted) + q][...] = jnp.zeros_like(out_refs[len(wanted) + q])

        for q in range(npar):
            out_refs[len(wanted) + q][...] += grads[nr + q]

    out_shape = [jax.ShapeDtypeStruct((n_rows, rows[k][1]), row_grads[k][0]) for k in wanted]
    out_shape += [jax.ShapeDtypeStruct(p.shape, F32) for p in params]
    out_specs = [_row_spec(tm, rows[k][1], 0) for k in wanted] + [_whole_spec(p) for p in params]
    in_specs = [_row_spec(tm, *r[1:]) for r in rows] + [_whole_spec(p) for p in params + consts]
    in_specs += [_row_spec(tm, *r[1:]) for r in ct_flat + res_flat]
    res = pl.pallas_call(
        body, name=name, grid=(n_rows // tm,),
        out_shape=out_shape, in_specs=in_specs, out_specs=out_specs,
        compiler_params=_cparams(),
    )(*[r[0] for r in rows], *params, *consts, *[r[0] for r in ct_flat + res_flat])
    return res[:len(wanted)], res[len(wanted):]


def _sigmoid(x):
    return 0.5 * jnp.tanh(0.5 * x) + 0.5


def _softplus(x):
    return jnp.maximum(x, 0.0) + jnp.log(1.0 + jnp.exp(-jnp.abs(x)))


def _seg_dot_impl(x, seg2):
    hi = x.astype(BF16)
    lo = (x - hi.astype(F32)).astype(BF16)
    return jnp.dot(jnp.concatenate([hi, lo], axis=1), seg2, preferred_element_type=F32)


@jax.custom_vjp
def _seg_dot(x, seg2):
    return _seg_dot_impl(x, seg2)


_seg_dot.defvjp(lambda x, seg2: (_seg_dot_impl(x, seg2), seg2),
                lambda seg2, ct: (_seg_dot_impl(ct, seg2), jnp.zeros_like(seg2)))


def _fn_norm(x, g):
    r = lax.rsqrt(jnp.mean(x * x, axis=-1, keepdims=True) + NORM_EPS)
    return ((x * r) * g,)


def _fn_gla_post(o_f, o_b, og, norm_g):
    o = o_f + o_b
    heads = []
    for h in range(4):
        oh = o[:, h * 128:(h + 1) * 128]
        heads.append(oh * lax.rsqrt(jnp.mean(oh * oh, axis=-1, keepdims=True) + HEAD_NORM_EPS))
    on = jnp.concatenate(heads, axis=1) * norm_g
    return (on * (og * _sigmoid(og)),)


def _fn_rwkv_pre(s_k, s_wag, w0_f, w0_b, a0, k_k, k_a, w2_f, w2_b, a2, g2, seg64):
    wa = s_wag[:, 0:128]
    gl = s_wag[:, 128:256]
    tw = jnp.tanh(wa)
    z_f = w0_f + jnp.dot(tw, w2_f, preferred_element_type=F32)
    z_b = w0_b + jnp.dot(tw, w2_b, preferred_element_type=F32)
    w_f = jnp.exp(-jnp.exp(-_softplus(-z_f) - 0.5))
    w_b = jnp.exp(-jnp.exp(-_softplus(-z_b) - 0.5))
    a = _sigmoid(a0 + jnp.dot(wa, a2, preferred_element_type=F32))
    g = jnp.dot(_sigmoid(gl), g2, preferred_element_type=F32)
    kk = s_k * k_k
    kkn = kk / jnp.maximum(jnp.sqrt(_seg_dot(kk * kk, seg64)), 1e-12)
    k2 = s_k * (1.0 + (a - 1.0) * k_a)
    return w_f, w_b, k2, -kkn, kkn * a, g


def _fn_rwkv_post(y_f, y_b, s_r, k2, s_v, g, ln_w, ln_b, r_k, seg64):
    y = y_f + y_b
    mu = _seg_dot(y, seg64) * (1.0 / 64.0)
    yc = y - mu
    var = _seg_dot(yc * yc, seg64) * (1.0 / 64.0)
    yn = yc * lax.rsqrt(var + RWKV_GN_EPS) * ln_w + ln_b
    bonus = _seg_dot(s_r * k2 * r_k, seg64) * s_v
    return ((yn + bonus) * g,)


def _fn_merge(ga, gb, y_a, y_b):
    return (_sigmoid(ga) * y_a + _sigmoid(gb) * y_b,)


def _loss_head(x2, target, gf):
    N, Dm = x2.shape
    tm = _pick(N, (512,))

    def fn(x, g, t):
        r = lax.rsqrt(jnp.mean(x * x, axis=-1, keepdims=True) + NORM_EPS)
        err = (x * r) * g - t
        return 0.5 * jnp.sum(jnp.mean(err * err, axis=-1, keepdims=True), axis=0, keepdims=True)

    def body(x_ref, t_ref, g_ref, loss_ref, dx_ref, dxb_ref, dg_ref):
        t = t_ref[...]
        loss, vjp = jax.vjp(lambda x, g: fn(x, g, t), x_ref[...], g_ref[...])
        dx, dg = vjp(jnp.ones((1, 1), F32))

        @pl.when(pl.program_id(0) == 0)
        def _():
            loss_ref[...] = jnp.zeros_like(loss_ref)
            dg_ref[...] = jnp.zeros_like(dg_ref)

        loss_ref[...] += jnp.broadcast_to(loss, loss_ref.shape)
        dg_ref[...] += dg
        dx_ref[...] = dx
        dxb_ref[...] = dx.astype(BF16)

    return pl.pallas_call(
        body, name="loss_head", grid=(N // tm,),
        out_shape=[jax.ShapeDtypeStruct((8, 128), F32), jax.ShapeDtypeStruct((N, Dm), F32),
                   jax.ShapeDtypeStruct((N, Dm), BF16), jax.ShapeDtypeStruct((1, Dm), F32)],
        in_specs=[_row_spec(tm, Dm, 0), _row_spec(tm, Dm, 0), _whole_spec(gf)],
        out_specs=[pl.BlockSpec((8, 128), lambda i: (0, 0)), _row_spec(tm, Dm, 0), _row_spec(tm, Dm, 0),
                   pl.BlockSpec((1, Dm), lambda i: (0, 0))],
        compiler_params=_cparams(),
    )(x2, target, gf)


def _shift_prev(u):
    rolled = pltpu.roll(u, 1, axis=0)
    row = lax.broadcasted_iota(jnp.int32, u.shape, 0)
    return jnp.where(row == 0, 0.0, rolled)


def _shift_next(u):
    T = u.shape[0]
    rolled = pltpu.roll(u, T - 1, axis=0)
    row = lax.broadcasted_iota(jnp.int32, u.shape, 0)
    return jnp.where(row == T - 1, 0.0, rolled)


_SHIFT_BLOCKS = 7


def _shift_src_col(j):
    return jnp.where(j < 6, 6 + j, 20)


def _token_shift(p, mu_prev, mu_next, B, T):
    def body(p_ref, mp_ref, mn_ref, s_ref):
        u = p_ref[...]
        s_ref[...] = u + mp_ref[...] * (_shift_prev(u) - u) + mn_ref[...] * (_shift_next(u) - u)

    return pl.pallas_call(
        body, name="token_shift", grid=(B, _SHIFT_BLOCKS),
        out_shape=jax.ShapeDtypeStruct((B * T, 1792), F32),
        in_specs=[pl.BlockSpec((T, 256), lambda b, j: (b, _shift_src_col(j))),
                  pl.BlockSpec((1, 256), lambda b, j: (0, j)), pl.BlockSpec((1, 256), lambda b, j: (0, j))],
        out_specs=pl.BlockSpec((T, 256), lambda b, j: (b, j)),
        compiler_params=_cparams(),
    )(p, mu_prev, mu_next)


def _token_shift_bwd(p, ds, mu_prev, mu_next, B, T):
    def body(p_ref, ds_ref, mp_ref, mn_ref, dp_ref, dmp_ref, dmn_ref):
        u, d = p_ref[...], ds_ref[...]
        mp, mn = mp_ref[...], mn_ref[...]
        dp = d * (1.0 - mp - mn) + _shift_next(d * mp) + _shift_prev(d * mn)
        dp_ref[...] = dp.astype(dp_ref.dtype)

        @pl.when(pl.program_id(1) == 0)
        def _():
            dmp_ref[...] = jnp.zeros_like(dmp_ref)
            dmn_ref[...] = jnp.zeros_like(dmn_ref)

        dmp_ref[...] += jnp.sum(d * (_shift_prev(u) - u), axis=0, keepdims=True)
        dmn_ref[...] += jnp.sum(d * (_shift_next(u) - u), axis=0, keepdims=True)

    return pl.pallas_call(
        body, name="token_shift_bwd", grid=(_SHIFT_BLOCKS, B),
        out_shape=[jax.ShapeDtypeStruct((B * T, 1792), BF16), jax.ShapeDtypeStruct((1, 1792), F32),
                   jax.ShapeDtypeStruct((1, 1792), F32)],
        in_specs=[pl.BlockSpec((T, 256), lambda j, b: (b, _shift_src_col(j))),
                  pl.BlockSpec((T, 256), lambda j, b: (b, j)),
                  pl.BlockSpec((1, 256), lambda j, b: (0, j)), pl.BlockSpec((1, 256), lambda j, b: (0, j))],
        out_specs=[pl.BlockSpec((T, 256), lambda j, b: (b, j)), pl.BlockSpec((1, 256), lambda j, b: (0, j)),
                   pl.BlockSpec((1, 256), lambda j, b: (0, j))],
        compiler_params=_cparams(),
    )(p, ds, mu_prev, mu_next)


_FF_BLOCKS = FF_PAD // 256


def _conv3(u, cw, cb):
    return cw[0:1] * _shift_prev(u) + cw[1:2] * u + cw[2:3] * _shift_next(u) + cb


def _ffn_conv(u, cw, cb, B, T):
    def body(ug_ref, uv_ref, cwg_ref, cwv_ref, cbg_ref, cbv_ref, o_ref):
        cg = _conv3(ug_ref[...], cwg_ref[...], cbg_ref[...])
        cv = _conv3(uv_ref[...], cwv_ref[...], cbv_ref[...])
        o_ref[...] = (cg * _sigmoid(cg) * cv).astype(o_ref.dtype)

    nb = _FF_BLOCKS
    return pl.pallas_call(
        body, name="ffn_conv", grid=(B, nb),
        out_shape=jax.ShapeDtypeStruct((B * T, FF_PAD), BF16),
        in_specs=[pl.BlockSpec((T, 256), lambda b, j: (b, j)), pl.BlockSpec((T, 256), lambda b, j: (b, j + nb)),
                  pl.BlockSpec((3, 256), lambda b, j: (0, j)), pl.BlockSpec((3, 256), lambda b, j: (0, j + nb)),
                  pl.BlockSpec((1, 256), lambda b, j: (0, j)), pl.BlockSpec((1, 256), lambda b, j: (0, j + nb))],
        out_specs=pl.BlockSpec((T, 256), lambda b, j: (b, j)),
        compiler_params=_cparams(),
    )(u, u, cw, cw, cb, cb)


def _ffn_conv_bwd(u, dact, cw, cb, B, T):
    def half(u_, dc, cw_):
        du = _shift_next(cw_[0:1] * dc) + cw_[1:2] * dc + _shift_prev(cw_[2:3] * dc)
        dcw = jnp.concatenate([jnp.sum(dc * _shift_prev(u_), axis=0, keepdims=True),
                               jnp.sum(dc * u_, axis=0, keepdims=True),
                               jnp.sum(dc * _shift_next(u_), axis=0, keepdims=True)], axis=0)
        return du, dcw, jnp.sum(dc, axis=0, keepdims=True)

    def body(ug_ref, uv_ref, da_ref, cwg_ref, cwv_ref, cbg_ref, cbv_ref,
             dug_ref, duv_ref, dcwg_ref, dcwv_ref, dcbg_ref, dcbv_ref):
        ug, uv, da = ug_ref[...], uv_ref[...], da_ref[...]
        cwg, cwv = cwg_ref[...], cwv_ref[...]
        cg = _conv3(ug, cwg, cbg_ref[...])
        cv = _conv3(uv, cwv, cbv_ref[...])
        sg = _sigmoid(cg)
        dcv = da * (cg * sg)
        dcg = da * cv * (sg * (1.0 + cg * (1.0 - sg)))
        dug, dcwg, dcbg = half(ug, dcg, cwg)
        duv, dcwv, dcbv = half(uv, dcv, cwv)
        dug_ref[...] = dug.astype(dug_ref.dtype)
        duv_ref[...] = duv.astype(duv_ref.dtype)

        @pl.when(pl.program_id(1) == 0)
        def _():
            for r in (dcwg_ref, dcwv_ref, dcbg_ref, dcbv_ref):
                r[...] = jnp.zeros_like(r)

        dcwg_ref[...] += dcwg
        dcwv_ref[...] += dcwv
        dcbg_ref[...] += dcbg
        dcbv_ref[...] += dcbv

    nb = _FF_BLOCKS
    N = B * T
    res = pl.pallas_call(
        body, name="ffn_conv_bwd", grid=(nb, B),
        out_shape=[jax.ShapeDtypeStruct((N, FF_PAD), BF16), jax.ShapeDtypeStruct((N, FF_PAD), BF16),
                   jax.ShapeDtypeStruct((3, FF_PAD), F32), jax.ShapeDtypeStruct((3, FF_PAD), F32),
                   jax.ShapeDtypeStruct((1, FF_PAD), F32), jax.ShapeDtypeStruct((1, FF_PAD), F32)],
        in_specs=[pl.BlockSpec((T, 256), lambda j, b: (b, j)), pl.BlockSpec((T, 256), lambda j, b: (b, j + nb)),
                  pl.BlockSpec((T, 256), lambda j, b: (b, j)),
                  pl.BlockSpec((3, 256), lambda j, b: (0, j)), pl.BlockSpec((3, 256), lambda j, b: (0, j + nb)),
                  pl.BlockSpec((1, 256), lambda j, b: (0, j)), pl.BlockSpec((1, 256), lambda j, b: (0, j + nb))],
        out_specs=[pl.BlockSpec((T, 256), lambda j, b: (b, j)), pl.BlockSpec((T, 256), lambda j, b: (b, j)),
                   pl.BlockSpec((3, 256), lambda j, b: (0, j)), pl.BlockSpec((3, 256), lambda j, b: (0, j)),
                   pl.BlockSpec((1, 256), lambda j, b: (0, j)), pl.BlockSpec((1, 256), lambda j, b: (0, j))],
        compiler_params=_cparams(),
    )(u, u, dact, cw, cw, cb, cb)
    dug, duv, dcwg, dcwv, dcbg, dcbv = res
    return dug, duv, jnp.concatenate([dcwg, dcwv], axis=1), jnp.concatenate([dcbg, dcbv], axis=1)


def _gla_chunk(q, k, v, afab, wa2, ba, state, rev):
    C = GLA_CHUNK
    n = q.shape[0]
    z = jnp.dot(afab.reshape(n * C, 128), wa2, preferred_element_type=F32).reshape(n, C, 256) + ba
    la = (jnp.minimum(z, 0.0) - jnp.log(1.0 + jnp.exp(-jnp.abs(z)))) * (1.0 / 16.0)
    row = lax.broadcasted_iota(jnp.int32, (n * C, n * C), 0)
    col = lax.broadcasted_iota(jnp.int32, (n * C, n * C), 1)
    ordered = ((col & (C - 1)) - (row & (C - 1))) * (1 - 2 * rev) <= 0
    tri_all = ordered & ((row >> 6) == (col >> 6))
    b = jnp.dot(tri_all.astype(F32), la.reshape(n * C, 256), precision=HIGHEST,
                preferred_element_type=F32).reshape(n, C, 256)
    tri = (lax.broadcasted_iota(jnp.int32, (C, C), 1) - lax.broadcasted_iota(jnp.int32, (C, C), 0)) * (1 - 2 * rev) <= 0
    rows = lax.broadcasted_iota(jnp.int32, (n, C, 256), 1)
    ref_row = jnp.where(rev == 0, C // 2, C - 1 - C // 2)
    last_row = jnp.where(rev == 0, C - 1, 0)
    b_ref = jnp.sum(jnp.where(rows == ref_row, b, 0.0), axis=1, keepdims=True)
    b_last = jnp.sum(jnp.where(rows == last_row, b, 0.0), axis=1, keepdims=True)
    qs = q * 0.125
    qi = qs * jnp.exp(b - b_ref)
    ki = k * jnp.exp(b_ref - b)
    kd = k * jnp.exp(b_last - b)
    qe = qs * jnp.exp(b)
    lane = lax.broadcasted_iota(jnp.int32, (1, 1, 256), 2)
    bdot = lambda x, y, cx, cy: lax.dot_general(x, y, (((cx,), (cy,)), ((0,), (0,))), preferred_element_type=F32)
    outs = []
    upd = jnp.zeros_like(state)
    for h in range(4):
        mh = ((lane >= 64 * h) & (lane < 64 * (h + 1))).astype(F32)
        vh = v[:, :, 128 * h:128 * (h + 1)]
        a = jnp.where(tri, bdot(qi * mh, ki, 2, 2), 0.0)
        outs.append(bdot(a, vh, 2, 1) + bdot(qe, state * mh, 2, 2))
        upd = upd + bdot(vh, kd * mh, 1, 1)
    return jnp.concatenate(outs, axis=2), state * jnp.exp(b_last) + upd


def _gla_chunk_at(nC):
    return lambda d, j: j + d * (nC - 1 - 2 * j)


def _gla_fwd(p, wa2, ba, B, T):
    nC = T // GLA_CHUNK
    N = B * T
    at = _gla_chunk_at(nC)
    p3 = p.reshape(B, T, p.shape[-1])

    def body(q_ref, k_ref, v_ref, af_ref, wa_ref, ba_ref, o_ref, st_ref, state):
        @pl.when(pl.program_id(1) == 0)
        def _():
            state[...] = jnp.zeros_like(state)

        st_ref[0, :, 0] = state[...]
        o, new = _gla_chunk(q_ref[...], k_ref[...], v_ref[...], af_ref[...], wa_ref[0], ba_ref[0], state[...],
                            pl.program_id(0))
        o_ref[0] = o
        state[...] = new

    blk = lambda w, col: pl.BlockSpec((B, 64, w), lambda d, j: (0, at(d, j), col))
    o, st = pl.pallas_call(
        body, name="gla_fwd", grid=(2, nC),
        out_shape=[jax.ShapeDtypeStruct((2, B, T, 512), F32), jax.ShapeDtypeStruct((2, B, nC, 128, 256), F32)],
        in_specs=[blk(256, 0), blk(256, 1), blk(512, 1), blk(128, 42),
                  pl.BlockSpec((1, 128, 256), lambda d, j: (d, 0, 0)),
                  pl.BlockSpec((1, 1, 256), lambda d, j: (d, 0, 0))],
        out_specs=[pl.BlockSpec((1, B, 64, 512), lambda d, j: (d, 0, at(d, j), 0)),
                   pl.BlockSpec((1, B, 1, 128, 256), lambda d, j: (d, 0, at(d, j), 0, 0))],
        scratch_shapes=[pltpu.VMEM((B, 128, 256), F32)],
        compiler_params=_cparams(),
    )(p3, p3, p3, p3, wa2, ba)
    return o.reshape(2, N, 512), st


def _gla_bwd(p, do, states, wa2, ba, B, T):
    nC = T // GLA_CHUNK
    N = B * T
    at_f = _gla_chunk_at(nC)
    at = lambda d, j: at_f(d, nC - 1 - j)
    p3 = p.reshape(B, T, p.shape[-1])

    def body(q_ref, k_ref, v_ref, af_ref, wa_ref, ba_ref, do_ref, st_ref,
             dqkv_ref, daf_ref, dwa_ref, dba_ref, dstate):
        rev = pl.program_id(0)

        @pl.when(pl.program_id(1) == 0)
        def _():
            dstate[...] = jnp.zeros_like(dstate)
            dwa_ref[...] = jnp.zeros_like(dwa_ref)
            dba_ref[...] = jnp.zeros_like(dba_ref)

        f = lambda q, k, v, af, wa, bb, st: _gla_chunk(q, k, v, af, wa, bb, st, rev)
        _, vjp = jax.vjp(f, q_ref[...], k_ref[...], v_ref[...], af_ref[...], wa_ref[0], ba_ref[0], st_ref[0, :, 0])
        dq, dk, dv, daf, dwa, dba, dst = vjp((do_ref[...], dstate[...]))
        dqkv_ref[0] = jnp.concatenate([dq, dk, dv], axis=2)
        daf_ref[0] = daf
        dwa_ref[0] += dwa
        dba_ref[0] += dba
        dstate[...] = dst

    blk = lambda w, col: pl.BlockSpec((B, 64, w), lambda d, j: (0, at(d, j), col))
    out4 = lambda w: pl.BlockSpec((1, B, 64, w), lambda d, j: (d, 0, at(d, j), 0))
    dqkv, daf, dwa, dba = pl.pallas_call(
        body, name="gla_bwd", grid=(2, nC),
        out_shape=[jax.ShapeDtypeStruct((2, B, T, 1024), F32), jax.ShapeDtypeStruct((2, B, T, 128), F32),
                   jax.ShapeDtypeStruct((2, 128, 256), F32), jax.ShapeDtypeStruct((2, 1, 256), F32)],
        in_specs=[blk(256, 0), blk(256, 1), blk(512, 1), blk(128, 42),
                  pl.BlockSpec((1, 128, 256), lambda d, j: (d, 0, 0)),
                  pl.BlockSpec((1, 1, 256), lambda d, j: (d, 0, 0)),
                  blk(512, 0),
                  pl.BlockSpec((1, B, 1, 128, 256), lambda d, j: (d, 0, at(d, j), 0, 0))],
        out_specs=[out4(1024), out4(128),
                   pl.BlockSpec((1, 128, 256), lambda d, j: (d, 0, 0)),
                   pl.BlockSpec((1, 1, 256), lambda d, j: (d, 0, 0))],
        scratch_shapes=[pltpu.VMEM((B, 128, 256), F32)],
        compiler_params=_cparams(),
    )(p3, p3, p3, p3, wa2, ba, do.reshape(B, T, 512), states)
    return dqkv.reshape(2, N, 1024), daf.reshape(2, N, 128), dwa, dba


def _seg_ones():
    m = lax.broadcasted_iota(jnp.int32, (256, 128), 0)
    n = lax.broadcasted_iota(jnp.int32, (256, 128), 1)
    return (((m >> 6) & 1) == (n >> 6)).astype(BF16)


def _seg_mm(x, ones2):
    hi = x.astype(BF16)
    lo = (x - hi.astype(F32)).astype(BF16)
    return jnp.dot(jnp.concatenate([hi, lo], axis=1), ones2, preferred_element_type=F32)


def _diag_matrix():
    r = np.arange(2048)[:, None] % 64
    c = np.arange(128)[None, :] % 64
    return jnp.asarray((r == c).astype(np.float32))


def _cols8(tile, dg, ones2):
    return _seg_mm(jnp.concatenate([_rows4(tile, q) for q in range(8)], axis=0) * dg[...], ones2)


def _rows4(tile, q):
    return jnp.concatenate([jnp.broadcast_to(tile[q:q + 1, 128 * p:128 * (p + 1)], (64, 128)) for p in range(4)],
                           axis=0)


def _head_rows():
    r = lax.broadcasted_iota(jnp.int32, (16, 256), 0)
    n = lax.broadcasted_iota(jnp.int32, (16, 256), 1)
    return (r == ((n >> 6) & 1)).astype(BF16)


def _head_sums_row(x, heads2):
    hi = x.astype(BF16)
    lo = (x - hi.astype(F32)).astype(BF16)
    out = lax.dot_general(heads2, jnp.concatenate([hi, lo], axis=1), (((1,), (1,)), ((), ())),
                          preferred_element_type=F32)
    return jnp.concatenate([out[h:h + 1, 64 * p:64 * (p + 1)] for p in range(4) for h in range(2)], axis=1)


def _colsum4(m):
    return jnp.concatenate([jnp.sum(m[64 * p:64 * (p + 1)], axis=0, keepdims=True) for p in range(4)], axis=1)


def _time_base(gi, n_groups, rev):
    return pl.multiple_of(((n_groups - 1 - gi) if rev else gi) * 8, 8)


def _scan_fwd_mxu(s, wf, wb, k2, na, bb, B, T):
    Tc = SCAN_CHUNK
    nT = T // Tc
    nG = Tc // 8
    N = B * T
    nb = _pick(B, (4, 2))
    fwd_j = lambda j: j
    bwd_j = lambda j: nT - 1 - j

    def body(rF, vF, kF, aF, bF, wF, rB, vB, kB, aB, bB, wB, dg, yF, yB, ckF, ckB,
             SF, SB, vcF, vcB, ypF, ypB, ytF, ytB):
        @pl.when(pl.program_id(1) == 0)
        def _():
            SF[...] = jnp.zeros_like(SF)
            SB[...] = jnp.zeros_like(SB)

        ckF[...] = SF[...]
        ckB[...] = SB[...]
        ones2, heads2 = _seg_ones(), _head_rows()
        chains = []
        for n in range(nb):
            chains.append((n, SF, (rF, vF, kF, aF, bF, wF), yF, vcF, ypF, ytF, False))
            chains.append((n, SB, (rB, vB, kB, aB, bB, wB), yB, vcB, ypB, ytB, True))

        def group(gi, carry):
            tiles, states = [], []
            for n, S_ref, refs, _, vc, _, _, rev in chains:
                base = _time_base(gi, nG, rev)
                t = [ref[n, pl.ds(base, 8), :] for ref in refs]
                tiles.append(t)
                states.append(S_ref[n])
                vc[n] = _cols8(t[1], dg, ones2)
            for i8 in range(8):
                for c, (n, _, _, _, vc, yp, _, rev) in enumerate(chains):
                    q = 7 - i8 if rev else i8
                    r, v, k, a, b, w = tiles[c]
                    S = states[c]
                    sa = _seg_mm(S * _rows4(a, q), ones2)
                    S = S * _rows4(w, q) + sa * _rows4(b, q) + vc[n, 256 * q:256 * (q + 1), :] * _rows4(k, q)
                    yp[n, 256 * q:256 * (q + 1), :] = S * _rows4(r, q)
                    states[c] = S
            for c, (n, S_ref, _, y_ref, _, yp, yt, rev) in enumerate(chains):
                S_ref[n] = states[c]
                for q in range(8):
                    yt[n, q:q + 1, :] = _head_sums_row(yp[n, 256 * q:256 * (q + 1), :], heads2)
                y_ref[n, pl.ds(_time_base(gi, nG, rev), 8), :] = yt[n]
            return carry

        lax.fori_loop(0, nG, group, 0)

    row_in = lambda at, col: pl.BlockSpec((nb, Tc, 512), lambda g, j: (g, at(j), col))
    state_io = lambda at: pl.BlockSpec((nb, 256, 128), lambda g, j: (g, at(j), 0))
    in_specs = []
    for at in (fwd_j, bwd_j):
        in_specs += [row_in(at, 0), row_in(at, 2)] + [row_in(at, 0)] * 4
    big = pltpu.VMEM((nb, 8 * 256, 128), F32)
    s3 = s.reshape(B, T, s.shape[-1])
    seq = lambda a: a.reshape(B, T, 512)
    y_f, y_b, ck_f, ck_b = pl.pallas_call(
        body, name="rwkv_scan", grid=(B // nb, nT),
        out_shape=[jax.ShapeDtypeStruct((B, T, 512), F32), jax.ShapeDtypeStruct((B, T, 512), F32),
                   jax.ShapeDtypeStruct((B, nT * 256, 128), F32), jax.ShapeDtypeStruct((B, nT * 256, 128), F32)],
        in_specs=in_specs + [pl.BlockSpec((2048, 128), lambda g, j: (0, 0))],
        out_specs=[row_in(fwd_j, 0), row_in(bwd_j, 0), state_io(fwd_j), state_io(bwd_j)],
        scratch_shapes=[pltpu.VMEM((nb, 256, 128), F32), pltpu.VMEM((nb, 256, 128), F32), big, big, big, big,
                        pltpu.VMEM((nb, 8, 512), F32), pltpu.VMEM((nb, 8, 512), F32)],
        compiler_params=_cparams(),
    )(s3, s3, seq(k2), seq(na), seq(bb), seq(wf), s3, s3, seq(k2), seq(na), seq(bb), seq(wb), _diag_matrix())
    ck_shape = (B * nT * 256, 128)
    return (y_f.reshape(N, 512), y_b.reshape(N, 512),
            ck_f.reshape(ck_shape), ck_b.reshape(ck_shape))


def _scan_bwd_mxu(s, wf, wb, k2, na, bb, dy, ckF, ckB, B, T):
    Tc = SCAN_CHUNK
    nT = T // Tc
    nG = Tc // 8
    N = B * T
    nb = _pick(B, (2,))
    f_at = lambda j: nT - 1 - j
    b_at = lambda j: j
    n_in, n_out, n_scr = 17, 12, 12

    def body(*refs):
        (rF, vF, kF, aF, bF, wF, dyF, ckF_ref, rB, vB, kB, aB, bB, wB, dyB, ckB_ref, dg) = refs[:n_in]
        outsF, outsB = refs[n_in:n_in + 6], refs[n_in + 6:n_in + n_out]
        chains = []
        for n in range(nb):
            stF, stB, saF, saB, vcF, vcB, dSF, dSB, bigF, bigB, tileF, tileB = \
                refs[n_in + n_out + n_scr * n:n_in + n_out + n_scr * (n + 1)]
            chains.append((n, stF, dSF, ckF_ref, (rF, vF, kF, aF, bF, wF, dyF), outsF, bigF, tileF, False, saF, vcF))
            chains.append((n, stB, dSB, ckB_ref, (rB, vB, kB, aB, bB, wB, dyB), outsB, bigB, tileB, True, saB, vcB))

        @pl.when(pl.program_id(1) == 0)
        def _():
            for chain in chains:
                chain[2][...] = jnp.zeros_like(chain[2])

        ones2, heads2 = _seg_ones(), _head_rows()
        for chain in chains:
            chain[1][0] = chain[3][chain[0]]

        cols8 = lambda tile: _cols8(tile, dg, ones2)

        def recompute(gi, carry):
            tiles, states = [], []
            for n, st, _, _, ins, _, big, _, rev, _, vc_keep in chains:
                base = _time_base(gi, nG, rev)
                t = [ref[n, pl.ds(base, 8), :] for ref in ins[1:6]]
                tiles.append(t)
                states.append(st[gi * 8])
                v_cols = cols8(t[0])
                for i8 in range(8):
                    q = 7 - i8 if rev else i8
                    vc_keep[gi * 8 + i8] = v_cols[256 * q:256 * (q + 1)]
            for i8 in range(8):
                for c, (_, st, _, _, _, _, _, _, rev, sa_keep, vc_keep) in enumerate(chains):
                    q = 7 - i8 if rev else i8
                    v, k, a, b, w = tiles[c]
                    S = states[c]
                    sa = _seg_mm(S * _rows4(a, q), ones2)
                    sa_keep[gi * 8 + i8] = sa
                    S = S * _rows4(w, q) + sa * _rows4(b, q) + vc_keep[gi * 8 + i8] * _rows4(k, q)
                    st[gi * 8 + i8 + 1] = S
                    states[c] = S
            return carry

        lax.fori_loop(0, nG, recompute, 0)

        def back(gg, carry):
            gi = nG - 1 - gg
            tiles, grads = [], []
            for n, st, dS_ref, _, ins, _, big, _, rev, _, _ in chains:
                base = _time_base(gi, nG, rev)
                t = [ref[n, pl.ds(base, 8), :] for ref in ins]
                tiles.append(t)
                grads.append(dS_ref[...])
                big[0] = cols8(t[6])
            for i8 in range(7, -1, -1):
                for c, (_, st, _, _, _, _, big, tile, rev, sa_keep, vc_keep) in enumerate(chains):
                    q = 7 - i8 if rev else i8
                    r, v, k, a, b, w, _ = tiles[c]
                    i = gi * 8 + i8
                    S_prev, S_t = st[i], st[i + 1]
                    rows = slice(256 * q, 256 * (q + 1))
                    dy_col, v_col, sa = big[0, rows, :], vc_keep[i], sa_keep[i]
                    dS = grads[c] + dy_col * _rows4(r, q)
                    sb = _seg_mm(dS * _rows4(b, q), ones2)
                    big[1, rows, :] = dS * _rows4(k, q)
                    tile[0, q:q + 1, :] = _colsum4(S_t * dy_col)
                    tile[2, q:q + 1, :] = _colsum4(dS * v_col)
                    tile[3, q:q + 1, :] = _colsum4(S_prev * sb)
                    tile[4, q:q + 1, :] = _colsum4(dS * sa)
                    tile[5, q:q + 1, :] = _colsum4(S_prev * dS)
                    grads[c] = dS * _rows4(w, q) + sb * _rows4(a, q)
            for c, (n, _, dS_ref, _, _, outs, big, tile, rev, _, _) in enumerate(chains):
                dS_ref[...] = grads[c]
                for q in range(8):
                    tile[1, q:q + 1, :] = _head_sums_row(big[1, 256 * q:256 * (q + 1), :], heads2)
                base = _time_base(gi, nG, rev)
                for o, o_ref in enumerate(outs):
                    o_ref[n, pl.ds(base, 8), :] = tile[o]
            return carry

        lax.fori_loop(0, nG, back, 0)

    row_io = lambda at, col: pl.BlockSpec((nb, Tc, 512), lambda g, j: (g, at(j), col))
    in_specs = []
    for at in (f_at, b_at):
        in_specs += [row_io(at, 0), row_io(at, 2)] + [row_io(at, 0)] * 5
        in_specs.append(pl.BlockSpec((nb, 256, 128), lambda g, j, at=at: (g, at(j), 0)))
    in_specs.append(pl.BlockSpec((2048, 128), lambda g, j: (0, 0)))
    out_specs = [row_io(f_at, 0)] * 6 + [row_io(b_at, 0)] * 6
    big = pltpu.VMEM((2, 8 * 256, 128), F32)
    states = pltpu.VMEM((Tc + 1, 256, 128), F32)
    per_step = pltpu.VMEM((Tc, 256, 128), F32)
    one_slot = [states, states, per_step, per_step, per_step, per_step,
                pltpu.VMEM((256, 128), F32), pltpu.VMEM((256, 128), F32), big, big,
                pltpu.VMEM((6, 8, 512), F32), pltpu.VMEM((6, 8, 512), F32)]
    s3 = s.reshape(B, T, s.shape[-1])
    seq = lambda a: a.reshape(B, T, 512)
    ck3 = lambda a: a.reshape(B, nT * 256, 128)
    outs = pl.pallas_call(
        body, name="rwkv_scan_bwd", grid=(B // nb, nT),
        out_shape=[jax.ShapeDtypeStruct((B, T, 512), F32)] * 12,
        in_specs=in_specs, out_specs=out_specs,
        scratch_shapes=one_slot * nb,
        compiler_params=_cparams(),
    )(s3, s3, seq(k2), seq(na), seq(bb), seq(wf), seq(dy), ck3(ckF),
      s3, s3, seq(k2), seq(na), seq(bb), seq(wb), seq(dy), ck3(ckB), _diag_matrix())
    return [o.reshape(N, 512) for o in outs]


def _cat_shards(g4, name, axis):
    return jnp.concatenate([g4[s][name] for s in range(4)], axis=axis)


def _split_shards(full, axis):
    return jnp.split(full, 4, axis=axis)


def kernel(x, norm1_g, w_in, gla_wa2_f, gla_ba_f, gla_wa2_b, gla_ba_b, gla_norm_g, gla_proj, rwkv_mu_prev, rwkv_mu_next, rwkv_w0_f, rwkv_w2_f, rwkv_w0_b, rwkv_w2_b, rwkv_a0, rwkv_a2, rwkv_g2, rwkv_k_k, rwkv_k_a, rwkv_r_k, rwkv_ln_w, rwkv_ln_b, rwkv_proj, w_out, norm2_g, ffn_up, ffn_conv_w, ffn_conv_b, ffn_down, norm_f_g, loss_target, m_norm1_g, m_w_in, m_gla_wa2_f, m_gla_ba_f, m_gla_wa2_b, m_gla_ba_b, m_gla_norm_g, m_gla_proj, m_rwkv_mu_prev, m_rwkv_mu_next, m_rwkv_w0_f, m_rwkv_w2_f, m_rwkv_w0_b, m_rwkv_w2_b, m_rwkv_a0, m_rwkv_a2, m_rwkv_g2, m_rwkv_k_k, m_rwkv_k_a, m_rwkv_r_k, m_rwkv_ln_w, m_rwkv_ln_b, m_rwkv_proj, m_w_out, m_norm2_g, m_ffn_up, m_ffn_conv_w, m_ffn_conv_b, m_ffn_down, m_norm_f_g, v_norm1_g, v_w_in, v_gla_wa2_f, v_gla_ba_f, v_gla_wa2_b, v_gla_ba_b, v_gla_norm_g, v_gla_proj, v_rwkv_mu_prev, v_rwkv_mu_next, v_rwkv_w0_f, v_rwkv_w2_f, v_rwkv_w0_b, v_rwkv_w2_b, v_rwkv_a0, v_rwkv_a2, v_rwkv_g2, v_rwkv_k_k, v_rwkv_k_a, v_rwkv_r_k, v_rwkv_ln_w, v_rwkv_ln_b, v_rwkv_proj, v_w_out, v_norm2_g, v_ffn_up, v_ffn_conv_w, v_ffn_conv_b, v_ffn_down, v_norm_f_g):
    args = locals()
    weights = {n: args[n] for n in WEIGHT_ORDER}
    mom_m = {n: args["m_" + n] for n in WEIGHT_ORDER}
    mom_v = {n: args["v_" + n] for n in WEIGHT_ORDER}
    shapes = {n: weights[n].shape for n in WEIGHT_ORDER}
    B, T, _ = x.shape
    N = B * T
    tm = _pick(N, (512,))

    def local(d):
        sh = {n: d[n].reshape(s) for n, s, _ in SHARDED}
        rp = {n: d[n].reshape(-1) for n, _ in REPLICATED}
        return sh, rp

    w_loc, m_loc, v_loc = local(weights), local(mom_m), local(mom_v)

    small_of = lambda loc: _pack(loc[0], _pack_replicated(loc[1]))
    w_small = small_of(w_loc)
    gathered = _allgather_chips([w_loc[0][n].astype(BF16) for n, _, _ in BIG] + [w_small])
    small_vals = [_unpack(gathered[-1][s]) for s in range(4)]
    W = {n: jnp.concatenate([gathered[i][s] for s in range(4)], axis=ax) for i, (n, _, ax) in enumerate(BIG)}
    W.update({n: _cat_shards(small_vals, n, ax) for n, _, ax in SMALL})
    R = {n: weights[n].reshape(1, -1) for n, _ in REPLICATED}

    zc = lambda r, c, dt=F32: jnp.zeros((r, c), dt)
    w_in_full = W["w_in"]
    w_in_p = jnp.concatenate([w_in_full[:, 0:1536], w_in_full[:, 1568:3104], w_in_full[:, 3360:5408],
                              w_in_full[:, 3104:3360], w_in_full[:, 1536:1568],
                              zc(1024, PROJ_PAD - N_PROJ, BF16)], axis=1)
    w_in_b = w_in_p
    pad_ff = lambda a: jnp.concatenate([a[:, :D_FF], zc(a.shape[0], FF_PAD - D_FF, a.dtype), a[:, D_FF:],
                                        zc(a.shape[0], FF_PAD - D_FF, a.dtype)], axis=1)
    ffn_up_p = pad_ff(W["ffn_up"])
    ffn_up_b = ffn_up_p
    conv_w_p = pad_ff(W["ffn_conv_w"])
    conv_b_p = pad_ff(R["ffn_conv_b"])
    ffn_down_p = jnp.concatenate([W["ffn_down"], zc(FF_PAD - D_FF, 1024, BF16)], axis=0)
    ffn_down_b = ffn_down_p
    w_out_b = W["w_out"]
    gla_proj_b = W["gla_proj"]
    rwkv_proj_b = W["rwkv_proj"]
    wa2 = jnp.stack([jnp.concatenate([W["gla_wa2_f"], zc(112, 256)], axis=0),
                     jnp.concatenate([zc(16, 256), W["gla_wa2_b"], zc(96, 256)], axis=0)])
    ba = jnp.stack([R["gla_ba_f"], R["gla_ba_b"]])
    w2_f = jnp.concatenate([W["rwkv_w2_f"], zc(64, 512)], axis=0)
    w2_b = jnp.concatenate([W["rwkv_w2_b"], zc(64, 512)], axis=0)
    a2 = jnp.concatenate([zc(64, 512), W["rwkv_a2"]], axis=0)
    g2 = W["rwkv_g2"]
    head_ones = np.kron(np.eye(8, dtype=np.float32), np.ones((64, 64), np.float32))
    seg64 = jnp.asarray(np.concatenate([head_ones, head_ones], axis=0), dtype=BF16)

    x2d = x.reshape(N, D_MODEL)
    tgt = loss_target.reshape(N, D_MODEL)

    (h1,) = _rowwise("norm1", _fn_norm, [(x2d, 1024, 0)], [R["norm1_g"]], [], [(1024, BF16)], N, tm)
    p = _matmul(h1, w_in_b, "proj_in")
    o_gla, gla_states = _gla_fwd(p, wa2, ba, B, T)
    gla_post_rows = [(o_gla, 512, 0, 0), (o_gla, 512, 0, 1), (p, 512, 2)]
    (gated,) = _rowwise("gla_post", _fn_gla_post, gla_post_rows, [R["gla_norm_g"]], [], [(512, BF16)], N, tm)
    y_a = _matmul(gated, gla_proj_b, "gla_out")
    s = _token_shift(p, R["rwkv_mu_prev"], R["rwkv_mu_next"], B, T)
    pre_rows = [(s, 512, 1), (s, 256, 6)]
    pre_params = [R["rwkv_w0_f"], R["rwkv_w0_b"], R["rwkv_a0"], R["rwkv_k_k"], R["rwkv_k_a"], w2_f, w2_b, a2, g2]
    wf, wb, k2, na, bb, g = _rowwise("rwkv_pre", _fn_rwkv_pre, pre_rows, pre_params, [seg64],
                                     [(512, F32)] * 6, N, tm)
    y_f, y_b, ck_f, ck_b = _scan_fwd_mxu(s, wf, wb, k2, na, bb, B, T)
    post_rows = [(y_f, 512, 0), (y_b, 512, 0), (s, 512, 0), (k2, 512, 0), (s, 512, 2), (g, 512, 0)]
    post_params = [R["rwkv_ln_w"], R["rwkv_ln_b"], R["rwkv_r_k"]]
    (o_rwkv,) = _rowwise("rwkv_post", _fn_rwkv_post, post_rows, post_params, [seg64], [(512, BF16)], N, tm)
    y_r = _matmul(o_rwkv, rwkv_proj_b, "rwkv_out")
    merge_rows = [(p, 1024, 3), (p, 1024, 4), (y_a, 1024, 0), (y_r, 1024, 0)]
    (merged,) = _rowwise("merge", _fn_merge, merge_rows, [], [], [(1024, BF16)], N, tm)
    x1 = _matmul(merged, w_out_b, "mix_out", residual=x2d)
    (h2,) = _rowwise("norm2", _fn_norm, [(x1, 1024, 0)], [R["norm2_g"]], [], [(1024, BF16)], N, tm)
    u = _matmul(h2, ffn_up_b, "ffn_up")
    act = _ffn_conv(u, conv_w_p, conv_b_p, B, T)
    x2 = _matmul(act, ffn_down_b, "ffn_down", residual=x1)
    loss_blk, dx2, dx2_b, d_norm_f = _loss_head(x2, tgt, weights["norm_f_g"].reshape(1, -1))

    d_act = _matmul(dx2_b, ffn_down_p.T.astype(BF16), "d_act")
    d_ffn_down = _matmul_tn(act, dx2_b, "dw_ffn_down")[:D_FF]
    du_g, du_v, d_conv_w_p, d_conv_b_p = _ffn_conv_bwd(u, d_act, conv_w_p, conv_b_p, B, T)
    up_t = ffn_up_p.T.astype(BF16)
    d_h2 = _matmul(du_v, up_t[FF_PAD:], "d_h2_v", residual=_matmul(du_g, up_t[:FF_PAD], "d_h2_g"))
    d_ffn_up = jnp.concatenate([_matmul_tn(h2, du_g, "dw_ffn_up_g")[:, :D_FF],
                                _matmul_tn(h2, du_v, "dw_ffn_up_v")[:, :D_FF]], axis=1)
    unpad_ff = lambda a: jnp.concatenate([a[:, :D_FF], a[:, FF_PAD:FF_PAD + D_FF]], axis=1)
    (dx1,), (d_norm2,) = _rowwise_bwd("norm2_bwd", _fn_norm, [(x1, 1024, 0)], [R["norm2_g"]], [],
                                      [[(d_h2, 1024, 0)]], [(F32, (dx2, 1024, 0))], N, tm)
    dx1_b = dx1.astype(BF16)
    d_merged = _matmul(dx1_b, W["w_out"].T.astype(BF16), "d_merged")
    d_w_out = _matmul_tn(merged, dx1_b, "dw_out")
    (d_ga, d_gb, d_ya, d_yr), _ = _rowwise_bwd("merge_bwd", _fn_merge, merge_rows, [], [],
                                               [[(d_merged, 1024, 0)]], [(BF16, None)] * 4, N, tm)
    d_o_rwkv = _matmul(d_yr, W["rwkv_proj"].T.astype(BF16), "d_o_rwkv")
    d_rwkv_proj = _matmul_tn(o_rwkv, d_yr, "dw_rwkv_proj")
    (d_y, d_r_bonus, d_k2_bonus, d_v_bonus, d_g), (d_ln_w, d_ln_b, d_r_k) = _rowwise_bwd(
        "rwkv_post_bwd", _fn_rwkv_post, post_rows, post_params, [seg64], [[(d_o_rwkv, 512, 0)]],
        [(F32, None), None, (F32, None), (F32, None), (F32, None), (F32, None)], N, tm)
    (drF, dvF, dkF, daF, dbF, dwF, drB, dvB, dkB, daB, dbB, dwB) = _scan_bwd_mxu(s, wf, wb, k2, na, bb, d_y, ck_f, ck_b, B, T)
    pre_cts = [[(dwF, 512, 0)], [(dwB, 512, 0)], [(dkF, 512, 0), (dkB, 512, 0), (d_k2_bonus, 512, 0)],
               [(daF, 512, 0), (daB, 512, 0)], [(dbF, 512, 0), (dbB, 512, 0)], [(d_g, 512, 0)]]
    (ds_k, ds_wag), pre_grads = _rowwise_bwd("rwkv_pre_bwd", _fn_rwkv_pre, pre_rows, pre_params, [seg64], pre_cts,
                                             [(F32, None), (F32, None)], N, tm)
    d_w0_f, d_w0_b, d_a0, d_k_k, d_k_a, d_w2_f, d_w2_b, d_a2, d_g2 = pre_grads
    ds = jnp.concatenate([drF + drB + d_r_bonus, ds_k, dvF + dvB + d_v_bonus, ds_wag], axis=1)
    d_p_rwkv, d_mu_prev, d_mu_next = _token_shift_bwd(p, ds, R["rwkv_mu_prev"], R["rwkv_mu_next"], B, T)
    d_gated = _matmul(d_ya, W["gla_proj"].T.astype(BF16), "d_gated")
    d_gla_proj = _matmul_tn(gated, d_ya, "dw_gla_proj")
    (d_o, d_og), (d_gla_norm,) = _rowwise_bwd(
        "gla_post_bwd", _fn_gla_post, gla_post_rows, [R["gla_norm_g"]], [], [[(d_gated, 512, 0)]],
        [(F32, None), None, (BF16, None)], N, tm)
    dqkv2, dafab2, d_wa2, d_ba = _gla_bwd(p, d_o, gla_states, wa2, ba, B, T)
    add2 = lambda a, b: (a + b,)
    (d_qkv,) = _rowwise("sum_dqkv", add2, [(dqkv2, 1024, 0, 0), (dqkv2, 1024, 0, 1)], [], [], [(1024, BF16)], N, tm)
    (d_afab,) = _rowwise("sum_dafab", add2, [(dafab2, 128, 0, 0), (dafab2, 128, 0, 1)], [], [], [(128, BF16)], N, tm)
    w_in_t = w_in_p.T.astype(BF16)
    w_rwkv_t = jnp.concatenate([w_in_t[1536:3072], w_in_t[5120:5376]], axis=0)
    d_h1 = _matmul(d_qkv, w_in_t[0:1024], "d_h1_qkv")
    d_h1 = _matmul(d_og, w_in_t[1024:1536], "d_h1_og", residual=d_h1)
    d_h1 = _matmul(d_p_rwkv, w_rwkv_t, "d_h1_rwkv", residual=d_h1)
    d_h1 = _matmul(d_ga, w_in_t[3072:4096], "d_h1_ga", residual=d_h1)
    d_h1 = _matmul(d_gb, w_in_t[4096:5120], "d_h1_gb", residual=d_h1)
    d_h1 = _matmul(d_afab, w_in_t[5376:5504], "d_h1_afab", residual=d_h1)
    d_w_in = jnp.concatenate([
        _matmul_tn(h1, d_qkv, "dw_in_qkv"), _matmul_tn(h1, d_og, "dw_in_og"),
        _matmul_tn(h1, d_afab, "dw_in_afab")[:, :32], _matmul_tn(h1, d_p_rwkv, "dw_in_rwkv"),
        _matmul_tn(h1, d_ga, "dw_in_ga"), _matmul_tn(h1, d_gb, "dw_in_gb")], axis=1)
    (grad_x,), (d_norm1,) = _rowwise_bwd("norm1_bwd", _fn_norm, [(x2d, 1024, 0)], [R["norm1_g"]], [],
                                         [[(d_h1, 1024, 0)]], [(F32, (dx1, 1024, 0))], N, tm)

    full_grads = {
        "w_in": d_w_in, "gla_wa2_f": d_wa2[0, 0:16], "gla_wa2_b": d_wa2[1, 16:32], "gla_proj": d_gla_proj,
        "rwkv_w2_f": d_w2_f[0:64], "rwkv_w2_b": d_w2_b[0:64], "rwkv_a2": d_a2[64:128], "rwkv_g2": d_g2,
        "rwkv_proj": d_rwkv_proj, "w_out": d_w_out, "ffn_up": d_ffn_up, "ffn_conv_w": unpad_ff(d_conv_w_p),
        "ffn_down": d_ffn_down,
    }
    repl_grads = {
        "norm1_g": d_norm1, "gla_ba_f": d_ba[0], "gla_ba_b": d_ba[1], "gla_norm_g": d_gla_norm,
        "rwkv_mu_prev": d_mu_prev, "rwkv_mu_next": d_mu_next, "rwkv_w0_f": d_w0_f, "rwkv_w0_b": d_w0_b,
        "rwkv_a0": d_a0, "rwkv_k_k": d_k_k, "rwkv_k_a": d_k_a, "rwkv_r_k": d_r_k, "rwkv_ln_w": d_ln_w,
        "rwkv_ln_b": d_ln_b, "norm2_g": d_norm2, "ffn_conv_b": unpad_ff(d_conv_b_p), "norm_f_g": d_norm_f,
    }
    split = {n: _split_shards(full_grads[n], ax) for n, _, ax in SHARDED}
    to_owners = [jnp.stack([p.astype(BF16) for p in split[n]]) for n, _, _ in BIG]
    repl_flat = _pack_replicated(repl_grads, loss_blk[0, 0])
    to_owners.append(jnp.stack([_pack({n: split[n][sidx] for n, _, _ in SMALL}, repl_flat) for sidx in range(4)]))
    received = _exchange_chips(to_owners, "scatter_grads", gather=False)
    names = [n for n, _, _ in BIG] + ["small"]
    mine = [_sum_sources(r, "sum_" + n) for r, n in zip(received, names)]
    other = _swap_with_sibling(mine)
    packs = [w_small, small_of(m_loc), small_of(v_loc)]
    results = {}
    for i, n in enumerate(names):
        wmv = packs if n == "small" else [d[0][n] for d in (w_loc, m_loc, v_loc)]
        results[n] = _adamw(mine[i], other[i], *wmv, "adamw_" + n)
    small = [_unpack(f) for f in results["small"]]
    outs = [small[0]["loss"], grad_x.reshape(B, T, D_MODEL)]
    for kind in range(4):
        for n in WEIGHT_ORDER:
            val = results[n][kind] if n in results else small[kind][n]
            outs.append(val.reshape(shapes[n]))
    return tuple(outs)
```

```python
import functools

import jax
import jax.numpy as jnp
import numpy as np
from jax import lax
from jax.experimental import pallas as pl
from jax.experimental.pallas import tpu as pltpu

F32 = jnp.float32
BF16 = jnp.bfloat16
HIGHEST = lax.Precision.HIGHEST
MESH_IDS = pl.DeviceIdType.MESH

D_MODEL = 1024
N_PROJ = 5408
PROJ_PAD = 5632
D_FF = 2752
FF_PAD = 2816
GLA_CHUNK = 64
SCAN_CHUNK = 16
NORM_EPS = 1e-6
HEAD_NORM_EPS = 1e-5
RWKV_GN_EPS = 64 * 1e-5
ADAM_LR, ADAM_B1, ADAM_B2, ADAM_EPS, ADAM_WD, ADAM_STEP = 0.001, 0.9, 0.999, 1e-08, 0.01, 10
VMEM_LIMIT = 56 * 1024 * 1024

FLAT_ROWS, FLAT_COLS = 128, 1024
BIG = (
    ("w_in", (1024, 1352), 1), ("gla_proj", (512, 256), 1), ("rwkv_proj", (512, 256), 1),
    ("w_out", (256, 1024), 0), ("ffn_up", (1024, 1376), 1), ("ffn_down", (688, 1024), 0),
)
SMALL = (
    ("gla_wa2_f", (16, 64), 1), ("gla_wa2_b", (16, 64), 1), ("rwkv_w2_f", (64, 128), 1),
    ("rwkv_w2_b", (64, 128), 1), ("rwkv_a2", (64, 128), 1), ("rwkv_g2", (128, 128), 1),
    ("ffn_conv_w", (3, 1376), 1),
)
SHARDED = BIG + SMALL
REPLICATED = (
    ("norm1_g", 1024), ("gla_ba_f", 256), ("gla_ba_b", 256), ("gla_norm_g", 512),
    ("rwkv_mu_prev", 1792), ("rwkv_mu_next", 1792), ("rwkv_w0_f", 512), ("rwkv_w0_b", 512),
    ("rwkv_a0", 512), ("rwkv_k_k", 512), ("rwkv_k_a", 512), ("rwkv_r_k", 512),
    ("rwkv_ln_w", 512), ("rwkv_ln_b", 512), ("norm2_g", 1024), ("ffn_conv_b", 5504),
    ("norm_f_g", 1024),
)
WEIGHT_ORDER = ("norm1_g", "w_in", "gla_wa2_f", "gla_ba_f", "gla_wa2_b", "gla_ba_b", "gla_norm_g", "gla_proj",
                "rwkv_mu_prev", "rwkv_mu_next", "rwkv_w0_f", "rwkv_w2_f", "rwkv_w0_b", "rwkv_w2_b", "rwkv_a0",
                "rwkv_a2", "rwkv_g2", "rwkv_k_k", "rwkv_k_a", "rwkv_r_k", "rwkv_ln_w", "rwkv_ln_b", "rwkv_proj",
                "w_out", "norm2_g", "ffn_up", "ffn_conv_w", "ffn_conv_b", "ffn_down", "norm_f_g")


def _cparams(**kw):
    return pltpu.CompilerParams(vmem_limit_bytes=VMEM_LIMIT, **kw)


def _pack_replicated(repl_vals, loss=None):
    parts = [repl_vals[n].reshape(-1) for n, _ in REPLICATED]
    parts.append(jnp.zeros((1,), F32) if loss is None else loss.reshape(1))
    return jnp.concatenate(parts)


def _pack(sharded_vals, repl_flat):
    parts = [sharded_vals[n].reshape(-1) for n, _, _ in SMALL] + [repl_flat]
    used = sum(int(np.prod(s)) for _, s, _ in SMALL) + sum(w for _, w in REPLICATED) + 1
    parts.append(jnp.zeros((FLAT_ROWS * FLAT_COLS - used,), F32))
    return jnp.concatenate(parts).reshape(FLAT_ROWS, FLAT_COLS)


def _unpack(flat):
    v = flat.reshape(-1)
    out, off = {}, 0
    for n, s, _ in SMALL:
        k = int(np.prod(s))
        out[n] = v[off:off + k].reshape(s)
        off += k
    for n, w in REPLICATED:
        out[n] = v[off:off + w]
        off += w
    out["loss"] = v[off]
    return out


def _chip_peers():
    x, y, c = lax.axis_index("x"), lax.axis_index("y"), lax.axis_index("c")
    return x, y, c, ((1 - x, y), (x, 1 - y), (1 - x, 1 - y))


def _exchange_chips(arrs, name, gather):
    n = len(arrs)

    def body(*refs):
        srcs, outs = refs[:n], refs[n:2 * n]
        send_sems, recv_sems, local_sems = refs[2 * n:]
        x, y, c, peers = _chip_peers()
        me = 2 * x + y
        own = []
        for i in range(n):
            cp = pltpu.make_async_copy(srcs[i] if gather else srcs[i].at[me], outs[i].at[me], local_sems.at[i])
            cp.start()
            own.append(cp)
        sends = []
        for k, (px, py) in enumerate(peers):
            for i in range(n):
                cp = pltpu.make_async_remote_copy(
                    src_ref=srcs[i] if gather else srcs[i].at[2 * px + py], dst_ref=outs[i].at[me],
                    send_sem=send_sems.at[3 * i + k], recv_sem=recv_sems.at[3 * i + k],
                    device_id=(px, py, c), device_id_type=MESH_IDS)
                cp.start()
                sends.append(cp)
        for k, (px, py) in enumerate(peers):
            for i in range(n):
                pltpu.make_async_remote_copy(
                    src_ref=srcs[i] if gather else srcs[i].at[me], dst_ref=outs[i].at[2 * px + py],
                    send_sem=send_sems.at[3 * i + k], recv_sem=recv_sems.at[3 * i + k],
                    device_id=(px, py, c), device_id_type=MESH_IDS).wait_recv()
        for cp in sends:
            cp.wait_send()
        for cp in own:
            cp.wait()

    out_shape = [jax.ShapeDtypeStruct(((4,) + a.shape) if gather else a.shape, a.dtype) for a in arrs]
    return pl.pallas_call(
        body, name=name, out_shape=out_shape,
        in_specs=[pl.BlockSpec(memory_space=pl.ANY)] * n,
        out_specs=[pl.BlockSpec(memory_space=pl.ANY)] * n,
        scratch_shapes=[pltpu.SemaphoreType.DMA((3 * n,)), pltpu.SemaphoreType.DMA((3 * n,)),
                        pltpu.SemaphoreType.DMA((n,))],
    )(*arrs)


def _allgather_chips(arrs):
    n = len(arrs)
    halves = [a.shape[0] // 2 for a in arrs]

    def body(*refs):
        srcs, outs = refs[:n], refs[n:2 * n]
        ici_send, ici_recv, d2d_send, d2d_recv, local_sems = refs[2 * n:]
        x, y, c, peers = _chip_peers()
        me = 2 * x + y
        rows = lambda i, who: pl.ds(who * halves[i], halves[i])
        own = []
        for i in range(n):
            cp = pltpu.make_async_copy(srcs[i], outs[i].at[me], local_sems.at[i])
            cp.start()
            own.append(cp)

        def over_ici(k, i, slot):
            px, py = peers[k]
            return pltpu.make_async_remote_copy(
                src_ref=srcs[i].at[rows(i, c)], dst_ref=outs[i].at[slot, rows(i, c)],
                send_sem=ici_send.at[3 * i + k], recv_sem=ici_recv.at[3 * i + k],
                device_id=(px, py, c), device_id_type=MESH_IDS)

        def over_d2d(k, i, half):
            px, py = peers[k]
            where = outs[i].at[2 * px + py, rows(i, half)]
            return pltpu.make_async_remote_copy(
                src_ref=where, dst_ref=where, send_sem=d2d_send.at[3 * i + k], recv_sem=d2d_recv.at[3 * i + k],
                device_id=(x, y, 1 - c), device_id_type=MESH_IDS)

        sends = [over_ici(k, i, me) for k in range(3) for i in range(n)]
        for cp in sends:
            cp.start()
        passed = []
        for k, (px, py) in enumerate(peers):
            for i in range(n):
                over_ici(k, i, 2 * px + py).wait_recv()
                cp = over_d2d(k, i, c)
                cp.start()
                passed.append(cp)
        for k in range(3):
            for i in range(n):
                over_d2d(k, i, 1 - c).wait_recv()
        for cp in sends + passed:
            cp.wait_send()
        for cp in own:
            cp.wait()

    return pl.pallas_call(
        body, name="allgather_weights",
        out_shape=[jax.ShapeDtypeStruct((4,) + a.shape, a.dtype) for a in arrs],
        in_specs=[pl.BlockSpec(memory_space=pl.ANY)] * n,
        out_specs=[pl.BlockSpec(memory_space=pl.ANY)] * n,
        scratch_shapes=[pltpu.SemaphoreType.DMA((3 * n,))] * 4 + [pltpu.SemaphoreType.DMA((n,))],
    )(*arrs)


def _swap_with_sibling(arrs):
    n = len(arrs)

    def body(*refs):
        srcs, outs = refs[:n], refs[n:2 * n]
        send_sems, recv_sems = refs[2 * n:]
        x, y, c = lax.axis_index("x"), lax.axis_index("y"), lax.axis_index("c")
        cps = [pltpu.make_async_remote_copy(src_ref=srcs[i], dst_ref=outs[i], send_sem=send_sems.at[i],
                                            recv_sem=recv_sems.at[i], device_id=(x, y, 1 - c),
                                            device_id_type=MESH_IDS) for i in range(n)]
        for cp in cps:
            cp.start()
        for cp in cps:
            cp.wait()

    return pl.pallas_call(
        body, name="swap_sibling",
        out_shape=[jax.ShapeDtypeStruct(a.shape, a.dtype) for a in arrs],
        in_specs=[pl.BlockSpec(memory_space=pl.ANY)] * n,
        out_specs=[pl.BlockSpec(memory_space=pl.ANY)] * n,
        scratch_shapes=[pltpu.SemaphoreType.DMA((n,)), pltpu.SemaphoreType.DMA((n,))],
    )(*arrs)


def _row_tile(rows, cols):
    cap = max(8, (3 << 19) // (4 * (-(-cols // 128) * 128)))
    best = None
    for t in range(8, min(rows, cap) + 1, 8):
        if rows % t == 0:
            best = t
    return best or rows


def _sum_sources(r4, name):
    _, A, Bc = r4.shape
    ta = _row_tile(A, Bc)

    def body(r_ref, o_ref):
        f = lambda s: r_ref[s].astype(F32)
        o_ref[...] = ((f(0) + f(1)) + f(2)) + f(3)

    return pl.pallas_call(
        body, name=name, grid=(A // ta,),
        out_shape=jax.ShapeDtypeStruct((A, Bc), F32),
        in_specs=[pl.BlockSpec((4, ta, Bc), lambda i: (0, i, 0))],
        out_specs=pl.BlockSpec((ta, Bc), lambda i: (i, 0)),
        compiler_params=_cparams(),
    )(r4)


def _adamw(own, other, w, m, v, name):
    R, C = own.shape
    tr = _row_tile(R, C)

    def body(a_ref, b_ref, w_ref, m_ref, v_ref, g_out, d_out, m_out, v_out):
        g = a_ref[...] + b_ref[...]
        m_new = ADAM_B1 * m_ref[...] + (1.0 - ADAM_B1) * g
        v_new = ADAM_B2 * v_ref[...] + (1.0 - ADAM_B2) * (g * g)
        m_hat = m_new / (1.0 - ADAM_B1 ** ADAM_STEP)
        v_hat = v_new / (1.0 - ADAM_B2 ** ADAM_STEP)
        g_out[...] = g
        d_out[...] = -ADAM_LR * (m_hat / (jnp.sqrt(v_hat) + ADAM_EPS) + ADAM_WD * w_ref[...])
        m_out[...] = m_new
        v_out[...] = v_new

    spec = pl.BlockSpec((tr, C), lambda i: (i, 0))
    return pl.pallas_call(
        body, name=name, grid=(R // tr,),
        out_shape=[jax.ShapeDtypeStruct((R, C), F32)] * 4,
        in_specs=[spec] * 5, out_specs=[spec] * 4,
        compiler_params=_cparams(),
    )(own, other, w, m, v)


def _pick(n, options):
    for o in options:
        if n % o == 0:
            return o
    return n


def _div128(n, cap):
    best = None
    for t in range(128, min(n, cap) + 1, 128):
        if n % t == 0:
            best = t
    return best or n


MATMUL_VMEM = 40 * 1024 * 1024


def _matmul(a, b, name, out_dtype=F32, residual=None):
    M, K = a.shape
    _, N = b.shape
    tm, tn = _pick(M, (1024, 512)), _div128(N, 1408)
    while tm > 256 and 2 * (2 * tm * K + 2 * K * tn + (8 if residual is not None else 4) * tm * tn) > MATMUL_VMEM:
        tm //= 2

    def body(*refs):
        a_ref, b_ref = refs[0], refs[1]
        o_ref = refs[-1]
        acc = jnp.dot(a_ref[...], b_ref[...], preferred_element_type=F32)
        if residual is not None:
            acc = acc + refs[2][...]
        o_ref[...] = acc.astype(out_dtype)

    in_specs = [pl.BlockSpec((tm, K), lambda j, i: (i, 0)), pl.BlockSpec((K, tn), lambda j, i: (0, j))]
    args = [a, b]
    if residual is not None:
        in_specs.append(pl.BlockSpec((tm, tn), lambda j, i: (i, j)))
        args.append(residual)
    return pl.pallas_call(
        body, name=name, grid=(N // tn, M // tm),
        out_shape=jax.ShapeDtypeStruct((M, N), out_dtype),
        in_specs=in_specs, out_specs=pl.BlockSpec((tm, tn), lambda j, i: (i, j)),
        compiler_params=_cparams(),
    )(*args)


def _matmul_tn(a, b, name):
    R, M = a.shape
    _, N = b.shape
    tr, tm, tn = _pick(R, (2048, 1024, 512)), _div128(M, 1408), _div128(N, 1408)
    while tr > 512 and 2 * (2 * tr * tm + 2 * tr * tn + 4 * tm * tn) > MATMUL_VMEM:
        tr //= 2

    def body(a_ref, b_ref, o_ref):
        @pl.when(pl.program_id(2) == 0)
        def _():
            o_ref[...] = jnp.zeros_like(o_ref)

        o_ref[...] += lax.dot_general(a_ref[...], b_ref[...], (((0,), (0,)), ((), ())),
                                      preferred_element_type=F32)

    return pl.pallas_call(
        body, name=name, grid=(M // tm, N // tn, R // tr),
        out_shape=jax.ShapeDtypeStruct((M, N), F32),
        in_specs=[pl.BlockSpec((tr, tm), lambda i, j, r: (r, i)), pl.BlockSpec((tr, tn), lambda i, j, r: (r, j))],
        out_specs=pl.BlockSpec((tm, tn), lambda i, j, r: (i, j)),
        compiler_params=_cparams(),
    )(a, b)


def _row_spec(tm, width, col, lead=None):
    if lead is None:
        return pl.BlockSpec((tm, width), lambda i: (i, col))
    return pl.BlockSpec((None, tm, width), lambda i: (lead, i, col))


def _whole_spec(arr):
    nd = arr.ndim
    return pl.BlockSpec(arr.shape, lambda i: (0,) * nd)


def _rowwise(name, fn, rows, params, consts, outs, n_rows, tm):
    nr, npar, nc = len(rows), len(params), len(consts)

    def body(*refs):
        vals = [r[...].astype(F32) for r in refs[:nr]] + [r[...] for r in refs[nr:nr + npar + nc]]
        res = fn(*vals)
        for o_ref, r in zip(refs[nr + npar + nc:], res):
            o_ref[...] = r.astype(o_ref.dtype)

    return pl.pallas_call(
        body, name=name, grid=(n_rows // tm,),
        out_shape=[jax.ShapeDtypeStruct((n_rows, w), dt) for w, dt in outs],
        in_specs=[_row_spec(tm, *r[1:]) for r in rows] + [_whole_spec(p) for p in params + consts],
        out_specs=[_row_spec(tm, w, 0) for w, _ in outs],
        compiler_params=_cparams(),
    )(*[r[0] for r in rows], *params, *consts)


def _rowwise_bwd(name, fn, rows, params, consts, cts, row_grads, n_rows, tm):
    nr, npar, nc = len(rows), len(params), len(consts)
    ct_flat = [p for pieces in cts for p in pieces]
    res_flat = [rg[1] for rg in row_grads if rg is not None and rg[1] is not None]
    n_ct, n_res = len(ct_flat), len(res_flat)
    n_in = nr + npar + nc + n_ct + n_res
    wanted = [k for k, rg in enumerate(row_grads) if rg is not None]

    def body(*refs):
        row_vals = [r[...].astype(F32) for r in refs[:nr]]
        par_vals = [r[...] for r in refs[nr:nr + npar]]
        const_vals = [r[...] for r in refs[nr + npar:nr + npar + nc]]
        ct_refs = refs[nr + npar + nc:nr + npar + nc + n_ct]
        res_refs = refs[nr + npar + nc + n_ct:n_in]
        out_refs = refs[n_in:]
        ct_vals, pos = [], 0
        for pieces in cts:
            acc = ct_refs[pos][...].astype(F32)
            for q in range(1, len(pieces)):
                acc = acc + ct_refs[pos + q][...].astype(F32)
            pos += len(pieces)
            ct_vals.append(acc)
        _, vjp = jax.vjp(lambda *a: tuple(fn(*a, *const_vals)), *row_vals, *par_vals)
        grads = vjp(tuple(ct_vals))
        ri = 0
        for slot, k in enumerate(wanted):
            g = grads[k]
            if row_grads[k][1] is not None:
                g = g + res_refs[ri][...].astype(F32)
                ri += 1
            out_refs[slot][...] = g.astype(out_refs[slot].dtype)

        @pl.when(pl.program_id(0) == 0)
        def _():
            for q in range(npar):
                out_refs[len(wanted) + q][...] = jnp.zeros_like(out_refs[len(wanted) + q])

        for q in range(npar):
            out_refs[len(wanted) + q][...] += grads[nr + q]

    out_shape = [jax.ShapeDtypeStruct((n_rows, rows[k][1]), row_grads[k][0]) for k in wanted]
    out_shape += [jax.ShapeDtypeStruct(p.shape, F32) for p in params]
    out_specs = [_row_spec(tm, rows[k][1], 0) for k in wanted] + [_whole_spec(p) for p in params]
    in_specs = [_row_spec(tm, *r[1:]) for r in rows] + [_whole_spec(p) for p in params + consts]
    in_specs += [_row_spec(tm, *r[1:]) for r in ct_flat + res_flat]
    res = pl.pallas_call(
        body, name=name, grid=(n_rows // tm,),
        out_shape=out_shape, in_specs=in_specs, out_specs=out_specs,
        compiler_params=_cparams(),
    )(*[r[0] for r in rows], *params, *consts, *[r[0] for r in ct_flat + res_flat])
    return res[:len(wanted)], res[len(wanted):]


def _sigmoid(x):
    return 0.5 * jnp.tanh(0.5 * x) + 0.5


def _softplus(x):
    return jnp.maximum(x, 0.0) + jnp.log(1.0 + jnp.exp(-jnp.abs(x)))


def _seg_dot_impl(x, seg2):
    hi = x.astype(BF16)
    lo = (x - hi.astype(F32)).astype(BF16)
    return jnp.dot(jnp.concatenate([hi, lo], axis=1), seg2, preferred_element_type=F32)


@jax.custom_vjp
def _seg_dot(x, seg2):
    return _seg_dot_impl(x, seg2)


_seg_dot.defvjp(lambda x, seg2: (_seg_dot_impl(x, seg2), seg2),
                lambda seg2, ct: (_seg_dot_impl(ct, seg2), jnp.zeros_like(seg2)))


def _fn_norm(x, g):
    r = lax.rsqrt(jnp.mean(x * x, axis=-1, keepdims=True) + NORM_EPS)
    return ((x * r) * g,)


def _fn_gla_post(o_f, o_b, og, norm_g):
    o = o_f + o_b
    heads = []
    for h in range(4):
        oh = o[:, h * 128:(h + 1) * 128]
        heads.append(oh * lax.rsqrt(jnp.mean(oh * oh, axis=-1, keepdims=True) + HEAD_NORM_EPS))
    on = jnp.concatenate(heads, axis=1) * norm_g
    return (on * (og * _sigmoid(og)),)


def _fn_rwkv_pre(s_k, s_wag, w0_f, w0_b, a0, k_k, k_a, w2_f, w2_b, a2, g2, seg64):
    wa = s_wag[:, 0:128]
    gl = s_wag[:, 128:256]
    tw = jnp.tanh(wa)
    z_f = w0_f + jnp.dot(tw, w2_f, preferred_element_type=F32)
    z_b = w0_b + jnp.dot(tw, w2_b, preferred_element_type=F32)
    w_f = jnp.exp(-jnp.exp(-_softplus(-z_f) - 0.5))
    w_b = jnp.exp(-jnp.exp(-_softplus(-z_b) - 0.5))
    a = _sigmoid(a0 + jnp.dot(wa, a2, preferred_element_type=F32))
    g = jnp.dot(_sigmoid(gl), g2, preferred_element_type=F32)
    kk = s_k * k_k
    kkn = kk / jnp.maximum(jnp.sqrt(_seg_dot(kk * kk, seg64)), 1e-12)
    k2 = s_k * (1.0 + (a - 1.0) * k_a)
    return w_f, w_b, k2, -kkn, kkn * a, g


def _fn_rwkv_post(y_f, y_b, s_r, k2, s_v, g, ln_w, ln_b, r_k, seg64):
    y = y_f + y_b
    mu = _seg_dot(y, seg64) * (1.0 / 64.0)
    yc = y - mu
    var = _seg_dot(yc * yc, seg64) * (1.0 / 64.0)
    yn = yc * lax.rsqrt(var + RWKV_GN_EPS) * ln_w + ln_b
    bonus = _seg_dot(s_r * k2 * r_k, seg64) * s_v
    return ((yn + bonus) * g,)


def _fn_merge(ga, gb, y_a, y_b):
    return (_sigmoid(ga) * y_a + _sigmoid(gb) * y_b,)


def _loss_head(x2, target, gf):
    N, Dm = x2.shape
    tm = _pick(N, (512,))

    def fn(x, g, t):
        r = lax.rsqrt(jnp.mean(x * x, axis=-1, keepdims=True) + NORM_EPS)
        err = (x * r) * g - t
        return 0.5 * jnp.sum(jnp.mean(err * err, axis=-1, keepdims=True), axis=0, keepdims=True)

    def body(x_ref, t_ref, g_ref, loss_ref, dx_ref, dxb_ref, dg_ref):
        t = t_ref[...]
        loss, vjp = jax.vjp(lambda x, g: fn(x, g, t), x_ref[...], g_ref[...])
        dx, dg = vjp(jnp.ones((1, 1), F32))

        @pl.when(pl.program_id(0) == 0)
        def _():
            loss_ref[...] = jnp.zeros_like(loss_ref)
            dg_ref[...] = jnp.zeros_like(dg_ref)

        loss_ref[...] += jnp.broadcast_to(loss, loss_ref.shape)
        dg_ref[...] += dg
        dx_ref[...] = dx
        dxb_ref[...] = dx.astype(BF16)

    return pl.pallas_call(
        body, name="loss_head", grid=(N // tm,),
        out_shape=[jax.ShapeDtypeStruct((8, 128), F32), jax.ShapeDtypeStruct((N, Dm), F32),
                   jax.ShapeDtypeStruct((N, Dm), BF16), jax.ShapeDtypeStruct((1, Dm), F32)],
        in_specs=[_row_spec(tm, Dm, 0), _row_spec(tm, Dm, 0), _whole_spec(gf)],
        out_specs=[pl.BlockSpec((8, 128), lambda i: (0, 0)), _row_spec(tm, Dm, 0), _row_spec(tm, Dm, 0),
                   pl.BlockSpec((1, Dm), lambda i: (0, 0))],
        compiler_params=_cparams(),
    )(x2, target, gf)


def _shift_prev(u):
    rolled = pltpu.roll(u, 1, axis=0)
    row = lax.broadcasted_iota(jnp.int32, u.shape, 0)
    return jnp.where(row == 0, 0.0, rolled)


def _shift_next(u):
    T = u.shape[0]
    rolled = pltpu.roll(u, T - 1, axis=0)
    row = lax.broadcasted_iota(jnp.int32, u.shape, 0)
    return jnp.where(row == T - 1, 0.0, rolled)


_SHIFT_BLOCKS = 7


def _shift_src_col(j):
    return jnp.where(j < 6, 6 + j, 20)


def _token_shift(p, mu_prev, mu_next, B, T):
    def body(p_ref, mp_ref, mn_ref, s_ref):
        u = p_ref[...]
        s_ref[...] = u + mp_ref[...] * (_shift_prev(u) - u) + mn_ref[...] * (_shift_next(u) - u)

    return pl.pallas_call(
        body, name="token_shift", grid=(B, _SHIFT_BLOCKS),
        out_shape=jax.ShapeDtypeStruct((B * T, 1792), F32),
        in_specs=[pl.BlockSpec((T, 256), lambda b, j: (b, _shift_src_col(j))),
                  pl.BlockSpec((1, 256), lambda b, j: (0, j)), pl.BlockSpec((1, 256), lambda b, j: (0, j))],
        out_specs=pl.BlockSpec((T, 256), lambda b, j: (b, j)),
        compiler_params=_cparams(),
    )(p, mu_prev, mu_next)


def _token_shift_bwd(p, ds, mu_prev, mu_next, B, T):
    def body(p_ref, ds_ref, mp_ref, mn_ref, dp_ref, dmp_ref, dmn_ref):
        u, d = p_ref[...], ds_ref[...]
        mp, mn = mp_ref[...], mn_ref[...]
        dp = d * (1.0 - mp - mn) + _shift_next(d * mp) + _shift_prev(d * mn)
        dp_ref[...] = dp.astype(dp_ref.dtype)

        @pl.when(pl.program_id(1) == 0)
        def _():
            dmp_ref[...] = jnp.zeros_like(dmp_ref)
            dmn_ref[...] = jnp.zeros_like(dmn_ref)

        dmp_ref[...] += jnp.sum(d * (_shift_prev(u) - u), axis=0, keepdims=True)
        dmn_ref[...] += jnp.sum(d * (_shift_next(u) - u), axis=0, keepdims=True)

    return pl.pallas_call(
        body, name="token_shift_bwd", grid=(_SHIFT_BLOCKS, B),
        out_shape=[jax.ShapeDtypeStruct((B * T, 1792), BF16), jax.ShapeDtypeStruct((1, 1792), F32),
                   jax.ShapeDtypeStruct((1, 1792), F32)],
        in_specs=[pl.BlockSpec((T, 256), lambda j, b: (b, _shift_src_col(j))),
                  pl.BlockSpec((T, 256), lambda j, b: (b, j)),
                  pl.BlockSpec((1, 256), lambda j, b: (0, j)), pl.BlockSpec((1, 256), lambda j, b: (0, j))],
        out_specs=[pl.BlockSpec((T, 256), lambda j, b: (b, j)), pl.BlockSpec((1, 256), lambda j, b: (0, j)),
                   pl.BlockSpec((1, 256), lambda j, b: (0, j))],
        compiler_params=_cparams(),
    )(p, ds, mu_prev, mu_next)


_FF_BLOCKS = FF_PAD // 256


def _conv3(u, cw, cb):
    return cw[0:1] * _shift_prev(u) + cw[1:2] * u + cw[2:3] * _shift_next(u) + cb


def _ffn_conv(u, cw, cb, B, T):
    def body(ug_ref, uv_ref, cwg_ref, cwv_ref, cbg_ref, cbv_ref, o_ref):
        cg = _conv3(ug_ref[...], cwg_ref[...], cbg_ref[...])
        cv = _conv3(uv_ref[...], cwv_ref[...], cbv_ref[...])
        o_ref[...] = (cg * _sigmoid(cg) * cv).astype(o_ref.dtype)

    nb = _FF_BLOCKS
    return pl.pallas_call(
        body, name="ffn_conv", grid=(B, nb),
        out_shape=jax.ShapeDtypeStruct((B * T, FF_PAD), BF16),
        in_specs=[pl.BlockSpec((T, 256), lambda b, j: (b, j)), pl.BlockSpec((T, 256), lambda b, j: (b, j + nb)),
                  pl.BlockSpec((3, 256), lambda b, j: (0, j)), pl.BlockSpec((3, 256), lambda b, j: (0, j + nb)),
                  pl.BlockSpec((1, 256), lambda b, j: (0, j)), pl.BlockSpec((1, 256), lambda b, j: (0, j + nb))],
        out_specs=pl.BlockSpec((T, 256), lambda b, j: (b, j)),
        compiler_params=_cparams(),
    )(u, u, cw, cw, cb, cb)


def _ffn_conv_bwd(u, dact, cw, cb, B, T):
    def half(u_, dc, cw_):
        du = _shift_next(cw_[0:1] * dc) + cw_[1:2] * dc + _shift_prev(cw_[2:3] * dc)
        dcw = jnp.concatenate([jnp.sum(dc * _shift_prev(u_), axis=0, keepdims=True),
                               jnp.sum(dc * u_, axis=0, keepdims=True),
                               jnp.sum(dc * _shift_next(u_), axis=0, keepdims=True)], axis=0)
        return du, dcw, jnp.sum(dc, axis=0, keepdims=True)

    def body(ug_ref, uv_ref, da_ref, cwg_ref, cwv_ref, cbg_ref, cbv_ref,
             dug_ref, duv_ref, dcwg_ref, dcwv_ref, dcbg_ref, dcbv_ref):
        ug, uv, da = ug_ref[...], uv_ref[...], da_ref[...]
        cwg, cwv = cwg_ref[...], cwv_ref[...]
        cg = _conv3(ug, cwg, cbg_ref[...])
        cv = _conv3(uv, cwv, cbv_ref[...])
        sg = _sigmoid(cg)
        dcv = da * (cg * sg)
        dcg = da * cv * (sg * (1.0 + cg * (1.0 - sg)))
        dug, dcwg, dcbg = half(ug, dcg, cwg)
        duv, dcwv, dcbv = half(uv, dcv, cwv)
        dug_ref[...] = dug.astype(dug_ref.dtype)
        duv_ref[...] = duv.astype(duv_ref.dtype)

        @pl.when(pl.program_id(1) == 0)
        def _():
            for r in (dcwg_ref, dcwv_ref, dcbg_ref, dcbv_ref):
                r[...] = jnp.zeros_like(r)

        dcwg_ref[...] += dcwg
        dcwv_ref[...] += dcwv
        dcbg_ref[...] += dcbg
        dcbv_ref[...] += dcbv

    nb = _FF_BLOCKS
    N = B * T
    res = pl.pallas_call(
        body, name="ffn_conv_bwd", grid=(nb, B),
        out_shape=[jax.ShapeDtypeStruct((N, FF_PAD), BF16), jax.ShapeDtypeStruct((N, FF_PAD), BF16),
                   jax.ShapeDtypeStruct((3, FF_PAD), F32), jax.ShapeDtypeStruct((3, FF_PAD), F32),
                   jax.ShapeDtypeStruct((1, FF_PAD), F32), jax.ShapeDtypeStruct((1, FF_PAD), F32)],
        in_specs=[pl.BlockSpec((T, 256), lambda j, b: (b, j)), pl.BlockSpec((T, 256), lambda j, b: (b, j + nb)),
                  pl.BlockSpec((T, 256), lambda j, b: (b, j)),
                  pl.BlockSpec((3, 256), lambda j, b: (0, j)), pl.BlockSpec((3, 256), lambda j, b: (0, j + nb)),
                  pl.BlockSpec((1, 256), lambda j, b: (0, j)), pl.BlockSpec((1, 256), lambda j, b: (0, j + nb))],
        out_specs=[pl.BlockSpec((T, 256), lambda j, b: (b, j)), pl.BlockSpec((T, 256), lambda j, b: (b, j)),
                   pl.BlockSpec((3, 256), lambda j, b: (0, j)), pl.BlockSpec((3, 256), lambda j, b: (0, j)),
                   pl.BlockSpec((1, 256), lambda j, b: (0, j)), pl.BlockSpec((1, 256), lambda j, b: (0, j))],
        compiler_params=_cparams(),
    )(u, u, dact, cw, cw, cb, cb)
    dug, duv, dcwg, dcwv, dcbg, dcbv = res
    return dug, duv, jnp.concatenate([dcwg, dcwv], axis=1), jnp.concatenate([dcbg, dcbv], axis=1)


def _gla_chunk(q, k, v, afab, wa2, ba, state, rev, narrow=False):
    C = GLA_CHUNK
    n = q.shape[0]
    z = jnp.dot(afab.reshape(n * C, 128), wa2, preferred_element_type=F32).reshape(n, C, 256) + ba
    la = (jnp.minimum(z, 0.0) - jnp.log(1.0 + jnp.exp(-jnp.abs(z)))) * (1.0 / 16.0)
    row = lax.broadcasted_iota(jnp.int32, (n * C, n * C), 0)
    col = lax.broadcasted_iota(jnp.int32, (n * C, n * C), 1)
    ordered = ((col & (C - 1)) - (row & (C - 1))) * (1 - 2 * rev) <= 0
    tri_all = ordered & ((row >> 6) == (col >> 6))
    b = jnp.dot(tri_all.astype(F32), la.reshape(n * C, 256), precision=HIGHEST,
                preferred_element_type=F32).reshape(n, C, 256)
    tri = (lax.broadcasted_iota(jnp.int32, (C, C), 1) - lax.broadcasted_iota(jnp.int32, (C, C), 0)) * (1 - 2 * rev) <= 0
    rows = lax.broadcasted_iota(jnp.int32, (n, C, 256), 1)
    ref_row = jnp.where(rev == 0, C // 2, C - 1 - C // 2)
    last_row = jnp.where(rev == 0, C - 1, 0)
    b_ref = jnp.sum(jnp.where(rows == ref_row, b, 0.0), axis=1, keepdims=True)
    b_last = jnp.sum(jnp.where(rows == last_row, b, 0.0), axis=1, keepdims=True)
    qs = q * 0.125
    qi = qs * jnp.exp(b - b_ref)
    ki = k * jnp.exp(b_ref - b)
    kd = k * jnp.exp(b_last - b)
    qe = qs * jnp.exp(b)
    lane = lax.broadcasted_iota(jnp.int32, (1, 1, 256), 2)
    op = (lambda t: t.astype(BF16)) if narrow else (lambda t: t)
    bdot = lambda x, y, cx, cy: lax.dot_general(op(x), op(y), (((cx,), (cy,)), ((0,), (0,))),
                                                preferred_element_type=F32)
    outs = []
    upd = jnp.zeros_like(state)
    for h in range(4):
        mh = ((lane >= 64 * h) & (lane < 64 * (h + 1))).astype(F32)
        vh = v[:, :, 128 * h:128 * (h + 1)]
        a = jnp.where(tri, bdot(qi * mh, ki, 2, 2), 0.0)
        outs.append(bdot(a, vh, 2, 1) + bdot(qe, state * mh, 2, 2))
        upd = upd + bdot(vh, kd * mh, 1, 1)
    return jnp.concatenate(outs, axis=2), state * jnp.exp(b_last) + upd


def _gla_chunk_at(nC):
    return lambda d, j: j + d * (nC - 1 - 2 * j)


def _gla_fwd(p, wa2, ba, B, T):
    nC = T // GLA_CHUNK
    N = B * T
    at = _gla_chunk_at(nC)
    p3 = p.reshape(B, T, p.shape[-1])

    def body(q_ref, k_ref, v_ref, af_ref, wa_ref, ba_ref, o_ref, st_ref, state):
        @pl.when(pl.program_id(1) == 0)
        def _():
            state[...] = jnp.zeros_like(state)

        st_ref[0, :, 0] = state[...]
        o, new = _gla_chunk(q_ref[...], k_ref[...], v_ref[...], af_ref[...], wa_ref[0], ba_ref[0], state[...],
                            pl.program_id(0), narrow=True)
        o_ref[0] = o
        state[...] = new

    blk = lambda w, col: pl.BlockSpec((B, 64, w), lambda d, j: (0, at(d, j), col))
    o, st = pl.pallas_call(
        body, name="gla_fwd", grid=(2, nC),
        out_shape=[jax.ShapeDtypeStruct((2, B, T, 512), F32), jax.ShapeDtypeStruct((2, B, nC, 128, 256), F32)],
        in_specs=[blk(256, 0), blk(256, 1), blk(512, 1), blk(128, 42),
                  pl.BlockSpec((1, 128, 256), lambda d, j: (d, 0, 0)),
                  pl.BlockSpec((1, 1, 256), lambda d, j: (d, 0, 0))],
        out_specs=[pl.BlockSpec((1, B, 64, 512), lambda d, j: (d, 0, at(d, j), 0)),
                   pl.BlockSpec((1, B, 1, 128, 256), lambda d, j: (d, 0, at(d, j), 0, 0))],
        scratch_shapes=[pltpu.VMEM((B, 128, 256), F32)],
        compiler_params=_cparams(),
    )(p3, p3, p3, p3, wa2, ba)
    return o.reshape(2, N, 512), st


def _gla_bwd(p, do, states, wa2, ba, B, T):
    nC = T // GLA_CHUNK
    N = B * T
    at_f = _gla_chunk_at(nC)
    at = lambda d, j: at_f(d, nC - 1 - j)
    p3 = p.reshape(B, T, p.shape[-1])

    def body(q_ref, k_ref, v_ref, af_ref, wa_ref, ba_ref, do_ref, st_ref,
             dqkv_ref, daf_ref, dwa_ref, dba_ref, dstate):
        rev = pl.program_id(0)

        @pl.when(pl.program_id(1) == 0)
        def _():
            dstate[...] = jnp.zeros_like(dstate)
            dwa_ref[...] = jnp.zeros_like(dwa_ref)
            dba_ref[...] = jnp.zeros_like(dba_ref)

        f = lambda q, k, v, af, wa, bb, st: _gla_chunk(q, k, v, af, wa, bb, st, rev)
        _, vjp = jax.vjp(f, q_ref[...], k_ref[...], v_ref[...], af_ref[...], wa_ref[0], ba_ref[0], st_ref[0, :, 0])
        dq, dk, dv, daf, dwa, dba, dst = vjp((do_ref[...], dstate[...]))
        dqkv_ref[0] = jnp.concatenate([dq, dk, dv], axis=2)
        daf_ref[0] = daf
        dwa_ref[0] += dwa
        dba_ref[0] += dba
        dstate[...] = dst

    blk = lambda w, col: pl.BlockSpec((B, 64, w), lambda d, j: (0, at(d, j), col))
    out4 = lambda w: pl.BlockSpec((1, B, 64, w), lambda d, j: (d, 0, at(d, j), 0))
    dqkv, daf, dwa, dba = pl.pallas_call(
        body, name="gla_bwd", grid=(2, nC),
        out_shape=[jax.ShapeDtypeStruct((2, B, T, 1024), F32), jax.ShapeDtypeStruct((2, B, T, 128), F32),
                   jax.ShapeDtypeStruct((2, 128, 256), F32), jax.ShapeDtypeStruct((2, 1, 256), F32)],
        in_specs=[blk(256, 0), blk(256, 1), blk(512, 1), blk(128, 42),
                  pl.BlockSpec((1, 128, 256), lambda d, j: (d, 0, 0)),
                  pl.BlockSpec((1, 1, 256), lambda d, j: (d, 0, 0)),
                  blk(512, 0),
                  pl.BlockSpec((1, B, 1, 128, 256), lambda d, j: (d, 0, at(d, j), 0, 0))],
        out_specs=[out4(1024), out4(128),
                   pl.BlockSpec((1, 128, 256), lambda d, j: (d, 0, 0)),
                   pl.BlockSpec((1, 1, 256), lambda d, j: (d, 0, 0))],
        scratch_shapes=[pltpu.VMEM((B, 128, 256), F32)],
        compiler_params=_cparams(),
    )(p3, p3, p3, p3, wa2, ba, do.reshape(B, T, 512), states)
    return dqkv.reshape(2, N, 1024), daf.reshape(2, N, 128), dwa, dba


def _seg_ones():
    m = lax.broadcasted_iota(jnp.int32, (256, 128), 0)
    n = lax.broadcasted_iota(jnp.int32, (256, 128), 1)
    return (((m >> 6) & 1) == (n >> 6)).astype(BF16)


def _seg_mm(x, ones2):
    hi = x.astype(BF16)
    lo = (x - hi.astype(F32)).astype(BF16)
    return jnp.dot(jnp.concatenate([hi, lo], axis=1), ones2, preferred_element_type=F32)


def _diag_matrix():
    r = np.arange(2048)[:, None] % 64
    c = np.arange(128)[None, :] % 64
    return jnp.asarray((r == c).astype(np.float32))


def _cols8(tile, dg, ones2):
    return _seg_mm(jnp.concatenate([_rows4(tile, q) for q in range(8)], axis=0) * dg[...], ones2)


def _rows4(tile, q):
    return jnp.concatenate([jnp.broadcast_to(tile[q:q + 1, 128 * p:128 * (p + 1)], (64, 128)) for p in range(4)],
                           axis=0)


def _head_rows():
    r = lax.broadcasted_iota(jnp.int32, (16, 256), 0)
    n = lax.broadcasted_iota(jnp.int32, (16, 256), 1)
    return (r == ((n >> 6) & 1)).astype(BF16)


def _head_sums_row(x, heads2):
    hi = x.astype(BF16)
    lo = (x - hi.astype(F32)).astype(BF16)
    out = lax.dot_general(heads2, jnp.concatenate([hi, lo], axis=1), (((1,), (1,)), ((), ())),
                          preferred_element_type=F32)
    return jnp.concatenate([out[h:h + 1, 64 * p:64 * (p + 1)] for p in range(4) for h in range(2)], axis=1)


def _colsum4(m):
    return jnp.concatenate([jnp.sum(m[64 * p:64 * (p + 1)], axis=0, keepdims=True) for p in range(4)], axis=1)


def _time_base(gi, n_groups, rev):
    return pl.multiple_of(((n_groups - 1 - gi) if rev else gi) * 8, 8)


def _scan_fwd_mxu(s, wf, wb, k2, na, bb, B, T):
    Tc = SCAN_CHUNK
    nT = T // Tc
    nG = Tc // 8
    N = B * T
    nb = _pick(B, (4, 2))
    fwd_j = lambda j: j
    bwd_j = lambda j: nT - 1 - j

    def body(rF, vF, kF, aF, bF, wF, rB, vB, kB, aB, bB, wB, dg, yF, yB, ckF, ckB,
             SF, SB, vcF, vcB, ypF, ypB, ytF, ytB):
        @pl.when(pl.program_id(1) == 0)
        def _():
            SF[...] = jnp.zeros_like(SF)
            SB[...] = jnp.zeros_like(SB)

        ckF[...] = SF[...]
        ckB[...] = SB[...]
        ones2, heads2 = _seg_ones(), _head_rows()
        chains = []
        for n in range(nb):
            chains.append((n, SF, (rF, vF, kF, aF, bF, wF), yF, vcF, ypF, ytF, False))
            chains.append((n, SB, (rB, vB, kB, aB, bB, wB), yB, vcB, ypB, ytB, True))

        def group(gi, carry):
            tiles, states = [], []
            for n, S_ref, refs, _, vc, _, _, rev in chains:
                base = _time_base(gi, nG, rev)
                t = [ref[n, pl.ds(base, 8), :] for ref in refs]
                tiles.append(t)
                states.append(S_ref[n])
                vc[n] = _cols8(t[1], dg, ones2)
            for i8 in range(8):
                for c, (n, _, _, _, vc, yp, _, rev) in enumerate(chains):
                    q = 7 - i8 if rev else i8
                    r, v, k, a, b, w = tiles[c]
                    S = states[c]
                    sa = _seg_mm(S * _rows4(a, q), ones2)
                    S = S * _rows4(w, q) + sa * _rows4(b, q) + vc[n, 256 * q:256 * (q + 1), :] * _rows4(k, q)
                    yp[n, 256 * q:256 * (q + 1), :] = S * _rows4(r, q)
                    states[c] = S
            for c, (n, S_ref, _, y_ref, _, yp, yt, rev) in enumerate(chains):
                S_ref[n] = states[c]
                for q in range(8):
                    yt[n, q:q + 1, :] = _head_sums_row(yp[n, 256 * q:256 * (q + 1), :], heads2)
                y_ref[n, pl.ds(_time_base(gi, nG, rev), 8), :] = yt[n]
            return carry

        lax.fori_loop(0, nG, group, 0)

    row_in = lambda at, col: pl.BlockSpec((nb, Tc, 512), lambda g, j: (g, at(j), col))
    state_io = lambda at: pl.BlockSpec((nb, 256, 128), lambda g, j: (g, at(j), 0))
    in_specs = []
    for at in (fwd_j, bwd_j):
        in_specs += [row_in(at, 0), row_in(at, 2)] + [row_in(at, 0)] * 4
    big = pltpu.VMEM((nb, 8 * 256, 128), F32)
    s3 = s.reshape(B, T, s.shape[-1])
    seq = lambda a: a.reshape(B, T, 512)
    y_f, y_b, ck_f, ck_b = pl.pallas_call(
        body, name="rwkv_scan", grid=(B // nb, nT),
        out_shape=[jax.ShapeDtypeStruct((B, T, 512), F32), jax.ShapeDtypeStruct((B, T, 512), F32),
                   jax.ShapeDtypeStruct((B, nT * 256, 128), F32), jax.ShapeDtypeStruct((B, nT * 256, 128), F32)],
        in_specs=in_specs + [pl.BlockSpec((2048, 128), lambda g, j: (0, 0))],
        out_specs=[row_in(fwd_j, 0), row_in(bwd_j, 0), state_io(fwd_j), state_io(bwd_j)],
        scratch_shapes=[pltpu.VMEM((nb, 256, 128), F32), pltpu.VMEM((nb, 256, 128), F32), big, big, big, big,
                        pltpu.VMEM((nb, 8, 512), F32), pltpu.VMEM((nb, 8, 512), F32)],
        compiler_params=_cparams(),
    )(s3, s3, seq(k2), seq(na), seq(bb), seq(wf), s3, s3, seq(k2), seq(na), seq(bb), seq(wb), _diag_matrix())
    ck_shape = (B * nT * 256, 128)
    return (y_f.reshape(N, 512), y_b.reshape(N, 512),
            ck_f.reshape(ck_shape), ck_b.reshape(ck_shape))


def _scan_bwd_mxu(s, wf, wb, k2, na, bb, dy, ckF, ckB, B, T):
    Tc = SCAN_CHUNK
    nT = T // Tc
    nG = Tc // 8
    N = B * T
    nb = _pick(B, (2,))
    f_at = lambda j: nT - 1 - j
    b_at = lambda j: j
    n_in, n_out, n_scr = 17, 12, 12

    def body(*refs):
        (rF, vF, kF, aF, bF, wF, dyF, ckF_ref, rB, vB, kB, aB, bB, wB, dyB, ckB_ref, dg) = refs[:n_in]
        outsF, outsB = refs[n_in:n_in + 6], refs[n_in + 6:n_in + n_out]
        chains = []
        for n in range(nb):
            stF, stB, saF, saB, vcF, vcB, dSF, dSB, bigF, bigB, tileF, tileB = \
                refs[n_in + n_out + n_scr * n:n_in + n_out + n_scr * (n + 1)]
            chains.append((n, stF, dSF, ckF_ref, (rF, vF, kF, aF, bF, wF, dyF), outsF, bigF, tileF, False, saF, vcF))
            chains.append((n, stB, dSB, ckB_ref, (rB, vB, kB, aB, bB, wB, dyB), outsB, bigB, tileB, True, saB, vcB))

        @pl.when(pl.program_id(1) == 0)
        def _():
            for chain in chains:
                chain[2][...] = jnp.zeros_like(chain[2])

        ones2, heads2 = _seg_ones(), _head_rows()
        for chain in chains:
            chain[1][0] = chain[3][chain[0]]

        cols8 = lambda tile: _cols8(tile, dg, ones2)

        def recompute(gi, carry):
            tiles, states = [], []
            for n, st, _, _, ins, _, big, _, rev, _, vc_keep in chains:
                base = _time_base(gi, nG, rev)
                t = [ref[n, pl.ds(base, 8), :] for ref in ins[1:6]]
                tiles.append(t)
                states.append(st[gi * 8])
                v_cols = cols8(t[0])
                for i8 in range(8):
                    q = 7 - i8 if rev else i8
                    vc_keep[gi * 8 + i8] = v_cols[256 * q:256 * (q + 1)]
            for i8 in range(8):
                for c, (_, st, _, _, _, _, _, _, rev, sa_keep, vc_keep) in enumerate(chains):
                    q = 7 - i8 if rev else i8
                    v, k, a, b, w = tiles[c]
                    S = states[c]
                    sa = _seg_mm(S * _rows4(a, q), ones2)
                    sa_keep[gi * 8 + i8] = sa
                    S = S * _rows4(w, q) + sa * _rows4(b, q) + vc_keep[gi * 8 + i8] * _rows4(k, q)
                    st[gi * 8 + i8 + 1] = S
                    states[c] = S
            return carry

        lax.fori_loop(0, nG, recompute, 0)

        def back(gg, carry):
            gi = nG - 1 - gg
            tiles, grads = [], []
            for n, st, dS_ref, _, ins, _, big, _, rev, _, _ in chains:
                base = _time_base(gi, nG, rev)
                t = [ref[n, pl.ds(base, 8), :] for ref in ins]
                tiles.append(t)
                grads.append(dS_ref[...])
                big[0] = cols8(t[6])
            for i8 in range(7, -1, -1):
                for c, (_, st, _, _, _, _, big, tile, rev, sa_keep, vc_keep) in enumerate(chains):
                    q = 7 - i8 if rev else i8
                    r, v, k, a, b, w, _ = tiles[c]
                    i = gi * 8 + i8
                    S_prev, S_t = st[i], st[i + 1]
                    rows = slice(256 * q, 256 * (q + 1))
                    dy_col, v_col, sa = big[0, rows, :], vc_keep[i], sa_keep[i]
                    dS = grads[c] + dy_col * _rows4(r, q)
                    sb = _seg_mm(dS * _rows4(b, q), ones2)
                    big[1, rows, :] = dS * _rows4(k, q)
                    tile[0, q:q + 1, :] = _colsum4(S_t * dy_col)
                    tile[2, q:q + 1, :] = _colsum4(dS * v_col)
                    tile[3, q:q + 1, :] = _colsum4(S_prev * sb)
                    tile[4, q:q + 1, :] = _colsum4(dS * sa)
                    tile[5, q:q + 1, :] = _colsum4(S_prev * dS)
                    grads[c] = dS * _rows4(w, q) + sb * _rows4(a, q)
            for c, (n, _, dS_ref, _, _, outs, big, tile, rev, _, _) in enumerate(chains):
                dS_ref[...] = grads[c]
                for q in range(8):
                    tile[1, q:q + 1, :] = _head_sums_row(big[1, 256 * q:256 * (q + 1), :], heads2)
                base = _time_base(gi, nG, rev)
                for o, o_ref in enumerate(outs):
                    o_ref[n, pl.ds(base, 8), :] = tile[o]
            return carry

        lax.fori_loop(0, nG, back, 0)

    row_io = lambda at, col: pl.BlockSpec((nb, Tc, 512), lambda g, j: (g, at(j), col))
    in_specs = []
    for at in (f_at, b_at):
        in_specs += [row_io(at, 0), row_io(at, 2)] + [row_io(at, 0)] * 5
        in_specs.append(pl.BlockSpec((nb, 256, 128), lambda g, j, at=at: (g, at(j), 0)))
    in_specs.append(pl.BlockSpec((2048, 128), lambda g, j: (0, 0)))
    out_specs = [row_io(f_at, 0)] * 6 + [row_io(b_at, 0)] * 6
    big = pltpu.VMEM((2, 8 * 256, 128), F32)
    states = pltpu.VMEM((Tc + 1, 256, 128), F32)
    per_step = pltpu.VMEM((Tc, 256, 128), F32)
    one_slot = [states, states, per_step, per_step, per_step, per_step,
                pltpu.VMEM((256, 128), F32), pltpu.VMEM((256, 128), F32), big, big,
                pltpu.VMEM((6, 8, 512), F32), pltpu.VMEM((6, 8, 512), F32)]
    s3 = s.reshape(B, T, s.shape[-1])
    seq = lambda a: a.reshape(B, T, 512)
    ck3 = lambda a: a.reshape(B, nT * 256, 128)
    outs = pl.pallas_call(
        body, name="rwkv_scan_bwd", grid=(B // nb, nT),
        out_shape=[jax.ShapeDtypeStruct((B, T, 512), F32)] * 12,
        in_specs=in_specs, out_specs=out_specs,
        scratch_shapes=one_slot * nb,
        compiler_params=_cparams(),
    )(s3, s3, seq(k2), seq(na), seq(bb), seq(wf), seq(dy), ck3(ckF),
      s3, s3, seq(k2), seq(na), seq(bb), seq(wb), seq(dy), ck3(ckB), _diag_matrix())
    return [o.reshape(N, 512) for o in outs]


def _cat_shards(g4, name, axis):
    return jnp.concatenate([g4[s][name] for s in range(4)], axis=axis)


def _split_shards(full, axis):
    return jnp.split(full, 4, axis=axis)


def kernel(x, norm1_g, w_in, gla_wa2_f, gla_ba_f, gla_wa2_b, gla_ba_b, gla_norm_g, gla_proj, rwkv_mu_prev, rwkv_mu_next, rwkv_w0_f, rwkv_w2_f, rwkv_w0_b, rwkv_w2_b, rwkv_a0, rwkv_a2, rwkv_g2, rwkv_k_k, rwkv_k_a, rwkv_r_k, rwkv_ln_w, rwkv_ln_b, rwkv_proj, w_out, norm2_g, ffn_up, ffn_conv_w, ffn_conv_b, ffn_down, norm_f_g, loss_target, m_norm1_g, m_w_in, m_gla_wa2_f, m_gla_ba_f, m_gla_wa2_b, m_gla_ba_b, m_gla_norm_g, m_gla_proj, m_rwkv_mu_prev, m_rwkv_mu_next, m_rwkv_w0_f, m_rwkv_w2_f, m_rwkv_w0_b, m_rwkv_w2_b, m_rwkv_a0, m_rwkv_a2, m_rwkv_g2, m_rwkv_k_k, m_rwkv_k_a, m_rwkv_r_k, m_rwkv_ln_w, m_rwkv_ln_b, m_rwkv_proj, m_w_out, m_norm2_g, m_ffn_up, m_ffn_conv_w, m_ffn_conv_b, m_ffn_down, m_norm_f_g, v_norm1_g, v_w_in, v_gla_wa2_f, v_gla_ba_f, v_gla_wa2_b, v_gla_ba_b, v_gla_norm_g, v_gla_proj, v_rwkv_mu_prev, v_rwkv_mu_next, v_rwkv_w0_f, v_rwkv_w2_f, v_rwkv_w0_b, v_rwkv_w2_b, v_rwkv_a0, v_rwkv_a2, v_rwkv_g2, v_rwkv_k_k, v_rwkv_k_a, v_rwkv_r_k, v_rwkv_ln_w, v_rwkv_ln_b, v_rwkv_proj, v_w_out, v_norm2_g, v_ffn_up, v_ffn_conv_w, v_ffn_conv_b, v_ffn_down, v_norm_f_g):
    args = locals()
    weights = {n: args[n] for n in WEIGHT_ORDER}
    mom_m = {n: args["m_" + n] for n in WEIGHT_ORDER}
    mom_v = {n: args["v_" + n] for n in WEIGHT_ORDER}
    shapes = {n: weights[n].shape for n in WEIGHT_ORDER}
    B, T, _ = x.shape
    N = B * T
    tm = _pick(N, (512,))

    def local(d):
        sh = {n: d[n].reshape(s) for n, s, _ in SHARDED}
        rp = {n: d[n].reshape(-1) for n, _ in REPLICATED}
        return sh, rp

    w_loc, m_loc, v_loc = local(weights), local(mom_m), local(mom_v)

    small_of = lambda loc: _pack(loc[0], _pack_replicated(loc[1]))
    w_small = small_of(w_loc)
    gathered = _allgather_chips([w_loc[0][n].astype(BF16) for n, _, _ in BIG] + [w_small])
    small_vals = [_unpack(gathered[-1][s]) for s in range(4)]
    W = {n: jnp.concatenate([gathered[i][s] for s in range(4)], axis=ax) for i, (n, _, ax) in enumerate(BIG)}
    W.update({n: _cat_shards(small_vals, n, ax) for n, _, ax in SMALL})
    R = {n: weights[n].reshape(1, -1) for n, _ in REPLICATED}

    zc = lambda r, c, dt=F32: jnp.zeros((r, c), dt)
    w_in_full = W["w_in"]
    w_in_p = jnp.concatenate([w_in_full[:, 0:1536], w_in_full[:, 1568:3104], w_in_full[:, 3360:5408],
                              w_in_full[:, 3104:3360], w_in_full[:, 1536:1568],
                              zc(1024, PROJ_PAD - N_PROJ, BF16)], axis=1)
    w_in_b = w_in_p
    pad_ff = lambda a: jnp.concatenate([a[:, :D_FF], zc(a.shape[0], FF_PAD - D_FF, a.dtype), a[:, D_FF:],
                                        zc(a.shape[0], FF_PAD - D_FF, a.dtype)], axis=1)
    ffn_up_p = pad_ff(W["ffn_up"])
    ffn_up_b = ffn_up_p
    conv_w_p = pad_ff(W["ffn_conv_w"])
    conv_b_p = pad_ff(R["ffn_conv_b"])
    ffn_down_p = jnp.concatenate([W["ffn_down"], zc(FF_PAD - D_FF, 1024, BF16)], axis=0)
    ffn_down_b = ffn_down_p
    w_out_b = W["w_out"]
    gla_proj_b = W["gla_proj"]
    rwkv_proj_b = W["rwkv_proj"]
    wa2 = jnp.stack([jnp.concatenate([W["gla_wa2_f"], zc(112, 256)], axis=0),
                     jnp.concatenate([zc(16, 256), W["gla_wa2_b"], zc(96, 256)], axis=0)])
    ba = jnp.stack([R["gla_ba_f"], R["gla_ba_b"]])
    w2_f = jnp.concatenate([W["rwkv_w2_f"], zc(64, 512)], axis=0)
    w2_b = jnp.concatenate([W["rwkv_w2_b"], zc(64, 512)], axis=0)
    a2 = jnp.concatenate([zc(64, 512), W["rwkv_a2"]], axis=0)
    g2 = W["rwkv_g2"]
    head_ones = np.kron(np.eye(8, dtype=np.float32), np.ones((64, 64), np.float32))
    seg64 = jnp.asarray(np.concatenate([head_ones, head_ones], axis=0), dtype=BF16)

    x2d = x.reshape(N, D_MODEL)
    tgt = loss_target.reshape(N, D_MODEL)

    (h1,) = _rowwise("norm1", _fn_norm, [(x2d, 1024, 0)], [R["norm1_g"]], [], [(1024, BF16)], N, tm)
    p = _matmul(h1, w_in_b, "proj_in")
    o_gla, gla_states = _gla_fwd(p, wa2, ba, B, T)
    gla_post_rows = [(o_gla, 512, 0, 0), (o_gla, 512, 0, 1), (p, 512, 2)]
    (gated,) = _rowwise("gla_post", _fn_gla_post, gla_post_rows, [R["gla_norm_g"]], [], [(512, BF16)], N, tm)
    y_a = _matmul(gated, gla_proj_b, "gla_out")
    s = _token_shift(p, R["rwkv_mu_prev"], R["rwkv_mu_next"], B, T)
    pre_rows = [(s, 512, 1), (s, 256, 6)]
    pre_params = [R["rwkv_w0_f"], R["rwkv_w0_b"], R["rwkv_a0"], R["rwkv_k_k"], R["rwkv_k_a"], w2_f, w2_b, a2, g2]
    wf, wb, k2, na, bb, g = _rowwise("rwkv_pre", _fn_rwkv_pre, pre_rows, pre_params, [seg64],
                                     [(512, F32)] * 6, N, tm)
    y_f, y_b, ck_f, ck_b = _scan_fwd_mxu(s, wf, wb, k2, na, bb, B, T)
    post_rows = [(y_f, 512, 0), (y_b, 512, 0), (s, 512, 0), (k2, 512, 0), (s, 512, 2), (g, 512, 0)]
    post_params = [R["rwkv_ln_w"], R["rwkv_ln_b"], R["rwkv_r_k"]]
    (o_rwkv,) = _rowwise("rwkv_post", _fn_rwkv_post, post_rows, post_params, [seg64], [(512, BF16)], N, tm)
    y_r = _matmul(o_rwkv, rwkv_proj_b, "rwkv_out")
    merge_rows = [(p, 1024, 3), (p, 1024, 4), (y_a, 1024, 0), (y_r, 1024, 0)]
    (merged,) = _rowwise("merge", _fn_merge, merge_rows, [], [], [(1024, BF16)], N, tm)
    x1 = _matmul(merged, w_out_b, "mix_out", residual=x2d)
    (h2,) = _rowwise("norm2", _fn_norm, [(x1, 1024, 0)], [R["norm2_g"]], [], [(1024, BF16)], N, tm)
    u = _matmul(h2, ffn_up_b, "ffn_up")
    act = _ffn_conv(u, conv_w_p, conv_b_p, B, T)
    x2 = _matmul(act, ffn_down_b, "ffn_down", residual=x1)
    loss_blk, dx2, dx2_b, d_norm_f = _loss_head(x2, tgt, weights["norm_f_g"].reshape(1, -1))

    d_act = _matmul(dx2_b, ffn_down_p.T.astype(BF16), "d_act")
    d_ffn_down = _matmul_tn(act, dx2_b, "dw_ffn_down")[:D_FF]
    du_g, du_v, d_conv_w_p, d_conv_b_p = _ffn_conv_bwd(u, d_act, conv_w_p, conv_b_p, B, T)
    up_t = ffn_up_p.T.astype(BF16)
    d_h2 = _matmul(du_v, up_t[FF_PAD:], "d_h2_v", residual=_matmul(du_g, up_t[:FF_PAD], "d_h2_g"))
    d_ffn_up = jnp.concatenate([_matmul_tn(h2, du_g, "dw_ffn_up_g")[:, :D_FF],
                                _matmul_tn(h2, du_v, "dw_ffn_up_v")[:, :D_FF]], axis=1)
    unpad_ff = lambda a: jnp.concatenate([a[:, :D_FF], a[:, FF_PAD:FF_PAD + D_FF]], axis=1)
    (dx1,), (d_norm2,) = _rowwise_bwd("norm2_bwd", _fn_norm, [(x1, 1024, 0)], [R["norm2_g"]], [],
                                      [[(d_h2, 1024, 0)]], [(F32, (dx2, 1024, 0))], N, tm)
    dx1_b = dx1.astype(BF16)
    d_merged = _matmul(dx1_b, W["w_out"].T.astype(BF16), "d_merged")
    d_w_out = _matmul_tn(merged, dx1_b, "dw_out")
    (d_ga, d_gb, d_ya, d_yr), _ = _rowwise_bwd("merge_bwd", _fn_merge, merge_rows, [], [],
                                               [[(d_merged, 1024, 0)]], [(BF16, None)] * 4, N, tm)
    d_o_rwkv = _matmul(d_yr, W["rwkv_proj"].T.astype(BF16), "d_o_rwkv")
    d_rwkv_proj = _matmul_tn(o_rwkv, d_yr, "dw_rwkv_proj")
    (d_y, d_r_bonus, d_k2_bonus, d_v_bonus, d_g), (d_ln_w, d_ln_b, d_r_k) = _rowwise_bwd(
        "rwkv_post_bwd", _fn_rwkv_post, post_rows, post_params, [seg64], [[(d_o_rwkv, 512, 0)]],
        [(F32, None), None, (F32, None), (F32, None), (F32, None), (F32, None)], N, tm)
    (drF, dvF, dkF, daF, dbF, dwF, drB, dvB, dkB, daB, dbB, dwB) = _scan_bwd_mxu(s, wf, wb, k2, na, bb, d_y, ck_f, ck_b, B, T)
    pre_cts = [[(dwF, 512, 0)], [(dwB, 512, 0)], [(dkF, 512, 0), (dkB, 512, 0), (d_k2_bonus, 512, 0)],
               [(daF, 512, 0), (daB, 512, 0)], [(dbF, 512, 0), (dbB, 512, 0)], [(d_g, 512, 0)]]
    (ds_k, ds_wag), pre_grads = _rowwise_bwd("rwkv_pre_bwd", _fn_rwkv_pre, pre_rows, pre_params, [seg64], pre_cts,
                                             [(F32, None), (F32, None)], N, tm)
    d_w0_f, d_w0_b, d_a0, d_k_k, d_k_a, d_w2_f, d_w2_b, d_a2, d_g2 = pre_grads
    ds = jnp.concatenate([drF + drB + d_r_bonus, ds_k, dvF + dvB + d_v_bonus, ds_wag], axis=1)
    d_p_rwkv, d_mu_prev, d_mu_next = _token_shift_bwd(p, ds, R["rwkv_mu_prev"], R["rwkv_mu_next"], B, T)
    d_gated = _matmul(d_ya, W["gla_proj"].T.astype(BF16), "d_gated")
    d_gla_proj = _matmul_tn(gated, d_ya, "dw_gla_proj")
    (d_o, d_og), (d_gla_norm,) = _rowwise_bwd(
        "gla_post_bwd", _fn_gla_post, gla_post_rows, [R["gla_norm_g"]], [], [[(d_gated, 512, 0)]],
        [(F32, None), None, (BF16, None)], N, tm)
    dqkv2, dafab2, d_wa2, d_ba = _gla_bwd(p, d_o, gla_states, wa2, ba, B, T)
    add2 = lambda a, b: (a + b,)
    (d_qkv,) = _rowwise("sum_dqkv", add2, [(dqkv2, 1024, 0, 0), (dqkv2, 1024, 0, 1)], [], [], [(1024, BF16)], N, tm)
    (d_afab,) = _rowwise("sum_dafab", add2, [(dafab2, 128, 0, 0), (dafab2, 128, 0, 1)], [], [], [(128, BF16)], N, tm)
    w_in_t = w_in_p.T.astype(BF16)
    w_rwkv_t = jnp.concatenate([w_in_t[1536:3072], w_in_t[5120:5376]], axis=0)
    d_h1 = _matmul(d_qkv, w_in_t[0:1024], "d_h1_qkv")
    d_h1 = _matmul(d_og, w_in_t[1024:1536], "d_h1_og", residual=d_h1)
    d_h1 = _matmul(d_p_rwkv, w_rwkv_t, "d_h1_rwkv", residual=d_h1)
    d_h1 = _matmul(d_ga, w_in_t[3072:4096], "d_h1_ga", residual=d_h1)
    d_h1 = _matmul(d_gb, w_in_t[4096:5120], "d_h1_gb", residual=d_h1)
    d_h1 = _matmul(d_afab, w_in_t[5376:5504], "d_h1_afab", residual=d_h1)
    d_w_in = jnp.concatenate([
        _matmul_tn(h1, d_qkv, "dw_in_qkv"), _matmul_tn(h1, d_og, "dw_in_og"),
        _matmul_tn(h1, d_afab, "dw_in_afab")[:, :32], _matmul_tn(h1, d_p_rwkv, "dw_in_rwkv"),
        _matmul_tn(h1, d_ga, "dw_in_ga"), _matmul_tn(h1, d_gb, "dw_in_gb")], axis=1)
    (grad_x,), (d_norm1,) = _rowwise_bwd("norm1_bwd", _fn_norm, [(x2d, 1024, 0)], [R["norm1_g"]], [],
                                         [[(d_h1, 1024, 0)]], [(F32, (dx1, 1024, 0))], N, tm)

    full_grads = {
        "w_in": d_w_in, "gla_wa2_f": d_wa2[0, 0:16], "gla_wa2_b": d_wa2[1, 16:32], "gla_proj": d_gla_proj,
        "rwkv_w2_f": d_w2_f[0:64], "rwkv_w2_b": d_w2_b[0:64], "rwkv_a2": d_a2[64:128], "rwkv_g2": d_g2,
        "rwkv_proj": d_rwkv_proj, "w_out": d_w_out, "ffn_up": d_ffn_up, "ffn_conv_w": unpad_ff(d_conv_w_p),
        "ffn_down": d_ffn_down,
    }
    repl_grads = {
        "norm1_g": d_norm1, "gla_ba_f": d_ba[0], "gla_ba_b": d_ba[1], "gla_norm_g": d_gla_norm,
        "rwkv_mu_prev": d_mu_prev, "rwkv_mu_next": d_mu_next, "rwkv_w0_f": d_w0_f, "rwkv_w0_b": d_w0_b,
        "rwkv_a0": d_a0, "rwkv_k_k": d_k_k, "rwkv_k_a": d_k_a, "rwkv_r_k": d_r_k, "rwkv_ln_w": d_ln_w,
        "rwkv_ln_b": d_ln_b, "norm2_g": d_norm2, "ffn_conv_b": unpad_ff(d_conv_b_p), "norm_f_g": d_norm_f,
    }
    split = {n: _split_shards(full_grads[n], ax) for n, _, ax in SHARDED}
    to_owners = [jnp.stack([p.astype(BF16) for p in split[n]]) for n, _, _ in BIG]
    repl_flat = _pack_replicated(repl_grads, loss_blk[0, 0])
    to_owners.append(jnp.stack([_pack({n: split[n][sidx] for n, _, _ in SMALL}, repl_flat) for sidx in range(4)]))
    received = _exchange_chips(to_owners, "scatter_grads", gather=False)
    names = [n for n, _, _ in BIG] + ["small"]
    mine = [_sum_sources(r, "sum_" + n) for r, n in zip(received, names)]
    other = _swap_with_sibling(mine)
    packs = [w_small, small_of(m_loc), small_of(v_loc)]
    results = {}
    for i, n in enumerate(names):
        wmv = packs if n == "small" else [d[0][n] for d in (w_loc, m_loc, v_loc)]
        results[n] = _adamw(mine[i], other[i], *wmv, "adamw_" + n)
    small = [_unpack(f) for f in results["small"]]
    outs = [small[0]["loss"], grad_x.reshape(B, T, D_MODEL)]
    for kind in range(4):
        for n in WEIGHT_ORDER:
            val = results[n][kind] if n in results else small[kind][n]
            outs.append(val.reshape(shapes[n]))
    return tuple(outs)
```
